```python
import math
import jax
import jax.numpy as jnp
from jax import lax
import numpy as np

D_MODEL = 1024
BATCH = 8
SEQ = 2048
DEPTH = 4

CTX_LEN = 256
GRID_W = 64
NORM_EPS = 1e-6

MLA_HEADS = 8
MLA_NOPE = 64
MLA_ROPE = 32
MLA_QK = MLA_NOPE + MLA_ROPE
MLA_V = 64
MLA_Q_LORA = 256
MLA_KV_LORA = 128
MLA_WIDTH = MLA_HEADS * MLA_V
ROPE_BASE = 10000.0
BLOCK_Q = 128

RW_HEADS = 8
RW_HEAD = 64
RW_WIDTH = RW_HEADS * RW_HEAD
RW_LORA_W = 64
RW_LORA_A = 64
RW_SHIFT = 3 * RW_WIDTH + 2 * RW_LORA_W + 2 * RW_LORA_A
RW_GN_EPS = 64e-5

EV_MIX = MLA_WIDTH + RW_WIDTH
EV_DQ = MLA_Q_LORA
EV_DKV = MLA_KV_LORA + MLA_ROPE
EV_IN = EV_DQ + EV_DKV + RW_SHIFT + EV_MIX

HY_WIDTH = D_MODEL
HY_IN = 4 * HY_WIDTH
HY_ORDER = 64
HY_BANDS = 16
HY_EMB = 1 + 2 * HY_BANDS
HY_INNER = 2
HY_FAST_DECAY = 0.3
HY_SLOW_DECAY = 1.5
HY_TARGET = 1e-2

N_EVEN = (DEPTH + 1) // 2
N_ODD = DEPTH // 2

kernel_name = 'hybrid_mla_rwkv7_hyena_prefix_block'


def rms_norm(t, g):
    tf = t.astype(jnp.float32)
    tf = tf * lax.rsqrt(jnp.mean(tf * tf, axis=-1, keepdims=True) + NORM_EPS)
    return (tf * g).astype(t.dtype)


def axial_rope_tables(seq_len):
    rows = seq_len // GRID_W
    row = jnp.repeat(jnp.arange(rows, dtype=jnp.float32), GRID_W)
    col = jnp.tile(jnp.arange(GRID_W, dtype=jnp.float32), rows)
    n_freq = MLA_ROPE // 4
    inv = ROPE_BASE ** (-jnp.arange(n_freq, dtype=jnp.float32) / n_freq)
    ang = jnp.concatenate([row[:, None] * inv, col[:, None] * inv], axis=-1)
    return jnp.cos(ang), jnp.sin(ang)


def apply_rope(t, cos, sin):
    tf = t.astype(jnp.float32).reshape(t.shape[:-1] + (MLA_ROPE // 2, 2))
    a, b = tf[..., 0], tf[..., 1]
    cs, sn = cos[:, None, :], sin[:, None, :]
    out = jnp.stack([a * cs - b * sn, a * sn + b * cs], axis=-1).reshape(t.shape)
    return out.astype(t.dtype)


def rope_tail(t, rope):
    return jnp.concatenate([t[..., :MLA_NOPE], apply_rope(t[..., MLA_NOPE:], *rope)], axis=-1)


def mla_queries(p_dq, q_a_norm, w_uq, q_norm, rope):
    q = jnp.einsum('blr,rhd->blhd', rms_norm(p_dq, q_a_norm), w_uq)
    q = rms_norm(q, q_norm)
    return q if rope is None else rope_tail(q, rope)


def mla_keys_values(p_dkv, kv_a_norm, w_ukv, k_norm, rope):
    B, L, _ = p_dkv.shape
    kv = jnp.einsum('blr,rhd->blhd', rms_norm(p_dkv[..., :MLA_KV_LORA], kv_a_norm), w_ukv)
    k_nope, v = kv[..., :MLA_NOPE], kv[..., MLA_NOPE:]
    k_rope = jnp.broadcast_to(p_dkv[:, :, None, MLA_KV_LORA:], (B, L, MLA_HEADS, MLA_ROPE))
    k = rms_norm(jnp.concatenate([k_nope, k_rope], axis=-1), k_norm)
    return (k if rope is None else rope_tail(k, rope)), v


def attend(q, k, v):
    s = jnp.einsum('bqhd,bkhd->bhqk', q, k, preferred_element_type=jnp.float32) * (q.shape[-1] ** -0.5)
    p = jax.nn.softmax(s, axis=-1)
    return jnp.einsum('bhqk,bkhd->bqhd', p.astype(v.dtype), v)


def blocked_attention(q, k, v):
    B, L, H, Dq = q.shape
    nb = L // BLOCK_Q
    qb = q.reshape(B, nb, BLOCK_Q, H, Dq).transpose(1, 0, 2, 3, 4)
    ob = lax.map(lambda blk: attend(blk, k, v), qb)
    return ob.transpose(1, 0, 2, 3, 4).reshape(B, L, H, v.shape[-1])


def bidir_shift(p, mu_prev, mu_next):
    zero = jnp.zeros_like(p[:, :1])
    prev = jnp.concatenate([zero, p[:, :-1]], axis=1)
    nxt = jnp.concatenate([p[:, 1:], zero], axis=1)
    return p + (prev - p) * mu_prev + (nxt - p) * mu_next


def rwkv_streams(ps, w0, w_up, a0, a_up, k_k, k_a):
    B, L, _ = ps.shape
    ps = ps.astype(jnp.float32)
    cut = [RW_WIDTH, 2 * RW_WIDTH, 3 * RW_WIDTH, 3 * RW_WIDTH + 2 * RW_LORA_W]
    r, k, v, wd, ad = jnp.split(ps, cut, axis=-1)
    wd = jnp.tanh(wd.reshape(B, L, 2, RW_LORA_W))
    ad = ad.reshape(B, L, 2, RW_LORA_A)
    w_log = -jax.nn.softplus(-(w0 + jnp.einsum('bldr,drc->bldc', wd, w_up))) - 0.5
    decay = jnp.exp(-jnp.exp(w_log))
    a = jax.nn.sigmoid(a0 + jnp.einsum('bldr,drc->bldc', ad, a_up))
    kk = (k * k_k).reshape(B, L, RW_HEADS, RW_HEAD)
    kk = kk * lax.rsqrt(jnp.maximum(jnp.sum(kk * kk, axis=-1, keepdims=True), 1e-24))
    kd = k[:, :, None, :] * (1.0 + (a - 1.0) * k_a)
    hd = lambda t: t.reshape(t.shape[:-1] + (RW_HEADS, RW_HEAD))
    return hd(r), hd(v), kk, hd(decay), hd(a), hd(kd)


def rwkv_scan(r, w, k, v, kk, a, s0, reverse):
    def step(S, inp):
        r_t, w_t, k_t, v_t, kk_t, a_t = inp
        sa = jnp.einsum('bhvk,bhk->bhv', S, -kk_t)
        S = S * w_t[:, :, None, :] + sa[..., None] * (kk_t * a_t)[:, :, None, :] + v_t[..., None] * k_t[:, :, None, :]
        return S, jnp.einsum('bhvk,bhk->bhv', S, r_t)
    xs = tuple(jnp.moveaxis(t, 1, 0) for t in (r, w, k, v, kk, a))
    S, out = lax.scan(step, s0, xs, reverse=reverse)
    return S, jnp.moveaxis(out, 0, 1)


def rwkv_run(streams, s0s):
    r, v, kk, decay, a, kd = streams
    outs, finals = [], []
    for d in range(2):
        S, o = rwkv_scan(r, decay[:, :, d], kd[:, :, d], v, kk, a[:, :, d], s0s[d], reverse=(d == 1))
        outs.append(o)
        finals.append(S)
    return outs, finals


def rwkv_finish(streams, outs, r_k, ln_w, ln_b):
    r, v, kk, decay, a, kd = streams
    B, L = r.shape[:2]
    o = outs[0] + outs[1]
    mu = jnp.mean(o, axis=-1, keepdims=True)
    var = jnp.mean(jnp.square(o - mu), axis=-1, keepdims=True)
    o = ((o - mu) * lax.rsqrt(var + RW_GN_EPS)).reshape(B, L, RW_WIDTH) * ln_w + ln_b
    bonus = jnp.sum(jnp.sum(r[:, :, None] * kd * r_k, axis=-1, keepdims=True), axis=2) * v
    return o + bonus.reshape(B, L, RW_WIDTH)


def merge_heads(o_mla, o_rw, g, w_out):
    B, L = g.shape[:2]
    o = jnp.concatenate([o_mla.reshape(B, L, MLA_WIDTH).astype(g.dtype), o_rw.astype(g.dtype)], axis=-1)
    return (o * jax.nn.silu(g)) @ w_out


def even_mixer(h, hc, rope, ctx_out, w_in, w_out, mla_p, rw_p, rw_out_p):
    cut = [EV_DQ, EV_DQ + EV_DKV, EV_DQ + EV_DKV + RW_SHIFT]
    p_dq, p_dkv, p_rw, g = jnp.split(h @ w_in, cut, axis=-1)
    pc_dq, pc_dkv, pc_rw, gc = jnp.split(hc @ w_in, cut, axis=-1)
    q_a_norm, w_uq, kv_a_norm, w_ukv, q_norm, k_norm = mla_p
    mu_prev, mu_next = rw_p[0], rw_p[1]
    q = mla_queries(p_dq, q_a_norm, w_uq, q_norm, rope)
    k, v = mla_keys_values(p_dkv, kv_a_norm, w_ukv, k_norm, rope)
    kc, vc = mla_keys_values(pc_dkv, kv_a_norm, w_ukv, k_norm, None)
    o_mla = blocked_attention(q, jnp.concatenate([k, kc], axis=1), jnp.concatenate([v, vc], axis=1))
    st_c = rwkv_streams(bidir_shift(pc_rw, mu_prev, mu_next), *rw_p[2:])
    st = rwkv_streams(bidir_shift(p_rw, mu_prev, mu_next), *rw_p[2:])
    s0 = jnp.zeros((hc.shape[0], RW_HEADS, RW_HEAD, RW_HEAD), jnp.float32)
    outs_c, fin_c = rwkv_run(st_c, (s0, s0))
    outs, _ = rwkv_run(st, fin_c)
    o_rw = rwkv_finish(st, outs, *rw_out_p)
    y = merge_heads(o_mla, o_rw, g, w_out)
    if not ctx_out:
        return y, None
    qc = mla_queries(pc_dq, q_a_norm, w_uq, q_norm, None)
    oc_mla = attend(qc, kc, vc)
    oc_rw = rwkv_finish(st_c, outs_c, *rw_out_p)
    return y, merge_heads(oc_mla, oc_rw, gc, w_out)


def centred_conv3(u, w, b):
    up = jnp.pad(u, ((0, 0), (1, 1), (0, 0)))
    return up[:, :-2] * w[0] + up[:, 1:-1] * w[1] + up[:, 2:] * w[2] + b


def hyena_filters(L, f_w1, f_b1, f_w2, f_b2, f_wout, freq):
    f32 = jnp.float32
    pos = jnp.arange(L, dtype=f32)[:, None]
    t = pos / (L - 1)
    bands = jnp.linspace(1e-4, HY_BANDS - 1, HY_BANDS, dtype=f32)
    ang = pos * (2.0 * math.pi / L) * bands
    z = jnp.concatenate([t, jnp.cos(ang), -jnp.sin(ang)], axis=-1)
    hdn = jnp.sin(freq * (z @ f_w1 + f_b1))
    for j in range(HY_INNER):
        hdn = jnp.sin(freq * (hdn @ f_w2[j] + f_b2[j]))
    filt = (hdn @ f_wout).reshape(L, 2, HY_WIDTH)
    deltas = jnp.abs(jnp.linspace(math.log(HY_TARGET) / HY_FAST_DECAY, math.log(HY_TARGET) / HY_SLOW_DECAY, HY_WIDTH, dtype=f32))
    filt = filt * jnp.exp(-t[:, :, None] * deltas)
    full = jnp.concatenate([filt[:, 0], jnp.zeros((1, HY_WIDTH), f32), filt[:0:-1, 1]], axis=0)
    return full / jnp.sum(jnp.abs(full), axis=0, keepdims=True)


def hyena_mixer(h, w_in, w_out, conv_w, conv_b, bias_d, f_w1, f_b1, f_w2, f_b2, f_wout, freq):
    L = h.shape[1]
    p = h @ w_in
    u, g = p[..., :3 * HY_WIDTH], p[..., 3 * HY_WIDTH:]
    x0, x1, v = jnp.split(centred_conv3(u, conv_w, conv_b), 3, axis=-1)
    v = (v * x1).astype(jnp.float32)
    filt = hyena_filters(L, f_w1, f_b1, f_w2, f_b2, f_wout, freq)
    n = 2 * L
    y = jnp.fft.irfft(jnp.fft.rfft(v, n=n, axis=1) * jnp.fft.rfft(filt, n=n, axis=0)[None], n=n, axis=1)[:, :L]
    y = (y + v * bias_d).astype(h.dtype) * x0
    return (y * jax.nn.silu(g)) @ w_out


def setup_inputs(seed: int = 0) -> dict:
    key = jax.random.key(seed)
    ks = iter(jax.random.split(key, 48))
    f32 = jnp.float32
    D = D_MODEL

    def nrm(shape, scale=1.0):
        return scale * jax.random.normal(next(ks), shape, f32)

    def gain(shape):
        return 1.0 + nrm(shape, 0.02)

    def unif(shape, lo, hi):
        return jax.random.uniform(next(ks), shape, f32, lo, hi)

    return {
        'x': nrm((BATCH, SEQ, D)),
        'c': nrm((BATCH, D)),
        'ctx': nrm((BATCH, CTX_LEN, D)),
        'c_ctx': nrm((D,)),
        'mod_w': nrm((DEPTH, D, 3 * D), 0.5 * D ** -0.5),
        'mod_b': nrm((DEPTH, 3 * D), 0.02),
        'norm_g': gain((DEPTH, D)),
        'ev_w_in': nrm((N_EVEN, D, EV_IN), D ** -0.5),
        'ev_w_out': nrm((N_EVEN, EV_MIX, D), EV_MIX ** -0.5),
        'mla_q_a_norm': gain((N_EVEN, MLA_Q_LORA)),
        'mla_w_uq': nrm((N_EVEN, MLA_Q_LORA, MLA_HEADS, MLA_QK), MLA_Q_LORA ** -0.5),
        'mla_kv_a_norm': gain((N_EVEN, MLA_KV_LORA)),
        'mla_w_ukv': nrm((N_EVEN, MLA_KV_LORA, MLA_HEADS, MLA_NOPE + MLA_V), MLA_KV_LORA ** -0.5),
        'mla_q_norm': gain((N_EVEN, MLA_QK)),
        'mla_k_norm': gain((N_EVEN, MLA_QK)),
        'rwkv_mu_prev': unif((N_EVEN, RW_SHIFT), 0.0, 0.5),
        'rwkv_mu_next': unif((N_EVEN, RW_SHIFT), 0.0, 0.5),
        'rwkv_w0': unif((N_EVEN, 2, RW_WIDTH), -4.0, 1.0),
        'rwkv_w_up': nrm((N_EVEN, 2, RW_LORA_W, RW_WIDTH), 0.1),
        'rwkv_a0': nrm((N_EVEN, 2, RW_WIDTH), 0.1),
        'rwkv_a_up': nrm((N_EVEN, 2, RW_LORA_A, RW_WIDTH), 0.1),
        'rwkv_k_k': 0.85 + nrm((N_EVEN, RW_WIDTH), 0.02),
        'rwkv_k_a': gain((N_EVEN, RW_WIDTH)),
        'rwkv_r_k': nrm((N_EVEN, 2, RW_HEADS, RW_HEAD), 0.1),
        'rwkv_ln_w': gain((N_EVEN, RW_WIDTH)),
        'rwkv_ln_b': nrm((N_EVEN, RW_WIDTH), 0.02),
        'od_w_in': nrm((N_ODD, D, HY_IN), D ** -0.5),
        'od_w_out': nrm((N_ODD, HY_WIDTH, D), HY_WIDTH ** -0.5),
        'hy_conv_w': nrm((N_ODD, 3, 3 * HY_WIDTH), 3 ** -0.5),
        'hy_conv_b': nrm((N_ODD, 3 * HY_WIDTH), 0.02),
        'hy_bias_d': nrm((N_ODD, HY_WIDTH), 0.5),
        'hy_f_w1': nrm((N_ODD, HY_EMB, HY_ORDER), HY_EMB ** -0.5),
        'hy_f_b1': nrm((N_ODD, HY_ORDER), 0.1),
        'hy_f_w2': nrm((N_ODD, HY_INNER, HY_ORDER, HY_ORDER), HY_ORDER ** -0.5),
        'hy_f_b2': nrm((N_ODD, HY_INNER, HY_ORDER), 0.1),
        'hy_f_wout': nrm((N_ODD, HY_ORDER, 2 * HY_WIDTH), HY_ORDER ** -0.5),
        'hy_freq': gain((N_ODD, HY_ORDER)),
    }


def reference(x, c, ctx, c_ctx, mod_w, mod_b, norm_g,
              ev_w_in, ev_w_out, mla_q_a_norm, mla_w_uq, mla_kv_a_norm, mla_w_ukv, mla_q_norm, mla_k_norm,
              rwkv_mu_prev, rwkv_mu_next, rwkv_w0, rwkv_w_up, rwkv_a0, rwkv_a_up, rwkv_k_k, rwkv_k_a,
              rwkv_r_k, rwkv_ln_w, rwkv_ln_b,
              od_w_in, od_w_out, hy_conv_w, hy_conv_b, hy_bias_d, hy_f_w1, hy_f_b1, hy_f_w2, hy_f_b2,
              hy_f_wout, hy_freq):
    rope = axial_rope_tables(x.shape[1])
    silu_c = jax.nn.silu(c)
    silu_cc = jax.nn.silu(c_ctx)
    xc = ctx
    for i in range(DEPTH):
        ctx_needed_later = any(j > i and j % 2 == 0 for j in range(DEPTH))
        shift, scale, gate = jnp.split((silu_c @ mod_w[i] + mod_b[i])[:, None, :], 3, axis=-1)
        h = rms_norm(x, norm_g[i]) * (1 + scale) + shift
        if i % 2 == 0 or ctx_needed_later:
            shift_c, scale_c, gate_c = jnp.split(silu_cc @ mod_w[i] + mod_b[i], 3, axis=-1)
            hc = rms_norm(xc, norm_g[i]) * (1 + scale_c) + shift_c
        if i % 2 == 0:
            e = i // 2
            mla_p = (mla_q_a_norm[e], mla_w_uq[e], mla_kv_a_norm[e], mla_w_ukv[e], mla_q_norm[e], mla_k_norm[e])
            rw_p = (rwkv_mu_prev[e], rwkv_mu_next[e], rwkv_w0[e], rwkv_w_up[e], rwkv_a0[e], rwkv_a_up[e],
                    rwkv_k_k[e], rwkv_k_a[e])
            rw_out_p = (rwkv_r_k[e], rwkv_ln_w[e], rwkv_ln_b[e])
            y, yc = even_mixer(h, hc, rope, ctx_needed_later, ev_w_in[e], ev_w_out[e], mla_p, rw_p, rw_out_p)
        else:
            o = i // 2
            hy_p = (od_w_in[o], od_w_out[o], hy_conv_w[o], hy_conv_b[o], hy_bias_d[o], hy_f_w1[o], hy_f_b1[o],
                    hy_f_w2[o], hy_f_b2[o], hy_f_wout[o], hy_freq[o])
            y = hyena_mixer(h, *hy_p)
            yc = hyena_mixer(hc, *hy_p) if ctx_needed_later else None
        x = x + gate * y
        if ctx_needed_later:
            xc = xc + gate_c * yc
    return x
```

```python
import functools
import math

import numpy as np
import jax
import jax.numpy as jnp
from jax import lax
from jax.experimental import pallas as pl
from jax.experimental.pallas import tpu as pltpu

F32 = jnp.float32
BF16 = jnp.bfloat16
HIGHEST = lax.Precision.HIGHEST

D_MODEL = 1024
DEPTH = 4
GRID_W = 64
NORM_EPS = 1e-6
MLA_HEADS = 8
MLA_NOPE = 64
MLA_ROPE = 32
MLA_QK = MLA_NOPE + MLA_ROPE
MLA_V = 64
MLA_Q_LORA = 256
MLA_KV_LORA = 128
MLA_WIDTH = MLA_HEADS * MLA_V
ROPE_BASE = 10000.0
RW_HEADS = 8
RW_HEAD = 64
RW_WIDTH = RW_HEADS * RW_HEAD
RW_LORA_W = 64
RW_LORA_A = 64
RW_SHIFT = 3 * RW_WIDTH + 2 * RW_LORA_W + 2 * RW_LORA_A
RW_GN_EPS = 64e-5
EV_DQ = MLA_Q_LORA
EV_DKV = MLA_KV_LORA + MLA_ROPE
HY_WIDTH = D_MODEL
HY_ORDER = 64
HY_BANDS = 16
HY_EMB = 1 + 2 * HY_BANDS
HY_INNER = 2
HY_FAST_DECAY = 0.3
HY_SLOW_DECAY = 1.5
HY_TARGET = 1e-2

LANES = 128
HEAD_PAD = 128
RW_PAIR = 2 * RW_HEAD
N_PAIRS = RW_WIDTH // RW_PAIR
CHUNK = 64
ROW_TILE = 256
ATTN_TQ = 256
DFT_TC = 256
VMEM_LIMIT = 56 * 1024 * 1024

_NN = (((1,), (0,)), ((), ()))
_NT = (((1,), (1,)), ((), ()))


def _mm(a, b, dn=_NN, mode="bf16"):
    if mode == "f32":
        return lax.dot_general(a, b, dn, precision=HIGHEST, preferred_element_type=F32)
    dg = functools.partial(lax.dot_general, dimension_numbers=dn, preferred_element_type=F32)
    ah = a.astype(BF16)
    bh = b.astype(BF16)
    if mode == "bf16":
        return dg(ah, bh)
    al = (a - ah.astype(F32)).astype(BF16)
    bl = (b - bh.astype(F32)).astype(BF16)
    return dg(ah, bh) + (dg(ah, bl) + dg(al, bh))


def _cparams(*sem):
    return pltpu.CompilerParams(dimension_semantics=sem, vmem_limit_bytes=VMEM_LIMIT)


def _silu(t):
    return t * jax.nn.sigmoid(t)


def _shifted_rows(p, prev_ref, next_ref):
    tm = p.shape[0]
    i = pl.program_id(1)
    last = pl.num_programs(1) - 1
    prev_row = jnp.where(i > 0, prev_ref[0, 7:8, :], 0.0)
    next_row = jnp.where(i < last, next_ref[0, 0:1, :], 0.0)
    rows = lax.broadcasted_iota(jnp.int32, (tm, 1), 0)
    prev = jnp.where(rows == 0, prev_row, pltpu.roll(p, 1, axis=0))
    nxt = jnp.where(rows == tm - 1, next_row, pltpu.roll(p, tm - 1, axis=0))
    return prev, nxt


def _halo_specs(tm, width, n_rows):
    t8 = tm // 8
    last8 = n_rows // 8 - 1
    main = pl.BlockSpec((1, tm, width), lambda b, i: (b, i, 0))
    prev = pl.BlockSpec((1, 8, width), lambda b, i: (b, jnp.maximum(i * t8 - 1, 0), 0))
    nxt = pl.BlockSpec((1, 8, width), lambda b, i: (b, jnp.minimum((i + 1) * t8, last8), 0))
    return main, prev, nxt


def _mod_kernel(c_ref, w_ref, b_ref, o_ref):
    o_ref[0] = _mm(_silu(c_ref[...]), w_ref[0], mode="x3") + b_ref[0]


def _modulation(cvec, mod_w, mod_b):
    rows, d = cvec.shape
    n = mod_w.shape[-1]
    tn = 1024
    return pl.pallas_call(
        _mod_kernel,
        grid=(DEPTH, n // tn),
        in_specs=[pl.BlockSpec((rows, d), lambda i, j: (0, 0)),
                  pl.BlockSpec((1, d, tn), lambda i, j: (i, 0, j)),
                  pl.BlockSpec((1, 1, tn), lambda i, j: (i, 0, j))],
        out_specs=pl.BlockSpec((1, rows, tn), lambda i, j: (i, 0, j)),
        out_shape=jax.ShapeDtypeStruct((DEPTH, rows, n), F32),
        compiler_params=_cparams("arbitrary", "arbitrary"),
        name="modulation",
    )(cvec, mod_w, mod_b.reshape(DEPTH, 1, n))


def _norm_proj_kernel(nw, x_ref, g_ref, sc_ref, sh_ref, *refs):
    x = x_ref[0]
    h = x * lax.rsqrt(jnp.mean(x * x, axis=-1, keepdims=True) + NORM_EPS) * g_ref[...]
    hb = (h * sc_ref[0] + sh_ref[0]).astype(BF16)
    for w_ref, o_ref in zip(refs[:nw], refs[nw:]):
        o_ref[0] = jnp.dot(hb, w_ref[...], preferred_element_type=F32)


def _norm_proj(x, g, scale1p, shift, weights):
    bsz, n_rows, d = x.shape
    tm = min(ROW_TILE, n_rows)
    vec = pl.BlockSpec((1, 1, d), lambda b, i: (b, 0, 0))
    in_specs = [pl.BlockSpec((1, tm, d), lambda b, i: (b, i, 0)),
                pl.BlockSpec((1, d), lambda b, i: (0, 0)), vec, vec]
    in_specs += [pl.BlockSpec(w.shape, lambda b, i: (0, 0)) for w in weights]
    return pl.pallas_call(
        functools.partial(_norm_proj_kernel, len(weights)),
        grid=(bsz, n_rows // tm),
        in_specs=in_specs,
        out_specs=[pl.BlockSpec((1, tm, w.shape[1]), lambda b, i: (b, i, 0)) for w in weights],
        out_shape=[jax.ShapeDtypeStruct((bsz, n_rows, w.shape[1]), F32) for w in weights],
        compiler_params=_cparams("parallel", "arbitrary"),
        name="norm_proj",
    )(x, g.reshape(1, d), scale1p, shift, *weights)


def _head_norm_rope(t, gain, tabs):
    ms = jnp.sum(t * t, axis=-1, keepdims=True) * (1.0 / MLA_QK)
    t = t * lax.rsqrt(ms + NORM_EPS) * gain
    if tabs is not None:
        cos_f, sin_a, sin_b = tabs
        t = (t * cos_f + pltpu.roll(t, MLA_ROPE // 2, axis=1) * sin_a
             + pltpu.roll(t, HEAD_PAD - MLA_ROPE // 2, axis=1) * sin_b)
    return t


def _q_prep_kernel(rope, p_ref, an_ref, w_ref, gn_ref, *refs):
    tabs = tuple(r[...] for r in refs[:3]) if rope else None
    o_ref = refs[-1]
    p = p_ref[0]
    a = p * lax.rsqrt(jnp.mean(p * p, axis=-1, keepdims=True) + NORM_EPS) * an_ref[...]
    q = jnp.dot(a.astype(BF16), w_ref[...], preferred_element_type=F32)
    gain = gn_ref[...]
    for h in range(MLA_HEADS):
        sl = slice(h * HEAD_PAD, (h + 1) * HEAD_PAD)
        o_ref[0, :, sl] = _head_norm_rope(q[:, sl], gain, tabs).astype(BF16)


def _q_prep(p_dq, a_norm, w_uq, gain, tabs):
    bsz, n_rows, r = p_dq.shape
    tm = min(ROW_TILE, n_rows)
    rope = tabs is not None
    width = MLA_HEADS * HEAD_PAD
    in_specs = [pl.BlockSpec((1, tm, r), lambda b, i: (b, i, 0)),
                pl.BlockSpec((1, r), lambda b, i: (0, 0)),
                pl.BlockSpec(w_uq.shape, lambda b, i: (0, 0)),
                pl.BlockSpec((1, HEAD_PAD), lambda b, i: (0, 0))]
    args = [p_dq, a_norm.reshape(1, r), w_uq, gain]
    if rope:
        in_specs += [pl.BlockSpec((tm, HEAD_PAD), lambda b, i: (i, 0))] * 3
        args += list(tabs)
    return pl.pallas_call(
        functools.partial(_q_prep_kernel, rope),
        grid=(bsz, n_rows // tm),
        in_specs=in_specs,
        out_specs=pl.BlockSpec((1, tm, width), lambda b, i: (b, i, 0)),
        out_shape=jax.ShapeDtypeStruct((bsz, n_rows, width), BF16),
        compiler_params=_cparams("parallel", "arbitrary"),
        name="mla_q_prep",
    )(*args)


def _kv_prep_kernel(rope, p_ref, an_ref, wk_ref, wv_ref, gn_ref, *refs):
    tabs = tuple(r[...] for r in refs[:3]) if rope else None
    k_ref, v_ref = refs[-2], refs[-1]
    lat = p_ref[0, :, :MLA_KV_LORA]
    k_rope = p_ref[0, :, MLA_KV_LORA:]
    a = lat * lax.rsqrt(jnp.mean(lat * lat, axis=-1, keepdims=True) + NORM_EPS) * an_ref[...]
    ab = a.astype(BF16)
    k_nope = jnp.dot(ab, wk_ref[...], preferred_element_type=F32)
    v_ref[0] = jnp.dot(ab, wv_ref[...], preferred_element_type=F32).astype(BF16)
    gain = gn_ref[...]
    for h in range(MLA_HEADS):
        sl = slice(h * HEAD_PAD, (h + 1) * HEAD_PAD)
        k_ref[0, :, sl] = _head_norm_rope(k_nope[:, sl] + k_rope, gain, tabs).astype(BF16)


def _kv_prep(p_dkv, a_norm, w_k, w_v, gain, tabs):
    bsz, n_rows, width_in = p_dkv.shape
    tm = min(ROW_TILE, n_rows)
    rope = tabs is not None
    kw = MLA_HEADS * HEAD_PAD
    in_specs = [pl.BlockSpec((1, tm, width_in), lambda b, i: (b, i, 0)),
                pl.BlockSpec((1, MLA_KV_LORA), lambda b, i: (0, 0)),
                pl.BlockSpec(w_k.shape, lambda b, i: (0, 0)),
                pl.BlockSpec(w_v.shape, lambda b, i: (0, 0)),
                pl.BlockSpec((1, HEAD_PAD), lambda b, i: (0, 0))]
    args = [p_dkv, a_norm.reshape(1, MLA_KV_LORA), w_k, w_v, gain]
    if rope:
        in_specs += [pl.BlockSpec((tm, HEAD_PAD), lambda b, i: (i, 0))] * 3
        args += list(tabs)
    return pl.pallas_call(
        functools.partial(_kv_prep_kernel, rope),
        grid=(bsz, n_rows // tm),
        in_specs=in_specs,
        out_specs=[pl.BlockSpec((1, tm, kw), lambda b, i: (b, i, 0)),
                   pl.BlockSpec((1, tm, MLA_WIDTH), lambda b, i: (b, i, 0))],
        out_shape=[jax.ShapeDtypeStruct((bsz, n_rows, kw), BF16),
                   jax.ShapeDtypeStruct((bsz, n_rows, MLA_WIDTH), BF16)],
        compiler_params=_cparams("parallel", "arbitrary"),
        name="mla_kv_prep",
    )(*args)


def _attn_kernel(nseg, q_ref, *refs):
    k_refs, v_refs, o_ref = refs[:nseg], refs[nseg:2 * nseg], refs[2 * nseg]
    tq = q_ref.shape[1]
    lane = lax.broadcasted_iota(jnp.int32, (tq, LANES), 1)
    for hp in range(MLA_HEADS // 2):
        vsl = slice(hp * LANES, (hp + 1) * LANES)
        outs = []
        for h in (2 * hp, 2 * hp + 1):
            sl = slice(h * HEAD_PAD, (h + 1) * HEAD_PAD)
            q = q_ref[0, :, sl]
            ss = [lax.dot_general(q, k_ref[0, :, sl], _NT, preferred_element_type=F32)
                  for k_ref in k_refs]
            m = functools.reduce(jnp.maximum, [jnp.max(s, axis=-1, keepdims=True) for s in ss])
            ps = [jnp.exp(s - m) for s in ss]
            denom = functools.reduce(jnp.add, [jnp.sum(p, axis=-1, keepdims=True) for p in ps])
            acc = functools.reduce(jnp.add, [
                jnp.dot(p.astype(BF16), v_ref[0, :, vsl], preferred_element_type=F32)
                for p, v_ref in zip(ps, v_refs)])
            outs.append(acc * (1.0 / denom))
        o_ref[0, :, vsl] = jnp.where(lane < MLA_V, outs[0], outs[1])


def _attention(q, ks, vs):
    bsz, n_q, qw = q.shape
    tq = min(ATTN_TQ, n_q)
    nseg = len(ks)
    in_specs = [pl.BlockSpec((1, tq, qw), lambda b, i: (b, i, 0))]
    in_specs += [pl.BlockSpec((1,) + k.shape[1:], lambda b, i: (b, 0, 0)) for k in ks]
    in_specs += [pl.BlockSpec((1,) + v.shape[1:], lambda b, i: (b, 0, 0)) for v in vs]
    return pl.pallas_call(
        functools.partial(_attn_kernel, nseg),
        grid=(bsz, n_q // tq),
        in_specs=in_specs,
        out_specs=pl.BlockSpec((1, tq, MLA_WIDTH), lambda b, i: (b, i, 0)),
        out_shape=jax.ShapeDtypeStruct((bsz, n_q, MLA_WIDTH), F32),
        compiler_params=_cparams("parallel", "arbitrary"),
        name="mla_attention",
    )(q, *ks, *vs)


def _rw_prep_kernel(p_ref, prev_ref, next_ref, mup_ref, mun_ref, w0_ref, wup_ref, a0_ref, aup_ref,
                    kk_ref, ka_ref, rk_ref, seg_ref,
                    r_out, v_out, kkn_out, bonus_out, lw0_out, lw1_out, b0_out, b1_out,
                    kd0_out, kd1_out):
    p = p_ref[0]
    prev, nxt = _shifted_rows(p, prev_ref, next_ref)
    ps = p + (prev - p) * mup_ref[...] + (nxt - p) * mun_ref[...]
    w = RW_WIDTH
    r, k, v = ps[:, :w], ps[:, w:2 * w], ps[:, 2 * w:3 * w]
    wd = jnp.tanh(ps[:, 3 * w:3 * w + 2 * RW_LORA_W])
    ad = ps[:, 3 * w + 2 * RW_LORA_W:]
    seg = seg_ref[...]
    kq = k * kk_ref[...]
    kk = kq * lax.rsqrt(jnp.maximum(_mm(kq * kq, seg, mode="f32"), 1e-24))
    r_out[0] = r
    v_out[0] = v
    kkn_out[0] = kk
    bonus_in = jnp.zeros_like(r)
    for d, (lw_out, b_out, kd_out) in enumerate(((lw0_out, b0_out, kd0_out),
                                                  (lw1_out, b1_out, kd1_out))):
        z = w0_ref[d:d + 1, :] + _mm(wd, wup_ref[d], mode="f32")
        w_log = -(jnp.maximum(-z, 0.0) + jnp.log1p(jnp.exp(-jnp.abs(z)))) - 0.5
        lw_out[0] = -jnp.exp(w_log)
        a = jax.nn.sigmoid(a0_ref[d:d + 1, :] + _mm(ad, aup_ref[d], mode="f32"))
        kd = k * (1.0 + (a - 1.0) * ka_ref[...])
        b_out[0] = kk * a
        kd_out[0] = kd
        bonus_in = bonus_in + r * kd * rk_ref[d:d + 1, :]
    bonus_out[0] = _mm(bonus_in, seg, mode="f32") * v


def _rw_prep(p_rw, prm):
    bsz, n_rows, width = p_rw.shape
    tm = min(ROW_TILE, n_rows)
    main, prev, nxt = _halo_specs(tm, width, n_rows)
    full = lambda a: pl.BlockSpec(a.shape, lambda b, i: (0,) * a.ndim)
    consts = [prm["mu_prev"], prm["mu_next"], prm["w0"], prm["w_up"], prm["a0"], prm["a_up"],
              prm["k_k"], prm["k_a"], prm["r_k"], prm["seg"]]
    out_spec = pl.BlockSpec((1, tm, RW_WIDTH), lambda b, i: (b, i, 0))
    out_shape = jax.ShapeDtypeStruct((bsz, n_rows, RW_WIDTH), F32)
    outs = pl.pallas_call(
        _rw_prep_kernel,
        grid=(bsz, n_rows // tm),
        in_specs=[main, prev, nxt] + [full(a) for a in consts],
        out_specs=[out_spec] * 10,
        out_shape=[out_shape] * 10,
        compiler_params=_cparams("parallel", "arbitrary"),
        name="rwkv_prep",
    )(p_rw, p_rw, p_rw, *consts)
    r, v, kk, bonus, lw0, lw1, b0, b1, kd0, kd1 = outs
    return dict(r=r, v=v, kk=kk, bonus=bonus, lw=(lw0, lw1), b=(b0, b1), kd=(kd0, kd1))


def _chunk_local_kernel(rev, mode, r_ref, v_ref, kk_ref, lw_ref, b_ref, kd_ref,
                        rp_ref, o0_ref, m_ref, n_ref):
    c = r_ref.shape[1]
    c2 = 2 * c
    ti = lax.broadcasted_iota(jnp.int32, (c, c), 0)
    si = lax.broadcasted_iota(jnp.int32, (c, c), 1)
    tri = jnp.where((si >= ti) if rev else (si <= ti), 1.0, 0.0)
    t2 = lax.broadcasted_iota(jnp.int32, (c2, c2), 0)
    s2 = lax.broadcasted_iota(jnp.int32, (c2, c2), 1)
    same = jnp.where((t2 >= c) == (s2 >= c), 1, 0)
    before = jnp.where((s2 > t2) if rev else (s2 < t2), same, 0) == 1
    upto = jnp.where((s2 >= t2) if rev else (s2 <= t2), same, 0) == 1
    eye2 = jnp.where(t2 == s2, 1.0, 0.0)
    head0 = lax.broadcasted_iota(jnp.int32, (c, RW_PAIR), 1) < RW_HEAD

    def stack(t):
        return jnp.concatenate([jnp.where(head0, t, 0.0), jnp.where(head0, 0.0, t)], axis=0)

    def unstack(t):
        return t[:c] + t[c:]

    mm = functools.partial(_mm, mode=mode)
    for p in range(N_PAIRS):
        sl = slice(p * RW_PAIR, (p + 1) * RW_PAIR)
        r, v, kk = r_ref[0, :, sl], v_ref[0, :, sl], kk_ref[0, :, sl]
        lw, b, kd = lw_ref[0, :, sl], b_ref[0, :, sl], kd_ref[0, :, sl]
        lam = _mm(tri, lw, mode="f32")
        lam_tot = lam[0:1] if rev else lam[c - 1:c]
        e_neg = jnp.exp(-lam)
        e_tail = jnp.exp(lam_tot - lam)
        at2 = stack(-kk * jnp.exp(lam - lw))
        rt2 = stack(r * jnp.exp(lam))
        bh2, kh2 = stack(b * e_neg), stack(kd * e_neg)
        bt2, kt2 = stack(b * e_tail), stack(kd * e_tail)
        v2 = stack(v)
        a_ab = jnp.where(before, mm(at2, bh2, _NT), 0.0)
        a_ak = jnp.where(before, mm(at2, kh2, _NT), 0.0)
        a_rb = jnp.where(upto, mm(rt2, bh2, _NT), 0.0)
        a_rk = jnp.where(upto, mm(rt2, kh2, _NT), 0.0)
        t_inv = eye2 + a_ab
        pw = a_ab
        for _ in range(int(math.log2(c)) - 1):
            pw = mm(pw, pw)
            t_inv = t_inv + mm(t_inv, pw)
        ap2 = mm(t_inv, at2)
        u02 = mm(t_inv, mm(a_ak, v2))
        rp_ref[0, :, sl] = unstack(rt2 + mm(a_rb, ap2))
        o0_ref[0, :, sl] = unstack(mm(a_rb, u02) + mm(a_rk, v2))
        decay_tot = jnp.where(eye2 == 1.0, jnp.exp(lam_tot), 0.0)
        m_ref[0, 0, :, sl] = unstack(decay_tot + mm(bt2.T, ap2))
        n_ref[0, 0, :, sl] = unstack(mm(bt2.T, u02) + mm(kt2.T, v2))


def _chunk_local(st, d, mode):
    rev = d == 1
    bsz, n_rows, w = st["r"].shape
    nch = n_rows // CHUNK
    cmap = (lambda b, j: (b, nch - 1 - j, 0)) if rev else (lambda b, j: (b, j, 0))
    cmap4 = (lambda b, j: (b, nch - 1 - j, 0, 0)) if rev else (lambda b, j: (b, j, 0, 0))
    spec = pl.BlockSpec((1, CHUNK, w), cmap)
    mspec = pl.BlockSpec((1, 1, RW_HEAD, w), cmap4)
    row_shape = jax.ShapeDtypeStruct((bsz, n_rows, w), F32)
    mat_shape = jax.ShapeDtypeStruct((bsz, nch, RW_HEAD, w), F32)
    return pl.pallas_call(
        functools.partial(_chunk_local_kernel, rev, mode),
        grid=(bsz, nch),
        in_specs=[spec] * 6,
        out_specs=[spec, spec, mspec, mspec],
        out_shape=[row_shape, row_shape, mat_shape, mat_shape],
        compiler_params=_cparams("parallel", "arbitrary"),
        name="rwkv_chunk_local",
    )(st["r"], st["v"], st["kk"], st["lw"][d], st["b"][d], st["kd"][d])


def _chunk_seq_kernel(mode, rp_ref, o0_ref, m_ref, n_ref, s0_ref, o_ref, sfin_ref, st_ref):
    j = pl.program_id(1)

    @pl.when(j == 0)
    def _():
        st_ref[...] = s0_ref[0]

    rows = lax.broadcasted_iota(jnp.int32, (RW_PAIR, RW_PAIR), 0) < RW_HEAD
    cols = lax.broadcasted_iota(jnp.int32, (RW_PAIR, RW_PAIR), 1) < RW_HEAD
    diag_blocks = rows == cols

    def block_diag(t):
        return jnp.where(diag_blocks, jnp.concatenate([t, t], axis=0), 0.0)

    for p in range(N_PAIRS):
        sl = slice(p * RW_PAIR, (p + 1) * RW_PAIR)
        state = st_ref[:, sl]
        o_ref[0, :, sl] = _mm(rp_ref[0, :, sl], state, mode=mode) + o0_ref[0, :, sl]
        st_ref[:, sl] = (_mm(block_diag(m_ref[0, 0, :, sl]), state, mode=mode)
                         + block_diag(n_ref[0, 0, :, sl]))

    @pl.when(j == pl.num_programs(1) - 1)
    def _():
        sfin_ref[0] = st_ref[...]


def _chunk_seq(rp, o0, m, n, s0, d, mode):
    rev = d == 1
    bsz, n_rows, w = rp.shape
    nch = n_rows // CHUNK
    cmap = (lambda b, j: (b, nch - 1 - j, 0)) if rev else (lambda b, j: (b, j, 0))
    cmap4 = (lambda b, j: (b, nch - 1 - j, 0, 0)) if rev else (lambda b, j: (b, j, 0, 0))
    spec = pl.BlockSpec((1, CHUNK, w), cmap)
    mspec = pl.BlockSpec((1, 1, RW_HEAD, w), cmap4)
    sspec = pl.BlockSpec((1, RW_PAIR, w), lambda b, j: (b, 0, 0))
    return pl.pallas_call(
        functools.partial(_chunk_seq_kernel, mode),
        grid=(bsz, nch),
        in_specs=[spec, spec, mspec, mspec, sspec],
        out_specs=[spec, sspec],
        out_shape=[jax.ShapeDtypeStruct((bsz, n_rows, w), F32),
                   jax.ShapeDtypeStruct((bsz, RW_PAIR, w), F32)],
        scratch_shapes=[pltpu.VMEM((RW_PAIR, w), F32)],
        compiler_params=_cparams("parallel", "arbitrary"),
        name="rwkv_chunk_seq",
    )(rp, o0, m, n, s0)


def _even_out_kernel(x_ref, gate_ref, om_ref, of_ref, ob_ref, bonus_ref, g_ref, lnw_ref, lnb_ref,
                     seg_ref, w_ref, o_ref):
    seg = seg_ref[...]
    o = of_ref[0] + ob_ref[0]
    mu = _mm(o, seg, mode="f32") * (1.0 / RW_HEAD)
    dlt = o - mu
    var = _mm(dlt * dlt, seg, mode="f32") * (1.0 / RW_HEAD)
    o_rw = dlt * lax.rsqrt(var + RW_GN_EPS) * lnw_ref[...] + lnb_ref[...] + bonus_ref[0]
    g = g_ref[0]
    z_m = (om_ref[0] * _silu(g[:, :MLA_WIDTH])).astype(BF16)
    z_r = (o_rw * _silu(g[:, MLA_WIDTH:])).astype(BF16)
    y = (jnp.dot(z_m, w_ref[:MLA_WIDTH, :], preferred_element_type=F32)
         + jnp.dot(z_r, w_ref[MLA_WIDTH:, :], preferred_element_type=F32))
    o_ref[0] = x_ref[0] + gate_ref[0] * y


def _even_out(x, gate, o_mla, o_f, o_b, bonus, g, ln_w, ln_b, seg, w_out):
    bsz, n_rows, d = x.shape
    tm = min(ROW_TILE, n_rows)
    rows = lambda width: pl.BlockSpec((1, tm, width), lambda b, i: (b, i, 0))
    full = lambda a: pl.BlockSpec(a.shape, lambda b, i: (0,) * a.ndim)
    return pl.pallas_call(
        _even_out_kernel,
        grid=(bsz, n_rows // tm),
        in_specs=[rows(d), pl.BlockSpec((1, 1, d), lambda b, i: (b, 0, 0)),
                  rows(MLA_WIDTH), rows(RW_WIDTH), rows(RW_WIDTH), rows(RW_WIDTH), rows(d),
                  full(ln_w), full(ln_b), full(seg), full(w_out)],
        out_specs=rows(d),
        out_shape=jax.ShapeDtypeStruct(x.shape, F32),
        compiler_params=_cparams("parallel", "arbitrary"),
        name="even_out",
    )(x, gate, o_mla, o_f, o_b, bonus, g, ln_w, ln_b, seg, w_out)


def _conv3_kernel(u_ref, prev_ref, next_ref, w_ref, b_ref, v_ref, x0_ref):
    u = u_ref[0]
    prev, nxt = _shifted_rows(u, prev_ref, next_ref)
    cv = prev * w_ref[0:1, :] + u * w_ref[1:2, :] + nxt * w_ref[2:3, :] + b_ref[...]
    hw = HY_WIDTH
    x0_ref[0] = cv[:, :hw]
    v_ref[0] = cv[:, 2 * hw:] * cv[:, hw:2 * hw]


def _conv3(u, conv_w, conv_b):
    bsz, n_rows, width = u.shape
    tm = min(ROW_TILE, n_rows)
    main, prev, nxt = _halo_specs(tm, width, n_rows)
    out_spec = pl.BlockSpec((1, tm, HY_WIDTH), lambda b, i: (b, i, 0))
    out_shape = jax.ShapeDtypeStruct((bsz, n_rows, HY_WIDTH), F32)
    return pl.pallas_call(
        _conv3_kernel,
        grid=(bsz, n_rows // tm),
        in_specs=[main, prev, nxt,
                  pl.BlockSpec(conv_w.shape, lambda b, i: (0, 0)),
                  pl.BlockSpec((1, width), lambda b, i: (0, 0))],
        out_specs=[out_spec, out_spec],
        out_shape=[out_shape, out_shape],
        compiler_params=_cparams("parallel", "arbitrary"),
        name="hyena_conv3",
    )(u, u, u, conv_w, conv_b.reshape(1, width))


def _hy_hidden_kernel(w1_ref, b1_ref, w2_ref, b2_ref, freq_ref, h_ref):
    n_pos = h_ref.shape[0]
    pos = lax.broadcasted_iota(jnp.int32, (n_pos, 1), 0).astype(F32)
    lane = lax.broadcasted_iota(jnp.int32, (1, LANES), 1)
    band_idx = jnp.where(lane <= HY_BANDS, lane - 1, lane - 1 - HY_BANDS).astype(F32)
    band = 1e-4 + band_idx * ((HY_BANDS - 1 - 1e-4) / (HY_BANDS - 1))
    ang = pos * (2.0 * math.pi / n_pos) * band
    z = jnp.where(lane == 0, pos / (n_pos - 1),
                  jnp.where(lane <= HY_BANDS, jnp.cos(ang),
                            jnp.where(lane <= 2 * HY_BANDS, -jnp.sin(ang), 0.0)))
    freq = freq_ref[...]
    hdn = jnp.sin(freq * (_mm(z, w1_ref[...], mode="f32") + b1_ref[...]))
    for j in range(HY_INNER):
        hdn = jnp.sin(freq * (_mm(hdn, w2_ref[j], mode="f32") + b2_ref[j]))
    h_ref[...] = hdn


def _hy_filter_kernel(h_ref, w0_ref, w1_ref, dl_ref, fs_ref, fd_ref):
    n_pos = h_ref.shape[0]
    pos = lax.broadcasted_iota(jnp.int32, (n_pos, 1), 0)
    t = pos.astype(F32) / (n_pos - 1)
    dec = jnp.exp(-t * dl_ref[...])
    hdn = h_ref[...]
    f_fwd = _mm(hdn, w0_ref[...], mode="f32") * dec
    f_bwd = jnp.where(pos == 0, 0.0, _mm(hdn, w1_ref[...], mode="f32") * dec)
    inv = 1.0 / (jnp.sum(jnp.abs(f_fwd), axis=0, keepdims=True)
                 + jnp.sum(jnp.abs(f_bwd), axis=0, keepdims=True))
    fs_ref[...] = (f_fwd + f_bwd) * inv
    fd_ref[...] = (f_bwd - f_fwd) * inv


def _hyena_filters(n_pos, f_w1, f_b1, f_w2, f_b2, f_wout, freq, deltas):
    w1 = jnp.zeros((LANES, HY_ORDER), F32).at[:HY_EMB].set(f_w1)
    hdn = pl.pallas_call(
        _hy_hidden_kernel,
        out_shape=jax.ShapeDtypeStruct((n_pos, HY_ORDER), F32),
        name="hyena_filter_hidden",
    )(w1, f_b1.reshape(1, HY_ORDER), f_w2, f_b2.reshape(HY_INNER, 1, HY_ORDER),
      freq.reshape(1, HY_ORDER))
    tn = 256
    cspec = pl.BlockSpec((HY_ORDER, tn), lambda j: (0, j))
    ospec = pl.BlockSpec((n_pos, tn), lambda j: (0, j))
    oshape = jax.ShapeDtypeStruct((n_pos, HY_WIDTH), F32)
    return pl.pallas_call(
        _hy_filter_kernel,
        grid=(HY_WIDTH // tn,),
        in_specs=[pl.BlockSpec((n_pos, HY_ORDER), lambda j: (0, 0)), cspec, cspec,
                  pl.BlockSpec((1, tn), lambda j: (0, j))],
        out_specs=[ospec, ospec],
        out_shape=[oshape, oshape],
        compiler_params=_cparams("arbitrary"),
        name="hyena_filter",
    )(hdn, f_wout[:, :HY_WIDTH], f_wout[:, HY_WIDTH:], deltas)


def _dft_tables_kernel(c_ref, s_ref):
    tr, n_half = c_ref.shape
    n = 2 * n_half
    k = lax.broadcasted_iota(jnp.int32, (tr, n_half), 0) + pl.program_id(0) * tr
    s = lax.broadcasted_iota(jnp.int32, (tr, n_half), 1)
    ph = (k * s) & (n - 1)
    ph = jnp.where(ph >= n_half, ph - n, ph)
    ang = ph.astype(F32) * (2.0 * math.pi / n)
    c_ref[...] = jnp.cos(ang).astype(BF16)
    s_ref[...] = jnp.sin(ang).astype(BF16)


def _dft_tables(n_half):
    tr = min(256, n_half)
    spec = pl.BlockSpec((tr, n_half), lambda i: (i, 0))
    shape = jax.ShapeDtypeStruct((n_half, n_half), BF16)
    return pl.pallas_call(
        _dft_tables_kernel,
        grid=(n_half // tr,),
        out_specs=[spec, spec],
        out_shape=[shape, shape],
        compiler_params=_cparams("arbitrary"),
        name="dft_tables",
    )()


def _alt_sign(n_pos):
    pos = lax.broadcasted_iota(jnp.int32, (n_pos, 1), 0)
    return jnp.where((pos & 1) == 0, 1.0, -1.0), pos


def _spectrum_kernel(fs_ref, fd_ref, c_ref, s_ref, hre_ref, him_ref, hny_ref):
    n_pos = fs_ref.shape[0]
    alt, pos = _alt_sign(n_pos)
    fs, fd = fs_ref[...], fd_ref[...]

    def mm2(tab, f):
        fh = f.astype(BF16)
        fl = (f - fh.astype(F32)).astype(BF16)
        return (jnp.dot(tab, fh, preferred_element_type=F32)
                + jnp.dot(tab, fl, preferred_element_type=F32))

    scale = 1.0 / n_pos
    hre_ref[...] = mm2(c_ref[...], fs) * jnp.where(pos == 0, 0.5 * scale, scale)
    him_ref[...] = mm2(s_ref[...], fd) * scale
    hny_ref[...] = jnp.sum(fs * alt, axis=0, keepdims=True) * (0.5 * scale)


def _table_spec(tab, nd):
    return pl.BlockSpec(tab.shape, (lambda j: (0, 0)) if nd == 1 else (lambda b, j: (0, 0)),
                        pipeline_mode=pl.Buffered(1))


def _spectrum(fs, fd, ctab, stab):
    n_pos, width = fs.shape
    tc = DFT_TC
    col = pl.BlockSpec((n_pos, tc), lambda j: (0, j))
    return pl.pallas_call(
        _spectrum_kernel,
        grid=(width // tc,),
        in_specs=[col, col, _table_spec(ctab, 1), _table_spec(stab, 1)],
        out_specs=[col, col, pl.BlockSpec((1, tc), lambda j: (0, j))],
        out_shape=[jax.ShapeDtypeStruct((n_pos, width), F32)] * 2
        + [jax.ShapeDtypeStruct((1, width), F32)],
        compiler_params=_cparams("arbitrary"),
        name="hyena_spectrum",
    )(fs, fd, ctab, stab)


def _dft_conv_kernel(v_ref, hre_ref, him_ref, hny_ref, c_ref, s_ref, y_ref):
    n_pos = v_ref.shape[1]
    alt, _ = _alt_sign(n_pos)
    v = v_ref[0]
    vb = v.astype(BF16)
    ctab, stab = c_ref[...], s_ref[...]
    v_re = jnp.dot(ctab, vb, preferred_element_type=F32)
    v_s = jnp.dot(stab, vb, preferred_element_type=F32)
    hre, him = hre_ref[...], him_ref[...]
    y_re = (v_re * hre + v_s * him).astype(BF16)
    y_im = (v_re * him - v_s * hre).astype(BF16)
    nyq = jnp.sum(v * alt, axis=0, keepdims=True) * hny_ref[...]
    y_ref[0] = (jnp.dot(ctab, y_re, preferred_element_type=F32)
                - jnp.dot(stab, y_im, preferred_element_type=F32) + alt * nyq)


def _dft_conv(v, hre, him, hny, ctab, stab):
    bsz, n_pos, width = v.shape
    tc = DFT_TC
    col = pl.BlockSpec((n_pos, tc), lambda b, j: (0, j))
    vspec = pl.BlockSpec((1, n_pos, tc), lambda b, j: (b, 0, j))
    return pl.pallas_call(
        _dft_conv_kernel,
        grid=(bsz, width // tc),
        in_specs=[vspec, col, col, pl.BlockSpec((1, tc), lambda b, j: (0, j)),
                  _table_spec(ctab, 2), _table_spec(stab, 2)],
        out_specs=vspec,
        out_shape=jax.ShapeDtypeStruct(v.shape, F32),
        compiler_params=_cparams("parallel", "arbitrary"),
        name="hyena_dft_conv",
    )(v, hre, him, hny, ctab, stab)


def _hy_out_kernel(x_ref, gate_ref, y_ref, v_ref, x0_ref, g_ref, bias_ref, w_ref, o_ref):
    z = (y_ref[0] + v_ref[0] * bias_ref[...]) * x0_ref[0] * _silu(g_ref[0])
    o_ref[0] = x_ref[0] + gate_ref[0] * jnp.dot(z.astype(BF16), w_ref[...],
                                                preferred_element_type=F32)


def _hy_out(x, gate, y, v, x0, g, bias_d, w_out):
    bsz, n_rows, d = x.shape
    tm = min(ROW_TILE, n_rows)
    rows = pl.BlockSpec((1, tm, d), lambda b, i: (b, i, 0))
    return pl.pallas_call(
        _hy_out_kernel,
        grid=(bsz, n_rows // tm),
        in_specs=[rows, pl.BlockSpec((1, 1, d), lambda b, i: (b, 0, 0)), rows, rows, rows, rows,
                  pl.BlockSpec((1, d), lambda b, i: (0, 0)),
                  pl.BlockSpec(w_out.shape, lambda b, i: (0, 0))],
        out_specs=rows,
        out_shape=jax.ShapeDtypeStruct(x.shape, F32),
        compiler_params=_cparams("parallel", "arbitrary"),
        name="hyena_out",
    )(x, gate, y, v, x0, g, bias_d.reshape(1, d), w_out)


def _rope_perm():
    pairs = np.arange(MLA_ROPE // 2)
    return np.concatenate([np.arange(MLA_NOPE), MLA_NOPE + 2 * pairs, MLA_NOPE + 2 * pairs + 1])


def _rope_tables(n_pos):
    rows = n_pos // GRID_W
    row = jnp.repeat(jnp.arange(rows, dtype=F32), GRID_W)
    col = jnp.tile(jnp.arange(GRID_W, dtype=F32), rows)
    n_freq = MLA_ROPE // 4
    inv = ROPE_BASE ** (-jnp.arange(n_freq, dtype=F32) / n_freq)
    ang = jnp.concatenate([row[:, None] * inv, col[:, None] * inv], axis=-1)
    cos, sin = jnp.cos(ang), jnp.sin(ang)
    half = MLA_ROPE // 2
    ones = jnp.ones((n_pos, MLA_NOPE), F32)
    zeros = jnp.zeros((n_pos, MLA_NOPE), F32)
    pad1 = jnp.ones((n_pos, HEAD_PAD - MLA_QK), F32)
    pad0 = jnp.zeros((n_pos, HEAD_PAD - MLA_QK), F32)
    z16 = jnp.zeros((n_pos, half), F32)
    cos_f = jnp.concatenate([ones, cos, cos, pad1], axis=-1)
    sin_a = jnp.concatenate([zeros, z16, sin, pad0], axis=-1)
    sin_b = jnp.concatenate([zeros, -sin, z16, pad0], axis=-1)
    return cos_f, sin_a, sin_b


def _pad_heads(w, width):
    k, h, _ = w.shape
    return jnp.zeros((k, h, HEAD_PAD), w.dtype).at[:, :, :width].set(w).reshape(k, h * HEAD_PAD)


def _even_weights(e, ev_w_in, ev_w_out, mla_q_a_norm, mla_w_uq, mla_kv_a_norm, mla_w_ukv,
                  mla_q_norm, mla_k_norm, rwkv_mu_prev, rwkv_mu_next, rwkv_w0, rwkv_w_up, rwkv_a0,
                  rwkv_a_up, rwkv_k_k, rwkv_k_a, rwkv_r_k, rwkv_ln_w, rwkv_ln_b):
    perm = _rope_perm()
    w_in = ev_w_in[e]
    d = w_in.shape[0]
    o_dkv = EV_DQ
    o_rw = EV_DQ + EV_DKV
    o_g = o_rw + RW_SHIFT
    w_dkv = jnp.zeros((d, 2 * LANES), F32)
    w_dkv = w_dkv.at[:, :MLA_KV_LORA].set(w_in[:, o_dkv:o_dkv + MLA_KV_LORA])
    rope_cols = o_dkv + MLA_KV_LORA + (perm[MLA_NOPE:] - MLA_NOPE)
    w_dkv = w_dkv.at[:, LANES + MLA_NOPE:LANES + MLA_QK].set(w_in[:, rope_cols])
    proj = [w_in[:, :EV_DQ].astype(BF16), w_dkv.astype(BF16),
            w_in[:, o_rw:o_g].astype(BF16), w_in[:, o_g:].astype(BF16)]
    w_ukv = mla_w_ukv[e]
    pad_gain = lambda g: jnp.zeros((1, HEAD_PAD), F32).at[0, :MLA_QK].set(g[perm])
    seg_id = np.arange(RW_WIDTH) // RW_HEAD
    zero_up = lambda up, dd: jnp.zeros((2 * RW_LORA_W, RW_WIDTH), F32).at[
        dd * RW_LORA_W:(dd + 1) * RW_LORA_W].set(up[dd])
    rw = dict(
        mu_prev=rwkv_mu_prev[e].reshape(1, RW_SHIFT), mu_next=rwkv_mu_next[e].reshape(1, RW_SHIFT),
        w0=rwkv_w0[e], a0=rwkv_a0[e],
        w_up=jnp.stack([zero_up(rwkv_w_up[e], 0), zero_up(rwkv_w_up[e], 1)]),
        a_up=jnp.stack([zero_up(rwkv_a_up[e], 0), zero_up(rwkv_a_up[e], 1)]),
        k_k=rwkv_k_k[e].reshape(1, RW_WIDTH), k_a=rwkv_k_a[e].reshape(1, RW_WIDTH),
        r_k=rwkv_r_k[e].reshape(2, RW_WIDTH),
        seg=jnp.asarray((seg_id[:, None] == seg_id[None, :]).astype(np.float32)),
    )
    return dict(
        proj=proj,
        q_a_norm=mla_q_a_norm[e], kv_a_norm=mla_kv_a_norm[e],
        w_uq=_pad_heads(mla_w_uq[e][:, :, perm], MLA_QK).astype(BF16),
        w_k=_pad_heads(w_ukv[:, :, :MLA_NOPE], MLA_NOPE).astype(BF16),
        w_v=w_ukv[:, :, MLA_NOPE:].reshape(MLA_KV_LORA, MLA_WIDTH).astype(BF16),
        q_gain=pad_gain(mla_q_norm[e]) * (MLA_QK ** -0.5),
        k_gain=pad_gain(mla_k_norm[e]),
        rw=rw,
        ln_w=rwkv_ln_w[e].reshape(1, RW_WIDTH), ln_b=rwkv_ln_b[e].reshape(1, RW_WIDTH),
        w_out=ev_w_out[e].astype(BF16),
    )


CHUNK_MODE = "x3"
STATE_MODE = "f32"


def _rwkv_branch(st_c, st, want_ctx):
    bsz = st["r"].shape[0]
    zero_state = jnp.zeros((bsz, RW_PAIR, RW_WIDTH), F32)
    outs, outs_c = [], []
    for d in range(2):
        rp, o0, m, n = _chunk_local(st_c, d, CHUNK_MODE)
        o_c, s_c = _chunk_seq(rp, o0, m, n, zero_state, d, STATE_MODE)
        rp, o0, m, n = _chunk_local(st, d, CHUNK_MODE)
        o, _ = _chunk_seq(rp, o0, m, n, s_c, d, STATE_MODE)
        outs.append(o)
        outs_c.append(o_c)
    return outs, (outs_c if want_ctx else None)


def _even_layer(x, xc, mod, mod_c, g_norm, wts, rope_tabs, ctx_out):
    shift, scale1p, gate = mod
    shift_c, scale1p_c, gate_c = mod_c
    p_dq, p_dkv, p_rw, g = _norm_proj(x, g_norm, scale1p, shift, wts["proj"])
    pc_dq, pc_dkv, pc_rw, gc = _norm_proj(xc, g_norm, scale1p_c, shift_c, wts["proj"])
    q = _q_prep(p_dq, wts["q_a_norm"], wts["w_uq"], wts["q_gain"], rope_tabs)
    k, v = _kv_prep(p_dkv, wts["kv_a_norm"], wts["w_k"], wts["w_v"], wts["k_gain"], rope_tabs)
    kc, vc = _kv_prep(pc_dkv, wts["kv_a_norm"], wts["w_k"], wts["w_v"], wts["k_gain"], None)
    o_mla = _attention(q, [k, kc], [v, vc])
    st_c = _rw_prep(pc_rw, wts["rw"])
    st = _rw_prep(p_rw, wts["rw"])
    outs, outs_c = _rwkv_branch(st_c, st, ctx_out)
    seg = wts["rw"]["seg"]
    x_new = _even_out(x, gate, o_mla, outs[0], outs[1], st["bonus"], g, wts["ln_w"], wts["ln_b"],
                      seg, wts["w_out"])
    if not ctx_out:
        return x_new, None
    qc = _q_prep(pc_dq, wts["q_a_norm"], wts["w_uq"], wts["q_gain"], None)
    oc_mla = _attention(qc, [kc], [vc])
    xc_new = _even_out(xc, gate_c, oc_mla, outs_c[0], outs_c[1], st_c["bonus"], gc, wts["ln_w"],
                       wts["ln_b"], seg, wts["w_out"])
    return x_new, xc_new


def _hyena_layer(x, mod, g_norm, wts, tables):
    shift, scale1p, gate = mod
    n_pos = x.shape[1]
    ctab, stab = tables
    u, g = _norm_proj(x, g_norm, scale1p, shift, wts["proj"])
    v, x0 = _conv3(u, wts["conv_w"], wts["conv_b"])
    fs, fd = _hyena_filters(n_pos, *wts["filt"])
    hre, him, hny = _spectrum(fs, fd, ctab, stab)
    y = _dft_conv(v, hre, him, hny, ctab, stab)
    return _hy_out(x, gate, y, v, x0, g, wts["bias_d"], wts["w_out"])


def kernel(x, c, ctx, c_ctx, mod_w, mod_b, norm_g, ev_w_in, ev_w_out, mla_q_a_norm, mla_w_uq, mla_kv_a_norm, mla_w_ukv, mla_q_norm, mla_k_norm, rwkv_mu_prev, rwkv_mu_next, rwkv_w0, rwkv_w_up, rwkv_a0, rwkv_a_up, rwkv_k_k, rwkv_k_a, rwkv_r_k, rwkv_ln_w, rwkv_ln_b, od_w_in, od_w_out, hy_conv_w, hy_conv_b, hy_bias_d, hy_f_w1, hy_f_b1, hy_f_w2, hy_f_b2, hy_f_wout, hy_freq):
    bsz, n_lat, d = x.shape
    n_ctx = ctx.shape[1]
    assert n_lat % max(CHUNK, GRID_W) == 0 and n_ctx % CHUNK == 0 and d == D_MODEL

    n_rows = -(-(bsz + 1) // 16) * 16
    cvec = jnp.zeros((n_rows, d), F32).at[:bsz].set(c).at[bsz].set(c_ctx)
    mods = _modulation(cvec, mod_w, mod_b)

    def split_mod(i, lo, hi, reps):
        m = mods[i, lo:hi]
        m = jnp.broadcast_to(m, (reps, 3 * d)) if hi - lo == 1 else m
        m = m[:, None, :]
        return m[..., :d], 1.0 + m[..., d:2 * d], m[..., 2 * d:]

    rope_tabs = _rope_tables(n_lat)
    deltas = jnp.abs(jnp.linspace(math.log(HY_TARGET) / HY_FAST_DECAY,
                                  math.log(HY_TARGET) / HY_SLOW_DECAY, HY_WIDTH,
                                  dtype=F32)).reshape(1, HY_WIDTH)
    tables = {n_lat: _dft_tables(n_lat)}

    xc = ctx
    for i in range(DEPTH):
        ctx_needed_later = any(j > i and j % 2 == 0 for j in range(DEPTH))
        mod = split_mod(i, 0, bsz, bsz)
        mod_c = split_mod(i, bsz, bsz + 1, bsz)
        if i % 2 == 0:
            wts = _even_weights(i // 2, ev_w_in, ev_w_out, mla_q_a_norm, mla_w_uq, mla_kv_a_norm,
                                mla_w_ukv, mla_q_norm, mla_k_norm, rwkv_mu_prev, rwkv_mu_next,
                                rwkv_w0, rwkv_w_up, rwkv_a0, rwkv_a_up, rwkv_k_k, rwkv_k_a,
                                rwkv_r_k, rwkv_ln_w, rwkv_ln_b)
            x, xc_new = _even_layer(x, xc, mod, mod_c, norm_g[i], wts, rope_tabs, ctx_needed_later)
            xc = xc_new if ctx_needed_later else xc
        else:
            o = i // 2
            w_in = od_w_in[o]
            wts = dict(
                proj=[w_in[:, :3 * HY_WIDTH].astype(BF16), w_in[:, 3 * HY_WIDTH:].astype(BF16)],
                conv_w=hy_conv_w[o], conv_b=hy_conv_b[o], bias_d=hy_bias_d[o],
                filt=(hy_f_w1[o], hy_f_b1[o], hy_f_w2[o], hy_f_b2[o], hy_f_wout[o], hy_freq[o],
                      deltas),
                w_out=od_w_out[o].astype(BF16),
            )
            if ctx_needed_later:
                if n_ctx not in tables:
                    tables[n_ctx] = _dft_tables(n_ctx)
                xc = _hyena_layer(xc, mod_c, norm_g[i], wts, tables[n_ctx])
            x = _hyena_layer(x, mod, norm_g[i], wts, tables[n_lat])
    return x
```

```python
import functools
import math

import numpy as np
import jax
import jax.numpy as jnp
from jax import lax
from jax.experimental import pallas as pl
from jax.experimental.pallas import tpu as pltpu

F32 = jnp.float32
BF16 = jnp.bfloat16
HIGHEST = lax.Precision.HIGHEST

D_MODEL = 1024
DEPTH = 4
GRID_W = 64
NORM_EPS = 1e-6
MLA_HEADS = 8
MLA_NOPE = 64
MLA_ROPE = 32
MLA_QK = MLA_NOPE + MLA_ROPE
MLA_V = 64
MLA_Q_LORA = 256
MLA_KV_LORA = 128
MLA_WIDTH = MLA_HEADS * MLA_V
ROPE_BASE = 10000.0
RW_HEADS = 8
RW_HEAD = 64
RW_WIDTH = RW_HEADS * RW_HEAD
RW_LORA_W = 64
RW_LORA_A = 64
RW_SHIFT = 3 * RW_WIDTH + 2 * RW_LORA_W + 2 * RW_LORA_A
RW_GN_EPS = 64e-5
EV_DQ = MLA_Q_LORA
EV_DKV = MLA_KV_LORA + MLA_ROPE
HY_WIDTH = D_MODEL
HY_ORDER = 64
HY_BANDS = 16
HY_EMB = 1 + 2 * HY_BANDS
HY_INNER = 2
HY_FAST_DECAY = 0.3
HY_SLOW_DECAY = 1.5
HY_TARGET = 1e-2

LANES = 128
HEAD_PAD = 128
RW_PAIR = 2 * RW_HEAD
N_PAIRS = RW_WIDTH // RW_PAIR
CHUNK = 64
ROW_TILE = 256
ATTN_TQ = 256
DFT_TC = 256
VMEM_LIMIT = 56 * 1024 * 1024

_NN = (((1,), (0,)), ((), ()))
_NT = (((1,), (1,)), ((), ()))


def _mm(a, b, dn=_NN, mode="bf16"):
    if mode == "f32":
        return lax.dot_general(a, b, dn, precision=HIGHEST, preferred_element_type=F32)
    dg = functools.partial(lax.dot_general, dimension_numbers=dn, preferred_element_type=F32)
    ah = a.astype(BF16)
    bh = b.astype(BF16)
    if mode == "bf16":
        return dg(ah, bh)
    al = (a - ah.astype(F32)).astype(BF16)
    if mode == "lhs2":
        return dg(ah, bh) + dg(al, bh)
    bl = (b - bh.astype(F32)).astype(BF16)
    return dg(ah, bh) + (dg(ah, bl) + dg(al, bh))


def _cparams(*sem):
    return pltpu.CompilerParams(dimension_semantics=sem, vmem_limit_bytes=VMEM_LIMIT)


def _silu(t):
    return t * jax.nn.sigmoid(t)


def _shifted_rows(p, prev_ref, next_ref):
    tm = p.shape[0]
    i = pl.program_id(1)
    last = pl.num_programs(1) - 1
    prev_row = jnp.where(i > 0, prev_ref[0, 7:8, :], 0.0)
    next_row = jnp.where(i < last, next_ref[0, 0:1, :], 0.0)
    rows = lax.broadcasted_iota(jnp.int32, (tm, 1), 0)
    prev = jnp.where(rows == 0, prev_row, pltpu.roll(p, 1, axis=0))
    nxt = jnp.where(rows == tm - 1, next_row, pltpu.roll(p, tm - 1, axis=0))
    return prev, nxt


def _halo_specs(tm, width, n_rows):
    t8 = tm // 8
    last8 = n_rows // 8 - 1
    main = pl.BlockSpec((1, tm, width), lambda b, i: (b, i, 0))
    prev = pl.BlockSpec((1, 8, width), lambda b, i: (b, jnp.maximum(i * t8 - 1, 0), 0))
    nxt = pl.BlockSpec((1, 8, width), lambda b, i: (b, jnp.minimum((i + 1) * t8, last8), 0))
    return main, prev, nxt


def _mod_kernel(c_ref, w_ref, b_ref, o_ref):
    o_ref[0] = _mm(_silu(c_ref[...]), w_ref[0], mode="x3") + b_ref[0]


def _modulation(cvec, mod_w, mod_b):
    rows, d = cvec.shape
    n = mod_w.shape[-1]
    tn = 1024
    return pl.pallas_call(
        _mod_kernel,
        grid=(DEPTH, n // tn),
        in_specs=[pl.BlockSpec((rows, d), lambda i, j: (0, 0)),
                  pl.BlockSpec((1, d, tn), lambda i, j: (i, 0, j)),
                  pl.BlockSpec((1, 1, tn), lambda i, j: (i, 0, j))],
        out_specs=pl.BlockSpec((1, rows, tn), lambda i, j: (i, 0, j)),
        out_shape=jax.ShapeDtypeStruct((DEPTH, rows, n), F32),
        compiler_params=_cparams("arbitrary", "arbitrary"),
        name="modulation",
    )(cvec, mod_w, mod_b.reshape(DEPTH, 1, n))


def _norm_proj_kernel(nw, x_ref, g_ref, sc_ref, sh_ref, *refs):
    x = x_ref[0]
    h = x * lax.rsqrt(jnp.mean(x * x, axis=-1, keepdims=True) + NORM_EPS) * g_ref[...]
    hb = (h * sc_ref[0] + sh_ref[0]).astype(BF16)
    for w_ref, o_ref in zip(refs[:nw], refs[nw:]):
        o_ref[0] = jnp.dot(hb, w_ref[...], preferred_element_type=F32)


def _norm_proj(x, g, scale1p, shift, weights):
    bsz, n_rows, d = x.shape
    tm = min(ROW_TILE, n_rows)
    vec = pl.BlockSpec((1, 1, d), lambda b, i: (b, 0, 0))
    in_specs = [pl.BlockSpec((1, tm, d), lambda b, i: (b, i, 0)),
                pl.BlockSpec((1, d), lambda b, i: (0, 0)), vec, vec]
    in_specs += [pl.BlockSpec(w.shape, lambda b, i: (0, 0)) for w in weights]
    return pl.pallas_call(
        functools.partial(_norm_proj_kernel, len(weights)),
        grid=(bsz, n_rows // tm),
        in_specs=in_specs,
        out_specs=[pl.BlockSpec((1, tm, w.shape[1]), lambda b, i: (b, i, 0)) for w in weights],
        out_shape=[jax.ShapeDtypeStruct((bsz, n_rows, w.shape[1]), F32) for w in weights],
        compiler_params=_cparams("parallel", "arbitrary"),
        name="norm_proj",
    )(x, g.reshape(1, d), scale1p, shift, *weights)


def _head_norm_rope(t, gain, tabs):
    ms = jnp.sum(t * t, axis=-1, keepdims=True) * (1.0 / MLA_QK)
    t = t * lax.rsqrt(ms + NORM_EPS) * gain
    if tabs is not None:
        cos_f, sin_a, sin_b = tabs
        t = (t * cos_f + pltpu.roll(t, MLA_ROPE // 2, axis=1) * sin_a
             + pltpu.roll(t, HEAD_PAD - MLA_ROPE // 2, axis=1) * sin_b)
    return t


def _q_prep_kernel(rope, p_ref, an_ref, w_ref, gn_ref, *refs):
    tabs = tuple(r[...] for r in refs[:3]) if rope else None
    o_ref = refs[-1]
    p = p_ref[0]
    a = p * lax.rsqrt(jnp.mean(p * p, axis=-1, keepdims=True) + NORM_EPS) * an_ref[...]
    q = jnp.dot(a.astype(BF16), w_ref[...], preferred_element_type=F32)
    gain = gn_ref[...]
    for h in range(MLA_HEADS):
        sl = slice(h * HEAD_PAD, (h + 1) * HEAD_PAD)
        o_ref[0, :, sl] = _head_norm_rope(q[:, sl], gain, tabs).astype(BF16)


def _q_prep(p_dq, a_norm, w_uq, gain, tabs):
    bsz, n_rows, r = p_dq.shape
    tm = min(ROW_TILE, n_rows)
    rope = tabs is not None
    width = MLA_HEADS * HEAD_PAD
    in_specs = [pl.BlockSpec((1, tm, r), lambda b, i: (b, i, 0)),
                pl.BlockSpec((1, r), lambda b, i: (0, 0)),
                pl.BlockSpec(w_uq.shape, lambda b, i: (0, 0)),
                pl.BlockSpec((1, HEAD_PAD), lambda b, i: (0, 0))]
    args = [p_dq, a_norm.reshape(1, r), w_uq, gain]
    if rope:
        in_specs += [pl.BlockSpec((tm, HEAD_PAD), lambda b, i: (i, 0))] * 3
        args += list(tabs)
    return pl.pallas_call(
        functools.partial(_q_prep_kernel, rope),
        grid=(bsz, n_rows // tm),
        in_specs=in_specs,
        out_specs=pl.BlockSpec((1, tm, width), lambda b, i: (b, i, 0)),
        out_shape=jax.ShapeDtypeStruct((bsz, n_rows, width), BF16),
        compiler_params=_cparams("parallel", "arbitrary"),
        name="mla_q_prep",
    )(*args)


def _kv_prep_kernel(rope, p_ref, an_ref, wk_ref, wv_ref, vone_ref, gn_ref, *refs):
    tabs = tuple(r[...] for r in refs[:3]) if rope else None
    k_ref, v_ref = refs[-2], refs[-1]
    lat = p_ref[0, :, :MLA_KV_LORA]
    k_rope = p_ref[0, :, MLA_KV_LORA:]
    a = lat * lax.rsqrt(jnp.mean(lat * lat, axis=-1, keepdims=True) + NORM_EPS) * an_ref[...]
    ab = a.astype(BF16)
    k_nope = jnp.dot(ab, wk_ref[...], preferred_element_type=F32)
    v_ref[0] = (jnp.dot(ab, wv_ref[...], preferred_element_type=F32) + vone_ref[...]).astype(BF16)
    gain = gn_ref[...]
    for h in range(MLA_HEADS):
        sl = slice(h * HEAD_PAD, (h + 1) * HEAD_PAD)
        k_ref[0, :, sl] = _head_norm_rope(k_nope[:, sl] + k_rope, gain, tabs).astype(BF16)


def _kv_prep(p_dkv, a_norm, w_k, w_v, v_ones, gain, tabs):
    bsz, n_rows, width_in = p_dkv.shape
    tm = min(ROW_TILE, n_rows)
    rope = tabs is not None
    kw = MLA_HEADS * HEAD_PAD
    in_specs = [pl.BlockSpec((1, tm, width_in), lambda b, i: (b, i, 0)),
                pl.BlockSpec((1, MLA_KV_LORA), lambda b, i: (0, 0)),
                pl.BlockSpec(w_k.shape, lambda b, i: (0, 0)),
                pl.BlockSpec(w_v.shape, lambda b, i: (0, 0)),
                pl.BlockSpec((1, kw), lambda b, i: (0, 0)),
                pl.BlockSpec((1, HEAD_PAD), lambda b, i: (0, 0))]
    args = [p_dkv, a_norm.reshape(1, MLA_KV_LORA), w_k, w_v, v_ones, gain]
    if rope:
        in_specs += [pl.BlockSpec((tm, HEAD_PAD), lambda b, i: (i, 0))] * 3
        args += list(tabs)
    return pl.pallas_call(
        functools.partial(_kv_prep_kernel, rope),
        grid=(bsz, n_rows // tm),
        in_specs=in_specs,
        out_specs=[pl.BlockSpec((1, tm, kw), lambda b, i: (b, i, 0))] * 2,
        out_shape=[jax.ShapeDtypeStruct((bsz, n_rows, kw), BF16)] * 2,
        compiler_params=_cparams("parallel", "arbitrary"),
        name="mla_kv_prep",
    )(*args)


def _attn_kernel(nseg, q_ref, *refs):
    k_refs, v_refs, o_ref = refs[:nseg], refs[nseg:2 * nseg], refs[2 * nseg]
    tq = q_ref.shape[1]
    lane = lax.broadcasted_iota(jnp.int32, (tq, LANES), 1)
    for hp in range(MLA_HEADS // 2):
        outs = []
        for h in (2 * hp, 2 * hp + 1):
            sl = slice(h * HEAD_PAD, (h + 1) * HEAD_PAD)
            q = q_ref[0, :, sl]
            ss = [lax.dot_general(q, k_ref[0, :, sl], _NT, preferred_element_type=F32)
                  for k_ref in k_refs]
            m = functools.reduce(jnp.maximum, [jnp.max(s, axis=-1, keepdims=True) for s in ss])
            acc = functools.reduce(jnp.add, [
                jnp.dot(jnp.exp2(s - m).astype(BF16), v_ref[0, :, sl], preferred_element_type=F32)
                for s, v_ref in zip(ss, v_refs)])
            ones_lane = MLA_V if h % 2 == 0 else 0
            outs.append(acc * (1.0 / acc[:, ones_lane:ones_lane + 1]))
        o_ref[0, :, hp * LANES:(hp + 1) * LANES] = jnp.where(lane < MLA_V, outs[0], outs[1])


def _attention(q, ks, vs):
    bsz, n_q, qw = q.shape
    tq = min(ATTN_TQ, n_q)
    nseg = len(ks)
    in_specs = [pl.BlockSpec((1, tq, qw), lambda b, i: (b, i, 0))]
    in_specs += [pl.BlockSpec((1,) + k.shape[1:], lambda b, i: (b, 0, 0)) for k in ks]
    in_specs += [pl.BlockSpec((1,) + v.shape[1:], lambda b, i: (b, 0, 0)) for v in vs]
    return pl.pallas_call(
        functools.partial(_attn_kernel, nseg),
        grid=(bsz, n_q // tq),
        in_specs=in_specs,
        out_specs=pl.BlockSpec((1, tq, MLA_WIDTH), lambda b, i: (b, i, 0)),
        out_shape=jax.ShapeDtypeStruct((bsz, n_q, MLA_WIDTH), F32),
        compiler_params=_cparams("parallel", "arbitrary"),
        name="mla_attention",
    )(q, *ks, *vs)


def _rw_prep_kernel(p_ref, prev_ref, next_ref, mup_ref, mun_ref, w0_ref, wup_ref, a0_ref, aup_ref,
                    kk_ref, ka_ref, rk_ref, seg_ref,
                    r_out, v_out, kkn_out, bonus_out, lw0_out, lw1_out, b0_out, b1_out,
                    kd0_out, kd1_out):
    p = p_ref[0]
    prev, nxt = _shifted_rows(p, prev_ref, next_ref)
    ps = p + (prev - p) * mup_ref[...] + (nxt - p) * mun_ref[...]
    w = RW_WIDTH
    r, k, v = ps[:, :w], ps[:, w:2 * w], ps[:, 2 * w:3 * w]
    wd = jnp.tanh(ps[:, 3 * w:3 * w + 2 * RW_LORA_W])
    ad = ps[:, 3 * w + 2 * RW_LORA_W:]
    seg = seg_ref[...]
    kq = k * kk_ref[...]
    kk = kq * lax.rsqrt(jnp.maximum(_mm(kq * kq, seg, mode="lhs2"), 1e-24))
    r_out[0] = r
    v_out[0] = v
    kkn_out[0] = kk
    bonus_in = jnp.zeros_like(r)
    for d, (lw_out, b_out, kd_out) in enumerate(((lw0_out, b0_out, kd0_out),
                                                  (lw1_out, b1_out, kd1_out))):
        z = w0_ref[d:d + 1, :] + _mm(wd, wup_ref[d], mode="x3")
        w_log = -(jnp.maximum(-z, 0.0) + jnp.log1p(jnp.exp(-jnp.abs(z)))) - 0.5
        lw_out[0] = -jnp.exp(w_log)
        a = jax.nn.sigmoid(a0_ref[d:d + 1, :] + _mm(ad, aup_ref[d], mode="x3"))
        kd = k * (1.0 + (a - 1.0) * ka_ref[...])
        b_out[0] = kk * a
        kd_out[0] = kd
        bonus_in = bonus_in + r * kd * rk_ref[d:d + 1, :]
    bonus_out[0] = _mm(bonus_in, seg, mode="lhs2") * v


def _rw_prep(p_rw, prm):
    bsz, n_rows, width = p_rw.shape
    tm = min(ROW_TILE, n_rows)
    main, prev, nxt = _halo_specs(tm, width, n_rows)
    full = lambda a: pl.BlockSpec(a.shape, lambda b, i: (0,) * a.ndim)
    consts = [prm["mu_prev"], prm["mu_next"], prm["w0"], prm["w_up"], prm["a0"], prm["a_up"],
              prm["k_k"], prm["k_a"], prm["r_k"], prm["seg"]]
    out_spec = pl.BlockSpec((1, tm, RW_WIDTH), lambda b, i: (b, i, 0))
    out_shape = jax.ShapeDtypeStruct((bsz, n_rows, RW_WIDTH), F32)
    outs = pl.pallas_call(
        _rw_prep_kernel,
        grid=(bsz, n_rows // tm),
        in_specs=[main, prev, nxt] + [full(a) for a in consts],
        out_specs=[out_spec] * 10,
        out_shape=[out_shape] * 10,
        compiler_params=_cparams("parallel", "arbitrary"),
        name="rwkv_prep",
    )(p_rw, p_rw, p_rw, *consts)
    r, v, kk, bonus, lw0, lw1, b0, b1, kd0, kd1 = outs
    return dict(r=r, v=v, kk=kk, bonus=bonus, lw=(lw0, lw1), b=(b0, b1), kd=(kd0, kd1))


def _rwkv_scan_kernel(rf_ref, vf_ref, kkf_ref, lw0_ref, b0_ref, kd0_ref,
                      rb_ref, vb_ref, kkb_ref, lw1_ref, b1_ref, kd1_ref, s0_ref,
                      of_ref, ob_ref, sfin_ref, st_ref):
    j = pl.program_id(1)

    @pl.when(j == 0)
    def _():
        st_ref[...] = s0_ref[0]

    c = rf_ref.shape[1]
    c2 = 2 * c
    dir_refs = ((rf_ref, vf_ref, kkf_ref, lw0_ref, b0_ref, kd0_ref, of_ref),
                (rb_ref, vb_ref, kkb_ref, lw1_ref, b1_ref, kd1_ref, ob_ref))
    ti = lax.broadcasted_iota(jnp.int32, (c, c), 0)
    si = lax.broadcasted_iota(jnp.int32, (c, c), 1)
    t2 = lax.broadcasted_iota(jnp.int32, (c2, c2), 0)
    s2 = lax.broadcasted_iota(jnp.int32, (c2, c2), 1)
    same = jnp.where((t2 >= c) == (s2 >= c), 1, 0)
    eye2 = jnp.where(t2 == s2, 1.0, 0.0)
    head0 = lax.broadcasted_iota(jnp.int32, (c, RW_PAIR), 1) < RW_HEAD
    masks = []
    for rev in (False, True):
        tri = jnp.where((si >= ti) if rev else (si <= ti), 1.0, 0.0)
        before = jnp.where((s2 > t2) if rev else (s2 < t2), same, 0) == 1
        upto = jnp.where((s2 >= t2) if rev else (s2 <= t2), same, 0) == 1
        masks.append((tri, before, upto))

    def stack(t):
        return jnp.concatenate([jnp.where(head0, t, 0.0), jnp.where(head0, 0.0, t)], axis=0)

    def unstack(t):
        return t[:c] + t[c:]

    mm = functools.partial(_mm, mode="bf16")
    items = [(d, p) for d in range(2) for p in range(N_PAIRS)]
    pair_sl = lambda p: slice(p * RW_PAIR, (p + 1) * RW_PAIR)

    lams = [_mm(masks[d][0], dir_refs[d][3][0], mode="f32") for d in range(2)]

    ops = []
    for d, p in items:
        r_ref, v_ref, kk_ref, lw_ref, b_ref, kd_ref, _ = dir_refs[d]
        sl = pair_sl(p)
        lam = lams[d][:, sl]
        lam_tot = lam[0:1] if d == 1 else lam[c - 1:c]
        e_neg = jnp.exp(-lam)
        e_tail = jnp.exp(lam_tot - lam)
        b, kd = b_ref[0, :, sl], kd_ref[0, :, sl]
        ops.append(dict(
            at2=stack(-kk_ref[0, :, sl] * jnp.exp(lam - lw_ref[0, :, sl])),
            rt2=stack(r_ref[0, :, sl] * jnp.exp(lam)),
            bh2=stack(b * e_neg), kh2=stack(kd * e_neg),
            bt2=stack(b * e_tail), kt2=stack(kd * e_tail),
            v2=stack(v_ref[0, :, sl]),
            decay_tot=jnp.where(eye2 == 1.0, jnp.exp(lam_tot), 0.0)))

    for (d, p), o in zip(items, ops):
        _, before, upto = masks[d]
        x = mm(jnp.concatenate([o["at2"], o["rt2"]], axis=0),
               jnp.concatenate([o["bh2"], o["kh2"]], axis=0), _NT)
        o["a_ab"] = jnp.where(before, x[:c2, :c2], 0.0)
        o["a_ak"] = jnp.where(before, x[:c2, c2:], 0.0)
        o["a_rb"] = jnp.where(upto, x[c2:, :c2], 0.0)
        o["a_rk"] = jnp.where(upto, x[c2:, c2:], 0.0)

    for o in ops:
        o["t_inv"] = eye2 + o["a_ab"]
        o["pw"] = mm(o["a_ab"], o["a_ab"])
        o["w"] = mm(o["a_ak"], o["v2"])
    for _ in range(int(math.log2(c)) - 2):
        for o in ops:
            y = mm(jnp.concatenate([o["pw"], o["t_inv"]], axis=0), o["pw"])
            o["pw"] = y[:c2]
            o["t_inv"] = o["t_inv"] + y[c2:]
    for o in ops:
        o["t_inv"] = o["t_inv"] + mm(o["t_inv"], o["pw"])

    for o in ops:
        o["z"] = mm(o["t_inv"], jnp.concatenate([o["at2"], o["w"]], axis=1))
    for (d, p), o in zip(items, ops):
        y1 = mm(o["a_rb"], o["z"])
        y2 = mm(o["bt2"].T, o["z"])
        y3 = mm(jnp.concatenate([o["a_rk"], o["kt2"].T], axis=0), o["v2"])
        rp = unstack(o["rt2"] + y1[:, :c2])
        o0 = unstack(y1[:, c2:] + y3[:c2])
        m_bd = o["decay_tot"] + y2[:, :c2]
        n_bd = y2[:, c2:] + y3[c2:]
        rows = slice(d * RW_PAIR, (d + 1) * RW_PAIR)
        sl = pair_sl(p)
        state = st_ref[rows, sl]
        dir_refs[d][6][0, :, sl] = mm(rp, state) + o0
        st_ref[rows, sl] = mm(m_bd, state) + n_bd

    @pl.when(j == pl.num_programs(1) - 1)
    def _():
        sfin_ref[0] = st_ref[...]


def _rwkv_scan(st, s0):
    bsz, n_rows, w = st["r"].shape
    nch = n_rows // CHUNK
    fwd = pl.BlockSpec((1, CHUNK, w), lambda b, j: (b, j, 0))
    bwd = pl.BlockSpec((1, CHUNK, w), lambda b, j: (b, nch - 1 - j, 0))
    sspec = pl.BlockSpec((1, 2 * RW_PAIR, w), lambda b, j: (b, 0, 0))
    row_shape = jax.ShapeDtypeStruct((bsz, n_rows, w), F32)
    return pl.pallas_call(
        _rwkv_scan_kernel,
        grid=(bsz, nch),
        in_specs=[fwd] * 6 + [bwd] * 6 + [sspec],
        out_specs=[fwd, bwd, sspec],
        out_shape=[row_shape, row_shape, jax.ShapeDtypeStruct((bsz, 2 * RW_PAIR, w), F32)],
        scratch_shapes=[pltpu.VMEM((2 * RW_PAIR, w), F32)],
        compiler_params=_cparams("parallel", "arbitrary"),
        name="rwkv_scan",
    )(st["r"], st["v"], st["kk"], st["lw"][0], st["b"][0], st["kd"][0],
      st["r"], st["v"], st["kk"], st["lw"][1], st["b"][1], st["kd"][1], s0)


def _even_out_kernel(x_ref, gate_ref, om_ref, of_ref, ob_ref, bonus_ref, g_ref, lnw_ref, lnb_ref,
                     seg_ref, w_ref, o_ref):
    seg = seg_ref[...]
    o = of_ref[0] + ob_ref[0]
    mu = _mm(o, seg, mode="lhs2") * (1.0 / RW_HEAD)
    dlt = o - mu
    var = _mm(dlt * dlt, seg, mode="lhs2") * (1.0 / RW_HEAD)
    o_rw = dlt * lax.rsqrt(var + RW_GN_EPS) * lnw_ref[...] + lnb_ref[...] + bonus_ref[0]
    g = g_ref[0]
    z_m = (om_ref[0] * _silu(g[:, :MLA_WIDTH])).astype(BF16)
    z_r = (o_rw * _silu(g[:, MLA_WIDTH:])).astype(BF16)
    y = (jnp.dot(z_m, w_ref[:MLA_WIDTH, :], preferred_element_type=F32)
         + jnp.dot(z_r, w_ref[MLA_WIDTH:, :], preferred_element_type=F32))
    o_ref[0] = x_ref[0] + gate_ref[0] * y


def _even_out(x, gate, o_mla, o_f, o_b, bonus, g, ln_w, ln_b, seg, w_out):
    bsz, n_rows, d = x.shape
    tm = min(ROW_TILE, n_rows)
    rows = lambda width: pl.BlockSpec((1, tm, width), lambda b, i: (b, i, 0))
    full = lambda a: pl.BlockSpec(a.shape, lambda b, i: (0,) * a.ndim)
    return pl.pallas_call(
        _even_out_kernel,
        grid=(bsz, n_rows // tm),
        in_specs=[rows(d), pl.BlockSpec((1, 1, d), lambda b, i: (b, 0, 0)),
                  rows(MLA_WIDTH), rows(RW_WIDTH), rows(RW_WIDTH), rows(RW_WIDTH), rows(d),
                  full(ln_w), full(ln_b), full(seg), full(w_out)],
        out_specs=rows(d),
        out_shape=jax.ShapeDtypeStruct(x.shape, F32),
        compiler_params=_cparams("parallel", "arbitrary"),
        name="even_out",
    )(x, gate, o_mla, o_f, o_b, bonus, g, ln_w, ln_b, seg, w_out)


def _conv3_kernel(u_ref, prev_ref, next_ref, w_ref, b_ref, v_ref, x0_ref):
    u = u_ref[0]
    prev, nxt = _shifted_rows(u, prev_ref, next_ref)
    cv = prev * w_ref[0:1, :] + u * w_ref[1:2, :] + nxt * w_ref[2:3, :] + b_ref[...]
    hw = HY_WIDTH
    x0_ref[0] = cv[:, :hw]
    v_ref[0] = cv[:, 2 * hw:] * cv[:, hw:2 * hw]


def _conv3(u, conv_w, conv_b):
    bsz, n_rows, width = u.shape
    tm = min(ROW_TILE, n_rows)
    main, prev, nxt = _halo_specs(tm, width, n_rows)
    out_spec = pl.BlockSpec((1, tm, HY_WIDTH), lambda b, i: (b, i, 0))
    out_shape = jax.ShapeDtypeStruct((bsz, n_rows, HY_WIDTH), F32)
    return pl.pallas_call(
        _conv3_kernel,
        grid=(bsz, n_rows // tm),
        in_specs=[main, prev, nxt,
                  pl.BlockSpec(conv_w.shape, lambda b, i: (0, 0)),
                  pl.BlockSpec((1, width), lambda b, i: (0, 0))],
        out_specs=[out_spec, out_spec],
        out_shape=[out_shape, out_shape],
        compiler_params=_cparams("parallel", "arbitrary"),
        name="hyena_conv3",
    )(u, u, u, conv_w, conv_b.reshape(1, width))


def _hy_hidden_kernel(w1_ref, b1_ref, w2_ref, b2_ref, freq_ref, h_ref):
    n_pos = h_ref.shape[0]
    pos = lax.broadcasted_iota(jnp.int32, (n_pos, 1), 0).astype(F32)
    lane = lax.broadcasted_iota(jnp.int32, (1, LANES), 1)
    band_idx = jnp.where(lane <= HY_BANDS, lane - 1, lane - 1 - HY_BANDS).astype(F32)
    band = 1e-4 + band_idx * ((HY_BANDS - 1 - 1e-4) / (HY_BANDS - 1))
    ang = pos * (2.0 * math.pi / n_pos) * band
    z = jnp.where(lane == 0, pos / (n_pos - 1),
                  jnp.where(lane <= HY_BANDS, jnp.cos(ang),
                            jnp.where(lane <= 2 * HY_BANDS, -jnp.sin(ang), 0.0)))
    freq = freq_ref[...]
    hdn = jnp.sin(freq * (_mm(z, w1_ref[...], mode="f32") + b1_ref[...]))
    for j in range(HY_INNER):
        hdn = jnp.sin(freq * (_mm(hdn, w2_ref[j], mode="f32") + b2_ref[j]))
    h_ref[...] = hdn


def _hy_filter_kernel(h_ref, w0_ref, w1_ref, dl_ref, fs_ref, fd_ref):
    n_pos = h_ref.shape[0]
    pos = lax.broadcasted_iota(jnp.int32, (n_pos, 1), 0)
    t = pos.astype(F32) / (n_pos - 1)
    dec = jnp.exp(-t * dl_ref[...])
    hdn = h_ref[...]
    f_fwd = _mm(hdn, w0_ref[...], mode="f32") * dec
    f_bwd = jnp.where(pos == 0, 0.0, _mm(hdn, w1_ref[...], mode="f32") * dec)
    inv = 1.0 / (jnp.sum(jnp.abs(f_fwd), axis=0, keepdims=True)
                 + jnp.sum(jnp.abs(f_bwd), axis=0, keepdims=True))
    fs_ref[...] = (f_fwd + f_bwd) * inv
    fd_ref[...] = (f_bwd - f_fwd) * inv


def _hyena_filters(n_pos, f_w1, f_b1, f_w2, f_b2, f_wout, freq, deltas):
    w1 = jnp.zeros((LANES, HY_ORDER), F32).at[:HY_EMB].set(f_w1)
    hdn = pl.pallas_call(
        _hy_hidden_kernel,
        out_shape=jax.ShapeDtypeStruct((n_pos, HY_ORDER), F32),
        name="hyena_filter_hidden",
    )(w1, f_b1.reshape(1, HY_ORDER), f_w2, f_b2.reshape(HY_INNER, 1, HY_ORDER),
      freq.reshape(1, HY_ORDER))
    tn = 256
    cspec = pl.BlockSpec((HY_ORDER, tn), lambda j: (0, j))
    ospec = pl.BlockSpec((n_pos, tn), lambda j: (0, j))
    oshape = jax.ShapeDtypeStruct((n_pos, HY_WIDTH), F32)
    return pl.pallas_call(
        _hy_filter_kernel,
        grid=(HY_WIDTH // tn,),
        in_specs=[pl.BlockSpec((n_pos, HY_ORDER), lambda j: (0, 0)), cspec, cspec,
                  pl.BlockSpec((1, tn), lambda j: (0, j))],
        out_specs=[ospec, ospec],
        out_shape=[oshape, oshape],
        compiler_params=_cparams("arbitrary"),
        name="hyena_filter",
    )(hdn, f_wout[:, :HY_WIDTH], f_wout[:, HY_WIDTH:], deltas)


def _dft_tables_kernel(c_ref, s_ref):
    tr, n_half = c_ref.shape
    n = 2 * n_half
    k = lax.broadcasted_iota(jnp.int32, (tr, n_half), 0) + pl.program_id(0) * tr
    s = lax.broadcasted_iota(jnp.int32, (tr, n_half), 1)
    ph = (k * s) & (n - 1)
    ph = jnp.where(ph >= n_half, ph - n, ph)
    ang = ph.astype(F32) * (2.0 * math.pi / n)
    c_ref[...] = jnp.cos(ang).astype(BF16)
    s_ref[...] = jnp.sin(ang).astype(BF16)


def _dft_tables(n_half):
    tr = min(256, n_half)
    spec = pl.BlockSpec((tr, n_half), lambda i: (i, 0))
    shape = jax.ShapeDtypeStruct((n_half, n_half), BF16)
    return pl.pallas_call(
        _dft_tables_kernel,
        grid=(n_half // tr,),
        out_specs=[spec, spec],
        out_shape=[shape, shape],
        compiler_params=_cparams("arbitrary"),
        name="dft_tables",
    )()


def _alt_sign(n_pos):
    pos = lax.broadcasted_iota(jnp.int32, (n_pos, 1), 0)
    return jnp.where((pos & 1) == 0, 1.0, -1.0), pos


def _spectrum_kernel(fs_ref, fd_ref, c_ref, s_ref, hre_ref, him_ref, hny_ref):
    n_pos = fs_ref.shape[0]
    alt, pos = _alt_sign(n_pos)
    fs, fd = fs_ref[...], fd_ref[...]

    def mm2(tab, f):
        fh = f.astype(BF16)
        fl = (f - fh.astype(F32)).astype(BF16)
        return (jnp.dot(tab, fh, preferred_element_type=F32)
                + jnp.dot(tab, fl, preferred_element_type=F32))

    scale = 1.0 / n_pos
    hre_ref[...] = mm2(c_ref[...], fs) * jnp.where(pos == 0, 0.5 * scale, scale)
    him_ref[...] = mm2(s_ref[...], fd) * scale
    hny_ref[...] = jnp.sum(fs * alt, axis=0, keepdims=True) * (0.5 * scale)


def _table_spec(tab, nd):
    return pl.BlockSpec(tab.shape, (lambda j: (0, 0)) if nd == 1 else (lambda b, j: (0, 0)),
                        pipeline_mode=pl.Buffered(1))


def _spectrum(fs, fd, ctab, stab):
    n_pos, width = fs.shape
    tc = DFT_TC
    col = pl.BlockSpec((n_pos, tc), lambda j: (0, j))
    return pl.pallas_call(
        _spectrum_kernel,
        grid=(width // tc,),
        in_specs=[col, col, _table_spec(ctab, 1), _table_spec(stab, 1)],
        out_specs=[col, col, pl.BlockSpec((1, tc), lambda j: (0, j))],
        out_shape=[jax.ShapeDtypeStruct((n_pos, width), F32)] * 2
        + [jax.ShapeDtypeStruct((1, width), F32)],
        compiler_params=_cparams("arbitrary"),
        name="hyena_spectrum",
    )(fs, fd, ctab, stab)


def _dft_conv_kernel(v_ref, hre_ref, him_ref, hny_ref, c_ref, s_ref, y_ref):
    n_pos = v_ref.shape[1]
    alt, _ = _alt_sign(n_pos)
    v = v_ref[0]
    vb = v.astype(BF16)
    ctab, stab = c_ref[...], s_ref[...]
    v_re = jnp.dot(ctab, vb, preferred_element_type=F32)
    v_s = jnp.dot(stab, vb, preferred_element_type=F32)
    hre, him = hre_ref[...], him_ref[...]
    y_re = (v_re * hre + v_s * him).astype(BF16)
    y_im = (v_re * him - v_s * hre).astype(BF16)
    nyq = jnp.sum(v * alt, axis=0, keepdims=True) * hny_ref[...]
    y_ref[0] = (jnp.dot(ctab, y_re, preferred_element_type=F32)
                - jnp.dot(stab, y_im, preferred_element_type=F32) + alt * nyq)


def _dft_conv(v, hre, him, hny, ctab, stab):
    bsz, n_pos, width = v.shape
    tc = DFT_TC
    col = pl.BlockSpec((n_pos, tc), lambda b, j: (0, j))
    vspec = pl.BlockSpec((1, n_pos, tc), lambda b, j: (b, 0, j))
    return pl.pallas_call(
        _dft_conv_kernel,
        grid=(bsz, width // tc),
        in_specs=[vspec, col, col, pl.BlockSpec((1, tc), lambda b, j: (0, j)),
                  _table_spec(ctab, 2), _table_spec(stab, 2)],
        out_specs=vspec,
        out_shape=jax.ShapeDtypeStruct(v.shape, F32),
        compiler_params=_cparams("parallel", "arbitrary"),
        name="hyena_dft_conv",
    )(v, hre, him, hny, ctab, stab)


def _hy_out_kernel(x_ref, gate_ref, y_ref, v_ref, x0_ref, g_ref, bias_ref, w_ref, o_ref):
    z = (y_ref[0] + v_ref[0] * bias_ref[...]) * x0_ref[0] * _silu(g_ref[0])
    o_ref[0] = x_ref[0] + gate_ref[0] * jnp.dot(z.astype(BF16), w_ref[...],
                                                preferred_element_type=F32)


def _hy_out(x, gate, y, v, x0, g, bias_d, w_out):
    bsz, n_rows, d = x.shape
    tm = min(ROW_TILE, n_rows)
    rows = pl.BlockSpec((1, tm, d), lambda b, i: (b, i, 0))
    return pl.pallas_call(
        _hy_out_kernel,
        grid=(bsz, n_rows // tm),
        in_specs=[rows, pl.BlockSpec((1, 1, d), lambda b, i: (b, 0, 0)), rows, rows, rows, rows,
                  pl.BlockSpec((1, d), lambda b, i: (0, 0)),
                  pl.BlockSpec(w_out.shape, lambda b, i: (0, 0))],
        out_specs=rows,
        out_shape=jax.ShapeDtypeStruct(x.shape, F32),
        compiler_params=_cparams("parallel", "arbitrary"),
        name="hyena_out",
    )(x, gate, y, v, x0, g, bias_d.reshape(1, d), w_out)


def _rope_perm():
    pairs = np.arange(MLA_ROPE // 2)
    return np.concatenate([np.arange(MLA_NOPE), MLA_NOPE + 2 * pairs, MLA_NOPE + 2 * pairs + 1])


def _rope_tables(n_pos):
    rows = n_pos // GRID_W
    row = jnp.repeat(jnp.arange(rows, dtype=F32), GRID_W)
    col = jnp.tile(jnp.arange(GRID_W, dtype=F32), rows)
    n_freq = MLA_ROPE // 4
    inv = ROPE_BASE ** (-jnp.arange(n_freq, dtype=F32) / n_freq)
    ang = jnp.concatenate([row[:, None] * inv, col[:, None] * inv], axis=-1)
    cos, sin = jnp.cos(ang), jnp.sin(ang)
    half = MLA_ROPE // 2
    ones = jnp.ones((n_pos, MLA_NOPE), F32)
    zeros = jnp.zeros((n_pos, MLA_NOPE), F32)
    pad1 = jnp.ones((n_pos, HEAD_PAD - MLA_QK), F32)
    pad0 = jnp.zeros((n_pos, HEAD_PAD - MLA_QK), F32)
    z16 = jnp.zeros((n_pos, half), F32)
    cos_f = jnp.concatenate([ones, cos, cos, pad1], axis=-1)
    sin_a = jnp.concatenate([zeros, z16, sin, pad0], axis=-1)
    sin_b = jnp.concatenate([zeros, -sin, z16, pad0], axis=-1)
    return cos_f, sin_a, sin_b


def _pad_heads(w, width):
    k, h, _ = w.shape
    return jnp.zeros((k, h, HEAD_PAD), w.dtype).at[:, :, :width].set(w).reshape(k, h * HEAD_PAD)


def _even_weights(e, ev_w_in, ev_w_out, mla_q_a_norm, mla_w_uq, mla_kv_a_norm, mla_w_ukv,
                  mla_q_norm, mla_k_norm, rwkv_mu_prev, rwkv_mu_next, rwkv_w0, rwkv_w_up, rwkv_a0,
                  rwkv_a_up, rwkv_k_k, rwkv_k_a, rwkv_r_k, rwkv_ln_w, rwkv_ln_b):
    perm = _rope_perm()
    w_in = ev_w_in[e]
    d = w_in.shape[0]
    o_dkv = EV_DQ
    o_rw = EV_DQ + EV_DKV
    o_g = o_rw + RW_SHIFT
    w_dkv = jnp.zeros((d, 2 * LANES), F32)
    w_dkv = w_dkv.at[:, :MLA_KV_LORA].set(w_in[:, o_dkv:o_dkv + MLA_KV_LORA])
    rope_cols = o_dkv + MLA_KV_LORA + (perm[MLA_NOPE:] - MLA_NOPE)
    w_dkv = w_dkv.at[:, LANES + MLA_NOPE:LANES + MLA_QK].set(w_in[:, rope_cols])
    proj = [w_in[:, :EV_DQ].astype(BF16), w_dkv.astype(BF16),
            w_in[:, o_rw:o_g].astype(BF16), w_in[:, o_g:].astype(BF16)]
    w_ukv = mla_w_ukv[e]
    pad_gain = lambda g: jnp.zeros((1, HEAD_PAD), F32).at[0, :MLA_QK].set(g[perm])
    seg_id = np.arange(RW_WIDTH) // RW_HEAD
    zero_up = lambda up, dd: jnp.zeros((2 * RW_LORA_W, RW_WIDTH), F32).at[
        dd * RW_LORA_W:(dd + 1) * RW_LORA_W].set(up[dd])
    rw = dict(
        mu_prev=rwkv_mu_prev[e].reshape(1, RW_SHIFT), mu_next=rwkv_mu_next[e].reshape(1, RW_SHIFT),
        w0=rwkv_w0[e], a0=rwkv_a0[e],
        w_up=jnp.stack([zero_up(rwkv_w_up[e], 0), zero_up(rwkv_w_up[e], 1)]),
        a_up=jnp.stack([zero_up(rwkv_a_up[e], 0), zero_up(rwkv_a_up[e], 1)]),
        k_k=rwkv_k_k[e].reshape(1, RW_WIDTH), k_a=rwkv_k_a[e].reshape(1, RW_WIDTH),
        r_k=rwkv_r_k[e].reshape(2, RW_WIDTH),
        seg=jnp.asarray((seg_id[:, None] == seg_id[None, :]).astype(np.float32)),
    )
    head_par = np.arange(MLA_HEADS) % 2
    v_lane = (np.arange(HEAD_PAD)[None, :] >= MLA_V) == (head_par[:, None] == 1)
    v_ones = (~v_lane).astype(np.float32)
    w_v = jnp.zeros((MLA_KV_LORA, MLA_HEADS, HEAD_PAD), F32)
    w_v = w_v.at[:, 0::2, :MLA_V].set(w_ukv[:, 0::2, MLA_NOPE:]).at[:, 1::2, MLA_V:].set(
        w_ukv[:, 1::2, MLA_NOPE:])
    return dict(
        proj=proj,
        q_a_norm=mla_q_a_norm[e], kv_a_norm=mla_kv_a_norm[e],
        w_uq=_pad_heads(mla_w_uq[e][:, :, perm], MLA_QK).astype(BF16),
        w_k=_pad_heads(w_ukv[:, :, :MLA_NOPE], MLA_NOPE).astype(BF16),
        w_v=w_v.reshape(MLA_KV_LORA, MLA_HEADS * HEAD_PAD).astype(BF16),
        v_ones=jnp.asarray(v_ones.reshape(1, MLA_HEADS * HEAD_PAD)),
        q_gain=pad_gain(mla_q_norm[e]) * (MLA_QK ** -0.5 * math.log2(math.e)),
        k_gain=pad_gain(mla_k_norm[e]),
        rw=rw,
        ln_w=rwkv_ln_w[e].reshape(1, RW_WIDTH), ln_b=rwkv_ln_b[e].reshape(1, RW_WIDTH),
        w_out=ev_w_out[e].astype(BF16),
    )


def _rwkv_branch(st_c, st, want_ctx):
    bsz = st["r"].shape[0]
    zero_state = jnp.zeros((bsz, 2 * RW_PAIR, RW_WIDTH), F32)
    of_c, ob_c, s_c = _rwkv_scan(st_c, zero_state)
    o_f, o_b, _ = _rwkv_scan(st, s_c)
    return [o_f, o_b], ([of_c, ob_c] if want_ctx else None)


def _even_layer(x, xc, mod, mod_c, g_norm, wts, rope_tabs, ctx_out):
    shift, scale1p, gate = mod
    shift_c, scale1p_c, gate_c = mod_c
    p_dq, p_dkv, p_rw, g = _norm_proj(x, g_norm, scale1p, shift, wts["proj"])
    pc_dq, pc_dkv, pc_rw, gc = _norm_proj(xc, g_norm, scale1p_c, shift_c, wts["proj"])
    q = _q_prep(p_dq, wts["q_a_norm"], wts["w_uq"], wts["q_gain"], rope_tabs)
    k, v = _kv_prep(p_dkv, wts["kv_a_norm"], wts["w_k"], wts["w_v"], wts["v_ones"], wts["k_gain"], rope_tabs)
    kc, vc = _kv_prep(pc_dkv, wts["kv_a_norm"], wts["w_k"], wts["w_v"], wts["v_ones"], wts["k_gain"], None)
    o_mla = _attention(q, [k, kc], [v, vc])
    st_c = _rw_prep(pc_rw, wts["rw"])
    st = _rw_prep(p_rw, wts["rw"])
    outs, outs_c = _rwkv_branch(st_c, st, ctx_out)
    seg = wts["rw"]["seg"]
    x_new = _even_out(x, gate, o_mla, outs[0], outs[1], st["bonus"], g, wts["ln_w"], wts["ln_b"],
                      seg, wts["w_out"])
    if not ctx_out:
        return x_new, None
    qc = _q_prep(pc_dq, wts["q_a_norm"], wts["w_uq"], wts["q_gain"], None)
    oc_mla = _attention(qc, [kc], [vc])
    xc_new = _even_out(xc, gate_c, oc_mla, outs_c[0], outs_c[1], st_c["bonus"], gc, wts["ln_w"],
                       wts["ln_b"], seg, wts["w_out"])
    return x_new, xc_new


def _hyena_layer(x, mod, g_norm, wts, tables):
    shift, scale1p, gate = mod
    n_pos = x.shape[1]
    ctab, stab = tables
    u, g = _norm_proj(x, g_norm, scale1p, shift, wts["proj"])
    v, x0 = _conv3(u, wts["conv_w"], wts["conv_b"])
    fs, fd = _hyena_filters(n_pos, *wts["filt"])
    hre, him, hny = _spectrum(fs, fd, ctab, stab)
    y = _dft_conv(v, hre, him, hny, ctab, stab)
    return _hy_out(x, gate, y, v, x0, g, wts["bias_d"], wts["w_out"])


def kernel(x, c, ctx, c_ctx, mod_w, mod_b, norm_g, ev_w_in, ev_w_out, mla_q_a_norm, mla_w_uq, mla_kv_a_norm, mla_w_ukv, mla_q_norm, mla_k_norm, rwkv_mu_prev, rwkv_mu_next, rwkv_w0, rwkv_w_up, rwkv_a0, rwkv_a_up, rwkv_k_k, rwkv_k_a, rwkv_r_k, rwkv_ln_w, rwkv_ln_b, od_w_in, od_w_out, hy_conv_w, hy_conv_b, hy_bias_d, hy_f_w1, hy_f_b1, hy_f_w2, hy_f_b2, hy_f_wout, hy_freq):
    bsz, n_lat, d = x.shape
    n_ctx = ctx.shape[1]
    assert n_lat % max(CHUNK, GRID_W) == 0 and n_ctx % CHUNK == 0 and d == D_MODEL
    assert CHUNK == RW_HEAD and 2 * CHUNK == RW_PAIR

    n_rows = -(-(bsz + 1) // 16) * 16
    cvec = jnp.zeros((n_rows, d), F32).at[:bsz].set(c).at[bsz].set(c_ctx)
    mods = _modulation(cvec, mod_w, mod_b)

    def split_mod(i, lo, hi, reps):
        m = mods[i, lo:hi]
        m = jnp.broadcast_to(m, (reps, 3 * d)) if hi - lo == 1 else m
        m = m[:, None, :]
        return m[..., :d], 1.0 + m[..., d:2 * d], m[..., 2 * d:]

    rope_tabs = _rope_tables(n_lat)
    deltas = jnp.abs(jnp.linspace(math.log(HY_TARGET) / HY_FAST_DECAY,
                                  math.log(HY_TARGET) / HY_SLOW_DECAY, HY_WIDTH,
                                  dtype=F32)).reshape(1, HY_WIDTH)
    tables = {n_lat: _dft_tables(n_lat)}

    xc = ctx
    for i in range(DEPTH):
        ctx_needed_later = any(j > i and j % 2 == 0 for j in range(DEPTH))
        mod = split_mod(i, 0, bsz, bsz)
        mod_c = split_mod(i, bsz, bsz + 1, bsz)
        if i % 2 == 0:
            wts = _even_weights(i // 2, ev_w_in, ev_w_out, mla_q_a_norm, mla_w_uq, mla_kv_a_norm,
                                mla_w_ukv, mla_q_norm, mla_k_norm, rwkv_mu_prev, rwkv_mu_next,
                                rwkv_w0, rwkv_w_up, rwkv_a0, rwkv_a_up, rwkv_k_k, rwkv_k_a,
                                rwkv_r_k, rwkv_ln_w, rwkv_ln_b)
            x, xc_new = _even_layer(x, xc, mod, mod_c, norm_g[i], wts, rope_tabs, ctx_needed_later)
            xc = xc_new if ctx_needed_later else xc
        else:
            o = i // 2
            w_in = od_w_in[o]
            wts = dict(
                proj=[w_in[:, :3 * HY_WIDTH].astype(BF16), w_in[:, 3 * HY_WIDTH:].astype(BF16)],
                conv_w=hy_conv_w[o], conv_b=hy_conv_b[o], bias_d=hy_bias_d[o],
                filt=(hy_f_w1[o], hy_f_b1[o], hy_f_w2[o], hy_f_b2[o], hy_f_wout[o], hy_freq[o],
                      deltas),
                w_out=od_w_out[o].astype(BF16),
            )
            if ctx_needed_later:
                if n_ctx not in tables:
                    tables[n_ctx] = _dft_tables(n_ctx)
                xc = _hyena_layer(xc, mod_c, norm_g[i], wts, tables[n_ctx])
            x = _hyena_layer(x, mod, norm_g[i], wts, tables[n_lat])
    return x
```

```python
import functools
import math

import numpy as np
import jax
import jax.numpy as jnp
from jax import lax
from jax.experimental import pallas as pl
from jax.experimental.pallas import tpu as pltpu

F32 = jnp.float32
BF16 = jnp.bfloat16
HIGHEST = lax.Precision.HIGHEST

D_MODEL = 1024
DEPTH = 4
GRID_W = 64
NORM_EPS = 1e-6
MLA_HEADS = 8
MLA_NOPE = 64
MLA_ROPE = 32
MLA_QK = MLA_NOPE + MLA_ROPE
MLA_V = 64
MLA_Q_LORA = 256
MLA_KV_LORA = 128
MLA_WIDTH = MLA_HEADS * MLA_V
ROPE_BASE = 10000.0
RW_HEADS = 8
RW_HEAD = 64
RW_WIDTH = RW_HEADS * RW_HEAD
RW_LORA_W = 64
RW_LORA_A = 64
RW_SHIFT = 3 * RW_WIDTH + 2 * RW_LORA_W + 2 * RW_LORA_A
RW_GN_EPS = 64e-5
EV_DQ = MLA_Q_LORA
EV_DKV = MLA_KV_LORA + MLA_ROPE
HY_WIDTH = D_MODEL
HY_ORDER = 64
HY_BANDS = 16
HY_EMB = 1 + 2 * HY_BANDS
HY_INNER = 2
HY_FAST_DECAY = 0.3
HY_SLOW_DECAY = 1.5
HY_TARGET = 1e-2

LANES = 128
HEAD_PAD = 128
RW_PAIR = 2 * RW_HEAD
N_PAIRS = RW_WIDTH // RW_PAIR
CHUNK = 64
ROW_TILE = 256
ATTN_TQ = 256
ATTN_TK = 256
DFT_TC = 256
DFT_TM = 256
HALO_ROWS = 16
SUB_ROWS = 128
VMEM_LIMIT = 56 * 1024 * 1024

_NN = (((1,), (0,)), ((), ()))
_NT = (((1,), (1,)), ((), ()))


def _mm(a, b, dn=_NN, mode="bf16"):
    if mode == "f32":
        return lax.dot_general(a, b, dn, precision=HIGHEST, preferred_element_type=F32)
    dg = functools.partial(lax.dot_general, dimension_numbers=dn, preferred_element_type=F32)
    ah = a.astype(BF16)
    bh = b.astype(BF16)
    if mode == "bf16":
        return dg(ah, bh)
    al = (a - ah.astype(F32)).astype(BF16)
    if mode == "lhs2":
        return dg(ah, bh) + dg(al, bh)
    bl = (b - bh.astype(F32)).astype(BF16)
    return dg(ah, bh) + (dg(ah, bl) + dg(al, bh))


def _cparams(*sem):
    return pltpu.CompilerParams(dimension_semantics=sem, vmem_limit_bytes=VMEM_LIMIT)


def _silu(t):
    return t * jax.nn.sigmoid(t)


def _shifted_rows(p, prev_ref, next_ref):
    tm = p.shape[0]
    i = pl.program_id(1)
    last = pl.num_programs(1) - 1
    prev_row = jnp.where(i > 0, prev_ref[0, HALO_ROWS - 1:HALO_ROWS, :].astype(F32), 0.0)
    next_row = jnp.where(i < last, next_ref[0, 0:1, :].astype(F32), 0.0)
    rows = lax.broadcasted_iota(jnp.int32, (tm, 1), 0)
    prev = jnp.where(rows == 0, prev_row, pltpu.roll(p, 1, axis=0))
    nxt = jnp.where(rows == tm - 1, next_row, pltpu.roll(p, tm - 1, axis=0))
    return prev, nxt


def _halo_specs(tm, width, n_rows):
    th = tm // HALO_ROWS
    last_h = n_rows // HALO_ROWS - 1
    main = pl.BlockSpec((1, tm, width), lambda b, i: (b, i, 0))
    prev = pl.BlockSpec((1, HALO_ROWS, width), lambda b, i: (b, jnp.maximum(i * th - 1, 0), 0))
    nxt = pl.BlockSpec((1, HALO_ROWS, width),
                       lambda b, i: (b, jnp.minimum((i + 1) * th, last_h), 0))
    return main, prev, nxt


def _mod_kernel(c_ref, w_ref, b_ref, o_ref):
    o_ref[0] = _mm(_silu(c_ref[...]), w_ref[0], mode="x3") + b_ref[0]


def _modulation(cvec, mod_w, mod_b):
    rows, d = cvec.shape
    n = mod_w.shape[-1]
    tn = 1024
    return pl.pallas_call(
        _mod_kernel,
        grid=(DEPTH, n // tn),
        in_specs=[pl.BlockSpec((rows, d), lambda i, j: (0, 0)),
                  pl.BlockSpec((1, d, tn), lambda i, j: (i, 0, j)),
                  pl.BlockSpec((1, 1, tn), lambda i, j: (i, 0, j))],
        out_specs=pl.BlockSpec((1, rows, tn), lambda i, j: (i, 0, j)),
        out_shape=jax.ShapeDtypeStruct((DEPTH, rows, n), F32),
        compiler_params=_cparams("arbitrary", "arbitrary"),
        name="modulation",
    )(cvec, mod_w, mod_b.reshape(DEPTH, 1, n))


def _norm_proj_kernel(nw, x_ref, g_ref, sc_ref, sh_ref, *refs):
    x = x_ref[0]
    h = x * lax.rsqrt(jnp.mean(x * x, axis=-1, keepdims=True) + NORM_EPS) * g_ref[...]
    hb = (h * sc_ref[0] + sh_ref[0]).astype(BF16)
    for w_ref, o_ref in zip(refs[:nw], refs[nw:]):
        o_ref[0] = jnp.dot(hb, w_ref[...], preferred_element_type=F32).astype(BF16)


def _norm_proj(x, g, scale1p, shift, weights):
    bsz, n_rows, d = x.shape
    tm = min(ROW_TILE, n_rows)
    vec = pl.BlockSpec((1, 1, d), lambda b, i: (b, 0, 0))
    in_specs = [pl.BlockSpec((1, tm, d), lambda b, i: (b, i, 0)),
                pl.BlockSpec((1, d), lambda b, i: (0, 0)), vec, vec]
    in_specs += [pl.BlockSpec(w.shape, lambda b, i: (0, 0)) for w in weights]
    return pl.pallas_call(
        functools.partial(_norm_proj_kernel, len(weights)),
        grid=(bsz, n_rows // tm),
        in_specs=in_specs,
        out_specs=[pl.BlockSpec((1, tm, w.shape[1]), lambda b, i: (b, i, 0)) for w in weights],
        out_shape=[jax.ShapeDtypeStruct((bsz, n_rows, w.shape[1]), BF16) for w in weights],
        compiler_params=_cparams("parallel", "arbitrary"),
        name="norm_proj",
    )(x, g.reshape(1, d), scale1p, shift, *weights)


def _head_norm_rope(t, gain, tabs):
    ms = jnp.sum(t * t, axis=-1, keepdims=True) * (1.0 / MLA_QK)
    t = t * lax.rsqrt(ms + NORM_EPS) * gain
    if tabs is not None:
        cos_f, sin_a, sin_b = tabs
        t = (t * cos_f + pltpu.roll(t, MLA_ROPE // 2, axis=1) * sin_a
             + pltpu.roll(t, HEAD_PAD - MLA_ROPE // 2, axis=1) * sin_b)
    return t


def _even_in_kernel(rope, x_ref, g_ref, sc_ref, sh_ref, wdq_ref, wdkv_ref, wrw_ref, wg_ref,
                    qan_ref, wuq_ref, qgn_ref, kan_ref, wk_ref, wv_ref, vone_ref, kgn_ref, *refs):
    q_ref, k_ref, v_ref, prw_ref, go_ref = refs[-5:]
    q_gain, k_gain = qgn_ref[...], kgn_ref[...]

    def rms(t, gain_ref):
        return t * lax.rsqrt(jnp.mean(t * t, axis=-1, keepdims=True) + NORM_EPS) * gain_ref[...]

    tm = x_ref.shape[1]
    sub = min(SUB_ROWS, tm)
    for r0 in range(0, tm, sub):
        rows = slice(r0, r0 + sub)
        tabs = tuple(r[rows, :] for r in refs[:3]) if rope else None
        hb = (rms(x_ref[0, rows, :], g_ref) * sc_ref[0] + sh_ref[0]).astype(BF16)
        prw_ref[0, rows, :] = jnp.dot(hb, wrw_ref[...], preferred_element_type=F32).astype(BF16)
        go_ref[0, rows, :] = jnp.dot(hb, wg_ref[...], preferred_element_type=F32).astype(BF16)
        p_dq = jnp.dot(hb, wdq_ref[...], preferred_element_type=F32)
        q = jnp.dot(rms(p_dq, qan_ref).astype(BF16), wuq_ref[...], preferred_element_type=F32)
        p_dkv = jnp.dot(hb, wdkv_ref[...], preferred_element_type=F32)
        k_rope = p_dkv[:, MLA_KV_LORA:]
        ab = rms(p_dkv[:, :MLA_KV_LORA], kan_ref).astype(BF16)
        k_nope = jnp.dot(ab, wk_ref[...], preferred_element_type=F32)
        v_ref[0, rows, :] = (jnp.dot(ab, wv_ref[...], preferred_element_type=F32)
                             + vone_ref[...]).astype(BF16)
        for hd in range(MLA_HEADS):
            sl = slice(hd * HEAD_PAD, (hd + 1) * HEAD_PAD)
            q_ref[0, rows, sl] = _head_norm_rope(q[:, sl], q_gain, tabs).astype(BF16)
            k_ref[0, rows, sl] = _head_norm_rope(k_nope[:, sl] + k_rope, k_gain,
                                                 tabs).astype(BF16)


def _even_in(x, g, scale1p, shift, wts, tabs):
    bsz, n_rows, d = x.shape
    tm = min(ROW_TILE, n_rows)
    rope = tabs is not None
    kw = MLA_HEADS * HEAD_PAD
    vec = pl.BlockSpec((1, 1, d), lambda b, i: (b, 0, 0))
    full = lambda a: pl.BlockSpec(a.shape, lambda b, i: (0,) * a.ndim)
    consts = list(wts["proj"]) + [wts["q_a_norm"].reshape(1, MLA_Q_LORA), wts["w_uq"], wts["q_gain"],
                                  wts["kv_a_norm"].reshape(1, MLA_KV_LORA), wts["w_k"], wts["w_v"],
                                  wts["v_ones"], wts["k_gain"]]
    in_specs = [pl.BlockSpec((1, tm, d), lambda b, i: (b, i, 0)),
                pl.BlockSpec((1, d), lambda b, i: (0, 0)), vec, vec] + [full(a) for a in consts]
    args = [x, g.reshape(1, d), scale1p, shift] + consts
    if rope:
        in_specs += [pl.BlockSpec((tm, HEAD_PAD), lambda b, i: (i, 0))] * 3
        args += list(tabs)
    widths = [kw, kw, kw, RW_SHIFT, d]
    return pl.pallas_call(
        functools.partial(_even_in_kernel, rope),
        grid=(bsz, n_rows // tm),
        in_specs=in_specs,
        out_specs=[pl.BlockSpec((1, tm, w), lambda b, i: (b, i, 0)) for w in widths],
        out_shape=[jax.ShapeDtypeStruct((bsz, n_rows, w), BF16) for w in widths],
        compiler_params=_cparams("parallel", "arbitrary"),
        name="even_in",
    )(*args)


def _attn_kernel(nseg, q_ref, *refs):
    k_refs, v_refs, o_ref = refs[:nseg], refs[nseg:2 * nseg], refs[2 * nseg]
    tq = q_ref.shape[1]
    lane = lax.broadcasted_iota(jnp.int32, (tq, LANES), 1)
    for hp in range(MLA_HEADS // 2):
        psl = slice(2 * hp * HEAD_PAD, (2 * hp + 2) * HEAD_PAD)
        outs = []
        for h in (2 * hp, 2 * hp + 1):
            sl = slice(h * HEAD_PAD, (h + 1) * HEAD_PAD)
            q = q_ref[0, :, sl]
            ss = [lax.dot_general(q, k_ref[0, :, sl], _NT, preferred_element_type=F32)
                  for k_ref in k_refs]
            m = functools.reduce(jnp.maximum, [jnp.max(s, axis=-1, keepdims=True) for s in ss])
            acc = None
            for s, v_ref in zip(ss, v_refs):
                for k0 in range(0, s.shape[1], ATTN_TK):
                    pv = jnp.dot(jnp.exp2(s[:, k0:k0 + ATTN_TK] - m).astype(BF16),
                                 v_ref[0, k0:k0 + ATTN_TK, psl], preferred_element_type=F32)
                    acc = pv if acc is None else pv + acc
            acc = acc[:, :HEAD_PAD] if h % 2 == 0 else acc[:, HEAD_PAD:]
            ones_lane = MLA_V if h % 2 == 0 else 0
            outs.append(acc * (1.0 / acc[:, ones_lane:ones_lane + 1]))
        o_ref[0, :, hp * LANES:(hp + 1) * LANES] = jnp.where(lane < MLA_V, outs[0],
                                                             outs[1]).astype(BF16)


def _attention(q, ks, vs):
    bsz, n_q, qw = q.shape
    tq = min(ATTN_TQ, n_q)
    nseg = len(ks)
    in_specs = [pl.BlockSpec((1, tq, qw), lambda b, i: (b, i, 0))]
    in_specs += [pl.BlockSpec((1,) + k.shape[1:], lambda b, i: (b, 0, 0)) for k in ks]
    in_specs += [pl.BlockSpec((1,) + v.shape[1:], lambda b, i: (b, 0, 0)) for v in vs]
    return pl.pallas_call(
        functools.partial(_attn_kernel, nseg),
        grid=(bsz, n_q // tq),
        in_specs=in_specs,
        out_specs=pl.BlockSpec((1, tq, MLA_WIDTH), lambda b, i: (b, i, 0)),
        out_shape=jax.ShapeDtypeStruct((bsz, n_q, MLA_WIDTH), BF16),
        compiler_params=_cparams("parallel", "arbitrary"),
        name="mla_attention",
    )(q, *ks, *vs)


def _rw_prep_kernel(p_ref, prev_ref, next_ref, mup_ref, mun_ref, w0_ref, wup_ref, a0_ref, aup_ref,
                    kk_ref, ka_ref, rk_ref, seg_ref,
                    r_out, v_out, kkn_out, bonus_out, lw0_out, lw1_out, b0_out, b1_out,
                    kd0_out, kd1_out):
    p = p_ref[0].astype(F32)
    prev, nxt = _shifted_rows(p, prev_ref, next_ref)
    ps = p + (prev - p) * mup_ref[...] + (nxt - p) * mun_ref[...]
    w = RW_WIDTH
    r, k, v = ps[:, :w], ps[:, w:2 * w], ps[:, 2 * w:3 * w]
    wd = jnp.tanh(ps[:, 3 * w:3 * w + 2 * RW_LORA_W])
    ad = ps[:, 3 * w + 2 * RW_LORA_W:]
    seg = seg_ref[...]
    kq = k * kk_ref[...]
    kk = kq * lax.rsqrt(jnp.maximum(_mm(kq * kq, seg, mode="lhs2"), 1e-24))
    r_out[0] = r.astype(BF16)
    v_out[0] = v.astype(BF16)
    kkn_out[0] = kk.astype(BF16)
    bonus_in = jnp.zeros_like(r)
    for d, (lw_out, b_out, kd_out) in enumerate(((lw0_out, b0_out, kd0_out),
                                                  (lw1_out, b1_out, kd1_out))):
        z = w0_ref[d:d + 1, :] + _mm(wd, wup_ref[d], mode="x3")
        w_log = -(jnp.maximum(-z, 0.0) + jnp.log1p(jnp.exp(-jnp.abs(z)))) - 0.5
        lw_out[0] = -jnp.exp(w_log)
        a = jax.nn.sigmoid(a0_ref[d:d + 1, :] + _mm(ad, aup_ref[d], mode="x3"))
        kd = k * (1.0 + (a - 1.0) * ka_ref[...])
        b_out[0] = (kk * a).astype(BF16)
        kd_out[0] = kd.astype(BF16)
        bonus_in = bonus_in + r * kd * rk_ref[d:d + 1, :]
    bonus_out[0] = (_mm(bonus_in, seg, mode="lhs2") * v).astype(BF16)


def _rw_prep(p_rw, prm):
    bsz, n_rows, width = p_rw.shape
    tm = min(ROW_TILE, n_rows)
    main, prev, nxt = _halo_specs(tm, width, n_rows)
    full = lambda a: pl.BlockSpec(a.shape, lambda b, i: (0,) * a.ndim)
    consts = [prm["mu_prev"], prm["mu_next"], prm["w0"], prm["w_up"], prm["a0"], prm["a_up"],
              prm["k_k"], prm["k_a"], prm["r_k"], prm["seg"]]
    out_spec = pl.BlockSpec((1, tm, RW_WIDTH), lambda b, i: (b, i, 0))
    shape = lambda dt: jax.ShapeDtypeStruct((bsz, n_rows, RW_WIDTH), dt)
    outs = pl.pallas_call(
        _rw_prep_kernel,
        grid=(bsz, n_rows // tm),
        in_specs=[main, prev, nxt] + [full(a) for a in consts],
        out_specs=[out_spec] * 10,
        out_shape=[shape(BF16)] * 4 + [shape(F32)] * 2 + [shape(BF16)] * 4,
        compiler_params=_cparams("parallel", "arbitrary"),
        name="rwkv_prep",
    )(p_rw, p_rw, p_rw, *consts)
    r, v, kk, bonus, lw0, lw1, b0, b1, kd0, kd1 = outs
    return dict(r=r, v=v, kk=kk, bonus=bonus, lw=(lw0, lw1), b=(b0, b1), kd=(kd0, kd1))


def _rwkv_scan_kernel(rf_ref, vf_ref, kkf_ref, lw0_ref, b0_ref, kd0_ref,
                      rb_ref, vb_ref, kkb_ref, lw1_ref, b1_ref, kd1_ref, s0_ref,
                      of_ref, ob_ref, sfin_ref, st_ref):
    j = pl.program_id(1)

    @pl.when(j == 0)
    def _():
        st_ref[...] = s0_ref[0]

    c = rf_ref.shape[1]
    c2 = 2 * c
    dir_refs = ((rf_ref, vf_ref, kkf_ref, lw0_ref, b0_ref, kd0_ref, of_ref),
                (rb_ref, vb_ref, kkb_ref, lw1_ref, b1_ref, kd1_ref, ob_ref))
    ti = lax.broadcasted_iota(jnp.int32, (c, c), 0)
    si = lax.broadcasted_iota(jnp.int32, (c, c), 1)
    t2 = lax.broadcasted_iota(jnp.int32, (c2, c2), 0)
    s2 = lax.broadcasted_iota(jnp.int32, (c2, c2), 1)
    same = jnp.where((t2 >= c) == (s2 >= c), 1, 0)
    eye2 = jnp.where(t2 == s2, 1.0, 0.0)
    head0 = lax.broadcasted_iota(jnp.int32, (c, RW_PAIR), 1) < RW_HEAD
    masks = []
    for rev in (False, True):
        tri = jnp.where((si >= ti) if rev else (si <= ti), 1.0, 0.0)
        before = jnp.where((s2 > t2) if rev else (s2 < t2), same, 0) == 1
        upto = jnp.where((s2 >= t2) if rev else (s2 <= t2), same, 0) == 1
        masks.append((tri, before, upto))

    def stack(t):
        return jnp.concatenate([jnp.where(head0, t, 0.0), jnp.where(head0, 0.0, t)], axis=0)

    def unstack(t):
        return t[:c] + t[c:]

    mm = functools.partial(_mm, mode="bf16")
    items = [(d, p) for d in range(2) for p in range(N_PAIRS)]
    pair_sl = lambda p: slice(p * RW_PAIR, (p + 1) * RW_PAIR)

    lams = [_mm(masks[d][0], dir_refs[d][3][0], mode="f32") for d in range(2)]

    ops = []
    for d, p in items:
        r_ref, v_ref, kk_ref, lw_ref, b_ref, kd_ref, _ = dir_refs[d]
        sl = pair_sl(p)
        lam = lams[d][:, sl]
        lam_tot = lam[0:1] if d == 1 else lam[c - 1:c]
        e_neg = jnp.exp(-lam)
        e_tail = jnp.exp(lam_tot - lam)
        ld = lambda ref: ref[0, :, sl].astype(F32)
        b, kd = ld(b_ref), ld(kd_ref)
        ops.append(dict(
            at2=stack(-ld(kk_ref) * jnp.exp(lam - lw_ref[0, :, sl])),
            rt2=stack(ld(r_ref) * jnp.exp(lam)),
            bh2=stack(b * e_neg), kh2=stack(kd * e_neg),
            bt2=stack(b * e_tail), kt2=stack(kd * e_tail),
            v2=stack(ld(v_ref)),
            decay_tot=jnp.where(eye2 == 1.0, jnp.exp(lam_tot), 0.0)))

    for (d, p), o in zip(items, ops):
        _, before, upto = masks[d]
        x = mm(jnp.concatenate([o["at2"], o["rt2"]], axis=0),
               jnp.concatenate([o["bh2"], o["kh2"]], axis=0), _NT)
        o["a_ab"] = jnp.where(before, x[:c2, :c2], 0.0)
        o["a_ak"] = jnp.where(before, x[:c2, c2:], 0.0)
        o["a_rb"] = jnp.where(upto, x[c2:, :c2], 0.0)
        o["a_rk"] = jnp.where(upto, x[c2:, c2:], 0.0)

    for o in ops:
        o["t_inv"] = eye2 + o["a_ab"]
        o["pw"] = mm(o["a_ab"], o["a_ab"])
        o["w"] = mm(o["a_ak"], o["v2"])
    for _ in range(int(math.log2(c)) - 2):
        for o in ops:
            y = mm(jnp.concatenate([o["pw"], o["t_inv"]], axis=0), o["pw"])
            o["pw"] = y[:c2]
            o["t_inv"] = o["t_inv"] + y[c2:]
    for o in ops:
        o["t_inv"] = o["t_inv"] + mm(o["t_inv"], o["pw"])

    for o in ops:
        o["z"] = mm(o["t_inv"], jnp.concatenate([o["at2"], o["w"]], axis=1))
    for (d, p), o in zip(items, ops):
        y1 = mm(o["a_rb"], o["z"])
        y2 = mm(o["bt2"].T, o["z"])
        y3 = mm(jnp.concatenate([o["a_rk"], o["kt2"].T], axis=0), o["v2"])
        rp = unstack(o["rt2"] + y1[:, :c2])
        o0 = unstack(y1[:, c2:] + y3[:c2])
        m_bd = o["decay_tot"] + y2[:, :c2]
        n_bd = y2[:, c2:] + y3[c2:]
        rows = slice(d * RW_PAIR, (d + 1) * RW_PAIR)
        sl = pair_sl(p)
        state = st_ref[rows, sl]
        dir_refs[d][6][0, :, sl] = (mm(rp, state) + o0).astype(BF16)
        st_ref[rows, sl] = mm(m_bd, state) + n_bd

    @pl.when(j == pl.num_programs(1) - 1)
    def _():
        sfin_ref[0] = st_ref[...]


def _rwkv_scan(st, s0):
    bsz, n_rows, w = st["r"].shape
    nch = n_rows // CHUNK
    fwd = pl.BlockSpec((1, CHUNK, w), lambda b, j: (b, j, 0))
    bwd = pl.BlockSpec((1, CHUNK, w), lambda b, j: (b, nch - 1 - j, 0))
    sspec = pl.BlockSpec((1, 2 * RW_PAIR, w), lambda b, j: (b, 0, 0))
    row_shape = jax.ShapeDtypeStruct((bsz, n_rows, w), BF16)
    return pl.pallas_call(
        _rwkv_scan_kernel,
        grid=(bsz, nch),
        in_specs=[fwd] * 6 + [bwd] * 6 + [sspec],
        out_specs=[fwd, bwd, sspec],
        out_shape=[row_shape, row_shape, jax.ShapeDtypeStruct((bsz, 2 * RW_PAIR, w), F32)],
        scratch_shapes=[pltpu.VMEM((2 * RW_PAIR, w), F32)],
        compiler_params=_cparams("parallel", "arbitrary"),
        name="rwkv_scan",
    )(st["r"], st["v"], st["kk"], st["lw"][0], st["b"][0], st["kd"][0],
      st["r"], st["v"], st["kk"], st["lw"][1], st["b"][1], st["kd"][1], s0)


def _even_out_kernel(x_ref, gate_ref, om_ref, of_ref, ob_ref, bonus_ref, g_ref, lnw_ref, lnb_ref,
                     seg_ref, w_ref, o_ref):
    seg = seg_ref[...]
    o = of_ref[0].astype(F32) + ob_ref[0].astype(F32)
    mu = _mm(o, seg, mode="lhs2") * (1.0 / RW_HEAD)
    dlt = o - mu
    var = _mm(dlt * dlt, seg, mode="lhs2") * (1.0 / RW_HEAD)
    o_rw = (dlt * lax.rsqrt(var + RW_GN_EPS) * lnw_ref[...] + lnb_ref[...]
            + bonus_ref[0].astype(F32))
    g = g_ref[0].astype(F32)
    z_m = (om_ref[0].astype(F32) * _silu(g[:, :MLA_WIDTH])).astype(BF16)
    z_r = (o_rw * _silu(g[:, MLA_WIDTH:])).astype(BF16)
    y = (jnp.dot(z_m, w_ref[:MLA_WIDTH, :], preferred_element_type=F32)
         + jnp.dot(z_r, w_ref[MLA_WIDTH:, :], preferred_element_type=F32))
    o_ref[0] = x_ref[0] + gate_ref[0] * y


def _even_out(x, gate, o_mla, o_f, o_b, bonus, g, ln_w, ln_b, seg, w_out):
    bsz, n_rows, d = x.shape
    tm = min(ROW_TILE, n_rows)
    rows = lambda width: pl.BlockSpec((1, tm, width), lambda b, i: (b, i, 0))
    full = lambda a: pl.BlockSpec(a.shape, lambda b, i: (0,) * a.ndim)
    return pl.pallas_call(
        _even_out_kernel,
        grid=(bsz, n_rows // tm),
        in_specs=[rows(d), pl.BlockSpec((1, 1, d), lambda b, i: (b, 0, 0)),
                  rows(MLA_WIDTH), rows(RW_WIDTH), rows(RW_WIDTH), rows(RW_WIDTH), rows(d),
                  full(ln_w), full(ln_b), full(seg), full(w_out)],
        out_specs=rows(d),
        out_shape=jax.ShapeDtypeStruct(x.shape, F32),
        compiler_params=_cparams("parallel", "arbitrary"),
        name="even_out",
    )(x, gate, o_mla, o_f, o_b, bonus, g, ln_w, ln_b, seg, w_out)


def _conv3_kernel(u_ref, prev_ref, next_ref, w_ref, b_ref, v_ref, x0_ref):
    u = u_ref[0].astype(F32)
    prev, nxt = _shifted_rows(u, prev_ref, next_ref)
    cv = prev * w_ref[0:1, :] + u * w_ref[1:2, :] + nxt * w_ref[2:3, :] + b_ref[...]
    hw = HY_WIDTH
    x0_ref[0] = cv[:, :hw].astype(BF16)
    v_ref[0] = (cv[:, 2 * hw:] * cv[:, hw:2 * hw]).astype(BF16)


def _conv3(u, conv_w, conv_b):
    bsz, n_rows, width = u.shape
    tm = min(ROW_TILE, n_rows)
    main, prev, nxt = _halo_specs(tm, width, n_rows)
    out_spec = pl.BlockSpec((1, tm, HY_WIDTH), lambda b, i: (b, i, 0))
    out_shape = jax.ShapeDtypeStruct((bsz, n_rows, HY_WIDTH), BF16)
    return pl.pallas_call(
        _conv3_kernel,
        grid=(bsz, n_rows // tm),
        in_specs=[main, prev, nxt,
                  pl.BlockSpec(conv_w.shape, lambda b, i: (0, 0)),
                  pl.BlockSpec((1, width), lambda b, i: (0, 0))],
        out_specs=[out_spec, out_spec],
        out_shape=[out_shape, out_shape],
        compiler_params=_cparams("parallel", "arbitrary"),
        name="hyena_conv3",
    )(u, u, u, conv_w, conv_b.reshape(1, width))


def _hy_hidden_kernel(w1_ref, b1_ref, w2_ref, b2_ref, freq_ref, h_ref):
    n_pos = h_ref.shape[0]
    pos = lax.broadcasted_iota(jnp.int32, (n_pos, 1), 0).astype(F32)
    lane = lax.broadcasted_iota(jnp.int32, (1, LANES), 1)
    band_idx = jnp.where(lane <= HY_BANDS, lane - 1, lane - 1 - HY_BANDS).astype(F32)
    band = 1e-4 + band_idx * ((HY_BANDS - 1 - 1e-4) / (HY_BANDS - 1))
    ang = pos * (2.0 * math.pi / n_pos) * band
    z = jnp.where(lane == 0, pos / (n_pos - 1),
                  jnp.where(lane <= HY_BANDS, jnp.cos(ang),
                            jnp.where(lane <= 2 * HY_BANDS, -jnp.sin(ang), 0.0)))
    freq = freq_ref[...]
    hdn = jnp.sin(freq * (_mm(z, w1_ref[...], mode="f32") + b1_ref[...]))
    for j in range(HY_INNER):
        hdn = jnp.sin(freq * (_mm(hdn, w2_ref[j], mode="f32") + b2_ref[j]))
    h_ref[...] = hdn


def _hy_filter_kernel(h_ref, w0_ref, w1_ref, dl_ref, fs_ref, fd_ref):
    n_pos = h_ref.shape[0]
    pos = lax.broadcasted_iota(jnp.int32, (n_pos, 1), 0)
    t = pos.astype(F32) / (n_pos - 1)
    dec = jnp.exp(-t * dl_ref[...])
    hdn = h_ref[...]
    f_fwd = _mm(hdn, w0_ref[...], mode="f32") * dec
    f_bwd = jnp.where(pos == 0, 0.0, _mm(hdn, w1_ref[...], mode="f32") * dec)
    inv = 1.0 / (jnp.sum(jnp.abs(f_fwd), axis=0, keepdims=True)
                 + jnp.sum(jnp.abs(f_bwd), axis=0, keepdims=True))
    fs_ref[...] = (f_fwd + f_bwd) * inv
    fd_ref[...] = (f_bwd - f_fwd) * inv


def _hyena_filters(n_pos, f_w1, f_b1, f_w2, f_b2, f_wout, freq, deltas):
    w1 = jnp.zeros((LANES, HY_ORDER), F32).at[:HY_EMB].set(f_w1)
    hdn = pl.pallas_call(
        _hy_hidden_kernel,
        out_shape=jax.ShapeDtypeStruct((n_pos, HY_ORDER), F32),
        name="hyena_filter_hidden",
    )(w1, f_b1.reshape(1, HY_ORDER), f_w2, f_b2.reshape(HY_INNER, 1, HY_ORDER),
      freq.reshape(1, HY_ORDER))
    tn = 256
    cspec = pl.BlockSpec((HY_ORDER, tn), lambda j: (0, j))
    ospec = pl.BlockSpec((n_pos, tn), lambda j: (0, j))
    oshape = jax.ShapeDtypeStruct((n_pos, HY_WIDTH), F32)
    return pl.pallas_call(
        _hy_filter_kernel,
        grid=(HY_WIDTH // tn,),
        in_specs=[pl.BlockSpec((n_pos, HY_ORDER), lambda j: (0, 0)), cspec, cspec,
                  pl.BlockSpec((1, tn), lambda j: (0, j))],
        out_specs=[ospec, ospec],
        out_shape=[oshape, oshape],
        compiler_params=_cparams("arbitrary"),
        name="hyena_filter",
    )(hdn, f_wout[:, :HY_WIDTH], f_wout[:, HY_WIDTH:], deltas)


def _dft_tables_kernel(c_ref, s_ref):
    tr, n_half = c_ref.shape
    n = 2 * n_half
    k = lax.broadcasted_iota(jnp.int32, (tr, n_half), 0) + pl.program_id(0) * tr
    s = lax.broadcasted_iota(jnp.int32, (tr, n_half), 1)
    ph = (k * s) & (n - 1)
    ph = jnp.where(ph >= n_half, ph - n, ph)
    ang = ph.astype(F32) * (2.0 * math.pi / n)
    c_ref[...] = jnp.cos(ang).astype(BF16)
    s_ref[...] = jnp.sin(ang).astype(BF16)


def _dft_tables(n_half):
    tr = min(256, n_half)
    spec = pl.BlockSpec((tr, n_half), lambda i: (i, 0))
    shape = jax.ShapeDtypeStruct((n_half, n_half), BF16)
    return pl.pallas_call(
        _dft_tables_kernel,
        grid=(n_half // tr,),
        out_specs=[spec, spec],
        out_shape=[shape, shape],
        compiler_params=_cparams("arbitrary"),
        name="dft_tables",
    )()


def _alt_sign(n_pos):
    pos = lax.broadcasted_iota(jnp.int32, (n_pos, 1), 0)
    return jnp.where((pos & 1) == 0, 1.0, -1.0), pos


def _spectrum_kernel(fs_ref, fd_ref, c_ref, s_ref, hre_ref, him_ref, hny_ref):
    n_pos = fs_ref.shape[0]
    alt, pos = _alt_sign(n_pos)
    fs, fd = fs_ref[...], fd_ref[...]

    def mm2(tab, f):
        fh = f.astype(BF16)
        fl = (f - fh.astype(F32)).astype(BF16)
        return (jnp.dot(tab, fh, preferred_element_type=F32)
                + jnp.dot(tab, fl, preferred_element_type=F32))

    scale = 1.0 / n_pos
    hre_ref[...] = mm2(c_ref[...], fs) * jnp.where(pos == 0, 0.5 * scale, scale)
    him_ref[...] = mm2(s_ref[...], fd) * scale
    hny_ref[...] = jnp.sum(fs * alt, axis=0, keepdims=True) * (0.5 * scale)


def _table_spec(tab, nd):
    return pl.BlockSpec(tab.shape, (lambda j: (0, 0)) if nd == 1 else (lambda b, j: (0, 0)),
                        pipeline_mode=pl.Buffered(1))


def _spectrum(fs, fd, ctab, stab):
    n_pos, width = fs.shape
    tc = DFT_TC
    col = pl.BlockSpec((n_pos, tc), lambda j: (0, j))
    return pl.pallas_call(
        _spectrum_kernel,
        grid=(width // tc,),
        in_specs=[col, col, _table_spec(ctab, 1), _table_spec(stab, 1)],
        out_specs=[col, col, pl.BlockSpec((1, tc), lambda j: (0, j))],
        out_shape=[jax.ShapeDtypeStruct((n_pos, width), F32)] * 2
        + [jax.ShapeDtypeStruct((1, width), F32)],
        compiler_params=_cparams("arbitrary"),
        name="hyena_spectrum",
    )(fs, fd, ctab, stab)


def _dft_conv_kernel(v_ref, hre_ref, him_ref, hny_ref, c_ref, s_ref, y_ref, yre_scr, yim_scr):
    n_pos = v_ref.shape[1]
    tm = min(DFT_TM, n_pos)
    alt, _ = _alt_sign(n_pos)
    vb = v_ref[0]
    nyq = jnp.sum(vb.astype(F32) * alt, axis=0, keepdims=True) * hny_ref[...]
    for r0 in range(0, n_pos, tm):
        rows = slice(r0, r0 + tm)
        v_re = jnp.dot(c_ref[rows, :], vb, preferred_element_type=F32)
        v_s = jnp.dot(s_ref[rows, :], vb, preferred_element_type=F32)
        hre, him = hre_ref[rows, :], him_ref[rows, :]
        yre_scr[rows, :] = (v_re * hre + v_s * him).astype(BF16)
        yim_scr[rows, :] = (v_s * hre - v_re * him).astype(BF16)
    y_re, y_ims = yre_scr[...], yim_scr[...]
    for r0 in range(0, n_pos, tm):
        rows = slice(r0, r0 + tm)
        y_ref[0, rows, :] = (jnp.dot(c_ref[rows, :], y_re, preferred_element_type=F32)
                             + jnp.dot(s_ref[rows, :], y_ims, preferred_element_type=F32)
                             + alt[rows] * nyq).astype(BF16)


def _dft_conv(v, hre, him, hny, ctab, stab):
    bsz, n_pos, width = v.shape
    tc = DFT_TC
    col = pl.BlockSpec((n_pos, tc), lambda b, j: (0, j))
    vspec = pl.BlockSpec((1, n_pos, tc), lambda b, j: (b, 0, j))
    return pl.pallas_call(
        _dft_conv_kernel,
        grid=(bsz, width // tc),
        in_specs=[vspec, col, col, pl.BlockSpec((1, tc), lambda b, j: (0, j)),
                  _table_spec(ctab, 2), _table_spec(stab, 2)],
        out_specs=vspec,
        out_shape=jax.ShapeDtypeStruct(v.shape, BF16),
        scratch_shapes=[pltpu.VMEM((n_pos, tc), BF16)] * 2,
        compiler_params=_cparams("parallel", "arbitrary"),
        name="hyena_dft_conv",
    )(v, hre, him, hny, ctab, stab)


def _hy_out_kernel(x_ref, gate_ref, y_ref, v_ref, x0_ref, g_ref, bias_ref, w_ref, o_ref):
    f32 = lambda ref: ref[0].astype(F32)
    z = (f32(y_ref) + f32(v_ref) * bias_ref[...]) * f32(x0_ref) * _silu(f32(g_ref))
    o_ref[0] = x_ref[0] + gate_ref[0] * jnp.dot(z.astype(BF16), w_ref[...],
                                                preferred_element_type=F32)


def _hy_out(x, gate, y, v, x0, g, bias_d, w_out):
    bsz, n_rows, d = x.shape
    tm = min(ROW_TILE, n_rows)
    rows = pl.BlockSpec((1, tm, d), lambda b, i: (b, i, 0))
    return pl.pallas_call(
        _hy_out_kernel,
        grid=(bsz, n_rows // tm),
        in_specs=[rows, pl.BlockSpec((1, 1, d), lambda b, i: (b, 0, 0)), rows, rows, rows, rows,
                  pl.BlockSpec((1, d), lambda b, i: (0, 0)),
                  pl.BlockSpec(w_out.shape, lambda b, i: (0, 0))],
        out_specs=rows,
        out_shape=jax.ShapeDtypeStruct(x.shape, F32),
        compiler_params=_cparams("parallel", "arbitrary"),
        name="hyena_out",
    )(x, gate, y, v, x0, g, bias_d.reshape(1, d), w_out)


def _rope_perm():
    pairs = np.arange(MLA_ROPE // 2)
    return np.concatenate([np.arange(MLA_NOPE), MLA_NOPE + 2 * pairs, MLA_NOPE + 2 * pairs + 1])


def _rope_tables(n_pos):
    rows = n_pos // GRID_W
    row = jnp.repeat(jnp.arange(rows, dtype=F32), GRID_W)
    col = jnp.tile(jnp.arange(GRID_W, dtype=F32), rows)
    n_freq = MLA_ROPE // 4
    inv = ROPE_BASE ** (-jnp.arange(n_freq, dtype=F32) / n_freq)
    ang = jnp.concatenate([row[:, None] * inv, col[:, None] * inv], axis=-1)
    cos, sin = jnp.cos(ang), jnp.sin(ang)
    half = MLA_ROPE // 2
    ones = jnp.ones((n_pos, MLA_NOPE), F32)
    zeros = jnp.zeros((n_pos, MLA_NOPE), F32)
    pad1 = jnp.ones((n_pos, HEAD_PAD - MLA_QK), F32)
    pad0 = jnp.zeros((n_pos, HEAD_PAD - MLA_QK), F32)
    z16 = jnp.zeros((n_pos, half), F32)
    cos_f = jnp.concatenate([ones, cos, cos, pad1], axis=-1)
    sin_a = jnp.concatenate([zeros, z16, sin, pad0], axis=-1)
    sin_b = jnp.concatenate([zeros, -sin, z16, pad0], axis=-1)
    return cos_f, sin_a, sin_b


def _pad_heads(w, width):
    k, h, _ = w.shape
    return jnp.zeros((k, h, HEAD_PAD), w.dtype).at[:, :, :width].set(w).reshape(k, h * HEAD_PAD)


def _even_weights(e, ev_w_in, ev_w_out, mla_q_a_norm, mla_w_uq, mla_kv_a_norm, mla_w_ukv,
                  mla_q_norm, mla_k_norm, rwkv_mu_prev, rwkv_mu_next, rwkv_w0, rwkv_w_up, rwkv_a0,
                  rwkv_a_up, rwkv_k_k, rwkv_k_a, rwkv_r_k, rwkv_ln_w, rwkv_ln_b):
    perm = _rope_perm()
    w_in = ev_w_in[e]
    d = w_in.shape[0]
    o_dkv = EV_DQ
    o_rw = EV_DQ + EV_DKV
    o_g = o_rw + RW_SHIFT
    w_dkv = jnp.zeros((d, 2 * LANES), F32)
    w_dkv = w_dkv.at[:, :MLA_KV_LORA].set(w_in[:, o_dkv:o_dkv + MLA_KV_LORA])
    rope_cols = o_dkv + MLA_KV_LORA + (perm[MLA_NOPE:] - MLA_NOPE)
    w_dkv = w_dkv.at[:, LANES + MLA_NOPE:LANES + MLA_QK].set(w_in[:, rope_cols])
    proj = [w_in[:, :EV_DQ].astype(BF16), w_dkv.astype(BF16),
            w_in[:, o_rw:o_g].astype(BF16), w_in[:, o_g:].astype(BF16)]
    w_ukv = mla_w_ukv[e]
    pad_gain = lambda g: jnp.zeros((1, HEAD_PAD), F32).at[0, :MLA_QK].set(g[perm])
    seg_id = np.arange(RW_WIDTH) // RW_HEAD
    zero_up = lambda up, dd: jnp.zeros((2 * RW_LORA_W, RW_WIDTH), F32).at[
        dd * RW_LORA_W:(dd + 1) * RW_LORA_W].set(up[dd])
    rw = dict(
        mu_prev=rwkv_mu_prev[e].reshape(1, RW_SHIFT), mu_next=rwkv_mu_next[e].reshape(1, RW_SHIFT),
        w0=rwkv_w0[e], a0=rwkv_a0[e],
        w_up=jnp.stack([zero_up(rwkv_w_up[e], 0), zero_up(rwkv_w_up[e], 1)]),
        a_up=jnp.stack([zero_up(rwkv_a_up[e], 0), zero_up(rwkv_a_up[e], 1)]),
        k_k=rwkv_k_k[e].reshape(1, RW_WIDTH), k_a=rwkv_k_a[e].reshape(1, RW_WIDTH),
        r_k=rwkv_r_k[e].reshape(2, RW_WIDTH),
        seg=jnp.asarray((seg_id[:, None] == seg_id[None, :]).astype(np.float32)),
    )
    head_par = np.arange(MLA_HEADS) % 2
    v_lane = (np.arange(HEAD_PAD)[None, :] >= MLA_V) == (head_par[:, None] == 1)
    v_ones = (~v_lane).astype(np.float32)
    w_vh = w_ukv[:, :, MLA_NOPE:].reshape(MLA_KV_LORA, MLA_HEADS // 2, 2, MLA_V)
    zero_v = jnp.zeros_like(w_vh[:, :, 0])
    w_v = jnp.stack([w_vh[:, :, 0], zero_v, zero_v, w_vh[:, :, 1]], axis=2)
    return dict(
        proj=proj,
        q_a_norm=mla_q_a_norm[e], kv_a_norm=mla_kv_a_norm[e],
        w_uq=_pad_heads(mla_w_uq[e][:, :, perm], MLA_QK).astype(BF16),
        w_k=_pad_heads(w_ukv[:, :, :MLA_NOPE], MLA_NOPE).astype(BF16),
        w_v=w_v.reshape(MLA_KV_LORA, MLA_HEADS * HEAD_PAD).astype(BF16),
        v_ones=jnp.asarray(v_ones.reshape(1, MLA_HEADS * HEAD_PAD)),
        q_gain=pad_gain(mla_q_norm[e]) * (MLA_QK ** -0.5 * math.log2(math.e)),
        k_gain=pad_gain(mla_k_norm[e]),
        rw=rw,
        ln_w=rwkv_ln_w[e].reshape(1, RW_WIDTH), ln_b=rwkv_ln_b[e].reshape(1, RW_WIDTH),
        w_out=ev_w_out[e].astype(BF16),
    )


def _rwkv_branch(st_c, st, want_ctx):
    bsz = st["r"].shape[0]
    zero_state = jnp.zeros((bsz, 2 * RW_PAIR, RW_WIDTH), F32)
    of_c, ob_c, s_c = _rwkv_scan(st_c, zero_state)
    o_f, o_b, _ = _rwkv_scan(st, s_c)
    return [o_f, o_b], ([of_c, ob_c] if want_ctx else None)


def _even_layer(x, xc, mod, mod_c, g_norm, wts, rope_tabs, ctx_out):
    shift, scale1p, gate = mod
    shift_c, scale1p_c, gate_c = mod_c
    q, k, v, p_rw, g = _even_in(x, g_norm, scale1p, shift, wts, rope_tabs)
    qc, kc, vc, pc_rw, gc = _even_in(xc, g_norm, scale1p_c, shift_c, wts, None)
    o_mla = _attention(q, [k, kc], [v, vc])
    st_c = _rw_prep(pc_rw, wts["rw"])
    st = _rw_prep(p_rw, wts["rw"])
    outs, outs_c = _rwkv_branch(st_c, st, ctx_out)
    seg = wts["rw"]["seg"]
    x_new = _even_out(x, gate, o_mla, outs[0], outs[1], st["bonus"], g, wts["ln_w"], wts["ln_b"],
                      seg, wts["w_out"])
    if not ctx_out:
        return x_new, None
    oc_mla = _attention(qc, [kc], [vc])
    xc_new = _even_out(xc, gate_c, oc_mla, outs_c[0], outs_c[1], st_c["bonus"], gc, wts["ln_w"],
                       wts["ln_b"], seg, wts["w_out"])
    return x_new, xc_new


def _hyena_layer(x, mod, g_norm, wts, tables):
    shift, scale1p, gate = mod
    n_pos = x.shape[1]
    ctab, stab = tables
    u, g = _norm_proj(x, g_norm, scale1p, shift, wts["proj"])
    v, x0 = _conv3(u, wts["conv_w"], wts["conv_b"])
    fs, fd = _hyena_filters(n_pos, *wts["filt"])
    hre, him, hny = _spectrum(fs, fd, ctab, stab)
    y = _dft_conv(v, hre, him, hny, ctab, stab)
    return _hy_out(x, gate, y, v, x0, g, wts["bias_d"], wts["w_out"])


def kernel(x, c, ctx, c_ctx, mod_w, mod_b, norm_g, ev_w_in, ev_w_out, mla_q_a_norm, mla_w_uq, mla_kv_a_norm, mla_w_ukv, mla_q_norm, mla_k_norm, rwkv_mu_prev, rwkv_mu_next, rwkv_w0, rwkv_w_up, rwkv_a0, rwkv_a_up, rwkv_k_k, rwkv_k_a, rwkv_r_k, rwkv_ln_w, rwkv_ln_b, od_w_in, od_w_out, hy_conv_w, hy_conv_b, hy_bias_d, hy_f_w1, hy_f_b1, hy_f_w2, hy_f_b2, hy_f_wout, hy_freq):
    bsz, n_lat, d = x.shape
    n_ctx = ctx.shape[1]
    assert n_lat % max(CHUNK, GRID_W) == 0 and n_ctx % CHUNK == 0 and d == D_MODEL
    assert CHUNK == RW_HEAD and 2 * CHUNK == RW_PAIR

    n_rows = -(-(bsz + 1) // 16) * 16
    cvec = jnp.zeros((n_rows, d), F32).at[:bsz].set(c).at[bsz].set(c_ctx)
    mods = _modulation(cvec, mod_w, mod_b)

    def split_mod(i, lo, hi, reps):
        m = mods[i, lo:hi]
        m = jnp.broadcast_to(m, (reps, 3 * d)) if hi - lo == 1 else m
        m = m[:, None, :]
        return m[..., :d], 1.0 + m[..., d:2 * d], m[..., 2 * d:]

    rope_tabs = _rope_tables(n_lat)
    deltas = jnp.abs(jnp.linspace(math.log(HY_TARGET) / HY_FAST_DECAY,
                                  math.log(HY_TARGET) / HY_SLOW_DECAY, HY_WIDTH,
                                  dtype=F32)).reshape(1, HY_WIDTH)
    tables = {n_lat: _dft_tables(n_lat)}

    xc = ctx
    for i in range(DEPTH):
        ctx_needed_later = any(j > i and j % 2 == 0 for j in range(DEPTH))
        mod = split_mod(i, 0, bsz, bsz)
        mod_c = split_mod(i, bsz, bsz + 1, bsz)
        if i % 2 == 0:
            wts = _even_weights(i // 2, ev_w_in, ev_w_out, mla_q_a_norm, mla_w_uq, mla_kv_a_norm,
                                mla_w_ukv, mla_q_norm, mla_k_norm, rwkv_mu_prev, rwkv_mu_next,
                                rwkv_w0, rwkv_w_up, rwkv_a0, rwkv_a_up, rwkv_k_k, rwkv_k_a,
                                rwkv_r_k, rwkv_ln_w, rwkv_ln_b)
            x, xc_new = _even_layer(x, xc, mod, mod_c, norm_g[i], wts, rope_tabs, ctx_needed_later)
            xc = xc_new if ctx_needed_later else xc
        else:
            o = i // 2
            w_in = od_w_in[o]
            wts = dict(
                proj=[w_in[:, :3 * HY_WIDTH].astype(BF16), w_in[:, 3 * HY_WIDTH:].astype(BF16)],
                conv_w=hy_conv_w[o], conv_b=hy_conv_b[o], bias_d=hy_bias_d[o],
                filt=(hy_f_w1[o], hy_f_b1[o], hy_f_w2[o], hy_f_b2[o], hy_f_wout[o], hy_freq[o],
                      deltas),
                w_out=od_w_out[o].astype(BF16),
            )
            if ctx_needed_later:
                if n_ctx not in tables:
                    tables[n_ctx] = _dft_tables(n_ctx)
                xc = _hyena_layer(xc, mod_c, norm_g[i], wts, tables[n_ctx])
            x = _hyena_layer(x, mod, norm_g[i], wts, tables[n_lat])
    return x
```

```python
import functools
import math

import numpy as np
import jax
import jax.numpy as jnp
from jax import lax
from jax.experimental import pallas as pl
from jax.experimental.pallas import tpu as pltpu

F32 = jnp.float32
BF16 = jnp.bfloat16
HIGHEST = lax.Precision.HIGHEST

D_MODEL = 1024
DEPTH = 4
GRID_W = 64
NORM_EPS = 1e-6
MLA_HEADS = 8
MLA_NOPE = 64
MLA_ROPE = 32
MLA_QK = MLA_NOPE + MLA_ROPE
MLA_V = 64
MLA_Q_LORA = 256
MLA_KV_LORA = 128
MLA_WIDTH = MLA_HEADS * MLA_V
ROPE_BASE = 10000.0
RW_HEADS = 8
RW_HEAD = 64
RW_WIDTH = RW_HEADS * RW_HEAD
RW_LORA_W = 64
RW_LORA_A = 64
RW_SHIFT = 3 * RW_WIDTH + 2 * RW_LORA_W + 2 * RW_LORA_A
RW_GN_EPS = 64e-5
EV_DQ = MLA_Q_LORA
EV_DKV = MLA_KV_LORA + MLA_ROPE
HY_WIDTH = D_MODEL
HY_ORDER = 64
HY_BANDS = 16
HY_EMB = 1 + 2 * HY_BANDS
HY_INNER = 2
HY_FAST_DECAY = 0.3
HY_SLOW_DECAY = 1.5
HY_TARGET = 1e-2

LANES = 128
HEAD_PAD = 128
RW_PAIR = 2 * RW_HEAD
N_PAIRS = RW_WIDTH // RW_PAIR
CHUNK = 64
ROW_TILE = 256
ATTN_TQ = 256
ATTN_TK = 256
DFT_TC = 256
DFT_TM = 256
HALO_ROWS = 16
SUB_ROWS = 128
VMEM_LIMIT = 56 * 1024 * 1024

_NN = (((1,), (0,)), ((), ()))
_NT = (((1,), (1,)), ((), ()))


def _mm(a, b, dn=_NN, mode="bf16"):
    if mode == "f32":
        return lax.dot_general(a, b, dn, precision=HIGHEST, preferred_element_type=F32)
    dg = functools.partial(lax.dot_general, dimension_numbers=dn, preferred_element_type=F32)
    ah = a.astype(BF16)
    bh = b.astype(BF16)
    if mode == "bf16":
        return dg(ah, bh)
    al = (a - ah.astype(F32)).astype(BF16)
    if mode == "lhs2":
        return dg(ah, bh) + dg(al, bh)
    bl = (b - bh.astype(F32)).astype(BF16)
    return dg(ah, bh) + (dg(ah, bl) + dg(al, bh))


def _cparams(*sem):
    return pltpu.CompilerParams(dimension_semantics=sem, vmem_limit_bytes=VMEM_LIMIT)


def _silu(t):
    return t * jax.nn.sigmoid(t)


def _shifted_rows(pb, prev_ref, next_ref, on_mxu):
    tm = pb.shape[0]
    i = pl.program_id(1)
    last = pl.num_programs(1) - 1
    prev_row = jnp.where(i > 0, prev_ref[0, HALO_ROWS - 1:HALO_ROWS, :].astype(F32), 0.0)
    next_row = jnp.where(i < last, next_ref[0, 0:1, :].astype(F32), 0.0)
    if on_mxu:
        out_row = lax.broadcasted_iota(jnp.int32, (2 * tm, tm), 0)
        src_row = lax.broadcasted_iota(jnp.int32, (2 * tm, tm), 1)
        want = jnp.where(out_row < tm, out_row - 1, out_row - tm + 1)
        both = jnp.dot(jnp.where(src_row == want, 1.0, 0.0).astype(BF16), pb,
                       preferred_element_type=F32)
        down, up = both[:tm], both[tm:]
    else:
        p = pb.astype(F32)
        down, up = pltpu.roll(p, 1, axis=0), pltpu.roll(p, tm - 1, axis=0)
    sub = lax.broadcasted_iota(jnp.int32, (8, 1), 0)
    prev = jnp.concatenate([jnp.where(sub == 0, prev_row, down[:8]), down[8:]], axis=0)
    nxt = jnp.concatenate([up[:tm - 8], jnp.where(sub == 7, next_row, up[tm - 8:])], axis=0)
    return prev, nxt


def _halo_specs(tm, width, n_rows):
    th = tm // HALO_ROWS
    last_h = n_rows // HALO_ROWS - 1
    main = pl.BlockSpec((1, tm, width), lambda b, i: (b, i, 0))
    prev = pl.BlockSpec((1, HALO_ROWS, width), lambda b, i: (b, jnp.maximum(i * th - 1, 0), 0))
    nxt = pl.BlockSpec((1, HALO_ROWS, width),
                       lambda b, i: (b, jnp.minimum((i + 1) * th, last_h), 0))
    return main, prev, nxt


def _mod_kernel(c_ref, w_ref, b_ref, o_ref):
    o_ref[0] = _mm(_silu(c_ref[...]), w_ref[0], mode="x3") + b_ref[0]


def _modulation(cvec, mod_w, mod_b):
    rows, d = cvec.shape
    n = mod_w.shape[-1]
    tn = 1024
    return pl.pallas_call(
        _mod_kernel,
        grid=(DEPTH, n // tn),
        in_specs=[pl.BlockSpec((rows, d), lambda i, j: (0, 0)),
                  pl.BlockSpec((1, d, tn), lambda i, j: (i, 0, j)),
                  pl.BlockSpec((1, 1, tn), lambda i, j: (i, 0, j))],
        out_specs=pl.BlockSpec((1, rows, tn), lambda i, j: (i, 0, j)),
        out_shape=jax.ShapeDtypeStruct((DEPTH, rows, n), F32),
        compiler_params=_cparams("arbitrary", "arbitrary"),
        name="modulation",
    )(cvec, mod_w, mod_b.reshape(DEPTH, 1, n))


def _norm_proj_kernel(nw, x_ref, g_ref, sc_ref, sh_ref, *refs):
    x = x_ref[0]
    h = x * lax.rsqrt(jnp.mean(x * x, axis=-1, keepdims=True) + NORM_EPS) * g_ref[...]
    hb = (h * sc_ref[0] + sh_ref[0]).astype(BF16)
    for w_ref, o_ref in zip(refs[:nw], refs[nw:]):
        o_ref[0] = jnp.dot(hb, w_ref[...], preferred_element_type=F32).astype(BF16)


def _norm_proj(x, g, scale1p, shift, weights):
    bsz, n_rows, d = x.shape
    tm = min(ROW_TILE, n_rows)
    vec = pl.BlockSpec((1, 1, d), lambda b, i: (b, 0, 0))
    in_specs = [pl.BlockSpec((1, tm, d), lambda b, i: (b, i, 0)),
                pl.BlockSpec((1, d), lambda b, i: (0, 0)), vec, vec]
    in_specs += [pl.BlockSpec(w.shape, lambda b, i: (0, 0)) for w in weights]
    return pl.pallas_call(
        functools.partial(_norm_proj_kernel, len(weights)),
        grid=(bsz, n_rows // tm),
        in_specs=in_specs,
        out_specs=[pl.BlockSpec((1, tm, w.shape[1]), lambda b, i: (b, i, 0)) for w in weights],
        out_shape=[jax.ShapeDtypeStruct((bsz, n_rows, w.shape[1]), BF16) for w in weights],
        compiler_params=_cparams("parallel", "arbitrary"),
        name="norm_proj",
    )(x, g.reshape(1, d), scale1p, shift, *weights)


def _head_norm_rope(t, gain, tabs):
    ms = jnp.sum(t * t, axis=-1, keepdims=True) * (1.0 / MLA_QK)
    t = t * lax.rsqrt(ms + NORM_EPS) * gain
    if tabs is not None:
        cos_f, sin_a, sin_b = tabs
        t = (t * cos_f + pltpu.roll(t, MLA_ROPE // 2, axis=1) * sin_a
             + pltpu.roll(t, HEAD_PAD - MLA_ROPE // 2, axis=1) * sin_b)
    return t


def _even_in_kernel(rope, x_ref, g_ref, sc_ref, sh_ref, wdq_ref, wdkv_ref, wrw_ref, wg_ref,
                    qan_ref, wuq_ref, qgn_ref, kan_ref, wk_ref, wv_ref, vone_ref, kgn_ref, *refs):
    q_ref, k_ref, v_ref, prw_ref, go_ref = refs[-5:]
    q_gain, k_gain = qgn_ref[...], kgn_ref[...]

    def rms(t, gain_ref):
        return t * lax.rsqrt(jnp.mean(t * t, axis=-1, keepdims=True) + NORM_EPS) * gain_ref[...]

    tm = x_ref.shape[1]
    sub = min(SUB_ROWS, tm)
    for r0 in range(0, tm, sub):
        rows = slice(r0, r0 + sub)
        tabs = tuple(r[rows, :] for r in refs[:3]) if rope else None
        hb = (rms(x_ref[0, rows, :], g_ref) * sc_ref[0] + sh_ref[0]).astype(BF16)
        prw_ref[0, rows, :] = jnp.dot(hb, wrw_ref[...], preferred_element_type=F32).astype(BF16)
        go_ref[0, rows, :] = jnp.dot(hb, wg_ref[...], preferred_element_type=F32).astype(BF16)
        p_dq = jnp.dot(hb, wdq_ref[...], preferred_element_type=F32)
        q = jnp.dot(rms(p_dq, qan_ref).astype(BF16), wuq_ref[...], preferred_element_type=F32)
        p_dkv = jnp.dot(hb, wdkv_ref[...], preferred_element_type=F32)
        k_rope = p_dkv[:, MLA_KV_LORA:]
        ab = rms(p_dkv[:, :MLA_KV_LORA], kan_ref).astype(BF16)
        k_nope = jnp.dot(ab, wk_ref[...], preferred_element_type=F32)
        v_ref[0, rows, :] = (jnp.dot(ab, wv_ref[...], preferred_element_type=F32)
                             + vone_ref[...]).astype(BF16)
        for hd in range(MLA_HEADS):
            sl = slice(hd * HEAD_PAD, (hd + 1) * HEAD_PAD)
            q_ref[0, rows, sl] = _head_norm_rope(q[:, sl], q_gain, tabs).astype(BF16)
            k_ref[0, rows, sl] = _head_norm_rope(k_nope[:, sl] + k_rope, k_gain,
                                                 tabs).astype(BF16)


def _even_in(x, g, scale1p, shift, wts, tabs):
    bsz, n_rows, d = x.shape
    tm = min(ROW_TILE, n_rows)
    rope = tabs is not None
    kw = MLA_HEADS * HEAD_PAD
    vec = pl.BlockSpec((1, 1, d), lambda b, i: (b, 0, 0))
    full = lambda a: pl.BlockSpec(a.shape, lambda b, i: (0,) * a.ndim)
    consts = list(wts["proj"]) + [wts["q_a_norm"].reshape(1, MLA_Q_LORA), wts["w_uq"], wts["q_gain"],
                                  wts["kv_a_norm"].reshape(1, MLA_KV_LORA), wts["w_k"], wts["w_v"],
                                  wts["v_ones"], wts["k_gain"]]
    in_specs = [pl.BlockSpec((1, tm, d), lambda b, i: (b, i, 0)),
                pl.BlockSpec((1, d), lambda b, i: (0, 0)), vec, vec] + [full(a) for a in consts]
    args = [x, g.reshape(1, d), scale1p, shift] + consts
    if rope:
        in_specs += [pl.BlockSpec((tm, HEAD_PAD), lambda b, i: (i, 0))] * 3
        args += list(tabs)
    widths = [kw, kw, kw, RW_SHIFT, d]
    return pl.pallas_call(
        functools.partial(_even_in_kernel, rope),
        grid=(bsz, n_rows // tm),
        in_specs=in_specs,
        out_specs=[pl.BlockSpec((1, tm, w), lambda b, i: (b, i, 0)) for w in widths],
        out_shape=[jax.ShapeDtypeStruct((bsz, n_rows, w), BF16) for w in widths],
        compiler_params=_cparams("parallel", "arbitrary"),
        name="even_in",
    )(*args)


def _attn_kernel(nseg, q_ref, *refs):
    k_refs, v_refs, o_ref = refs[:nseg], refs[nseg:2 * nseg], refs[2 * nseg]
    tq = q_ref.shape[1]
    lane = lax.broadcasted_iota(jnp.int32, (tq, LANES), 1)
    for hp in range(MLA_HEADS // 2):
        psl = slice(2 * hp * HEAD_PAD, (2 * hp + 2) * HEAD_PAD)
        outs = []
        for h in (2 * hp, 2 * hp + 1):
            sl = slice(h * HEAD_PAD, (h + 1) * HEAD_PAD)
            q = q_ref[0, :, sl]
            ss = [lax.dot_general(q, k_ref[0, :, sl], _NT, preferred_element_type=F32)
                  for k_ref in k_refs]
            m = functools.reduce(jnp.maximum, [jnp.max(s, axis=-1, keepdims=True) for s in ss])
            acc = None
            for s, v_ref in zip(ss, v_refs):
                for k0 in range(0, s.shape[1], ATTN_TK):
                    pv = jnp.dot(jnp.exp2(s[:, k0:k0 + ATTN_TK] - m).astype(BF16),
                                 v_ref[0, k0:k0 + ATTN_TK, psl], preferred_element_type=F32)
                    acc = pv if acc is None else pv + acc
            acc = acc[:, :HEAD_PAD] if h % 2 == 0 else acc[:, HEAD_PAD:]
            ones_lane = MLA_V if h % 2 == 0 else 0
            outs.append(acc * (1.0 / acc[:, ones_lane:ones_lane + 1]))
        o_ref[0, :, hp * LANES:(hp + 1) * LANES] = jnp.where(lane < MLA_V, outs[0],
                                                             outs[1]).astype(BF16)


def _attention(q, ks, vs):
    bsz, n_q, qw = q.shape
    tq = min(ATTN_TQ, n_q)
    nseg = len(ks)
    in_specs = [pl.BlockSpec((1, tq, qw), lambda b, i: (b, i, 0))]
    in_specs += [pl.BlockSpec((1,) + k.shape[1:], lambda b, i: (b, 0, 0)) for k in ks]
    in_specs += [pl.BlockSpec((1,) + v.shape[1:], lambda b, i: (b, 0, 0)) for v in vs]
    return pl.pallas_call(
        functools.partial(_attn_kernel, nseg),
        grid=(bsz, n_q // tq),
        in_specs=in_specs,
        out_specs=pl.BlockSpec((1, tq, MLA_WIDTH), lambda b, i: (b, i, 0)),
        out_shape=jax.ShapeDtypeStruct((bsz, n_q, MLA_WIDTH), BF16),
        compiler_params=_cparams("parallel", "arbitrary"),
        name="mla_attention",
    )(q, *ks, *vs)


def _rw_prep_kernel(p_ref, prev_ref, next_ref, mup_ref, mun_ref, w0_ref, wup_ref, a0_ref, aup_ref,
                    kk_ref, ka_ref, rk_ref, seg_ref,
                    r_out, v_out, kkn_out, bonus_out, lw0_out, lw1_out, b0_out, b1_out,
                    kd0_out, kd1_out):
    pb = p_ref[0]
    p = pb.astype(F32)
    prev, nxt = _shifted_rows(pb, prev_ref, next_ref, on_mxu=False)
    mu_p, mu_n = mup_ref[...], mun_ref[...]
    ps = p * (1.0 - mu_p - mu_n) + prev * mu_p + nxt * mu_n
    w = RW_WIDTH
    r, k, v = ps[:, :w], ps[:, w:2 * w], ps[:, 2 * w:3 * w]
    wd = jnp.tanh(ps[:, 3 * w:3 * w + 2 * RW_LORA_W])
    ad = ps[:, 3 * w + 2 * RW_LORA_W:]
    seg = seg_ref[...]
    kq = k * kk_ref[...]
    kk = kq * lax.rsqrt(jnp.maximum(_mm(kq * kq, seg, mode="lhs2"), 1e-24))
    r_out[0] = r.astype(BF16)
    v_out[0] = v.astype(BF16)
    kkn_out[0] = kk.astype(BF16)
    bonus_in = jnp.zeros_like(r)
    for d, (lw_out, b_out, kd_out) in enumerate(((lw0_out, b0_out, kd0_out),
                                                  (lw1_out, b1_out, kd1_out))):
        z = w0_ref[d:d + 1, :] + _mm(wd, wup_ref[d], mode="x3")
        lw_out[0] = -math.exp(-0.5) * jax.nn.sigmoid(z)
        a = jax.nn.sigmoid(a0_ref[d:d + 1, :] + _mm(ad, aup_ref[d], mode="x3"))
        kd = k * (1.0 + (a - 1.0) * ka_ref[...])
        b_out[0] = (kk * a).astype(BF16)
        kd_out[0] = kd.astype(BF16)
        bonus_in = bonus_in + r * kd * rk_ref[d:d + 1, :]
    bonus_out[0] = (_mm(bonus_in, seg, mode="lhs2") * v).astype(BF16)


def _rw_prep(p_rw, prm):
    bsz, n_rows, width = p_rw.shape
    tm = min(ROW_TILE, n_rows)
    main, prev, nxt = _halo_specs(tm, width, n_rows)
    full = lambda a: pl.BlockSpec(a.shape, lambda b, i: (0,) * a.ndim)
    consts = [prm["mu_prev"], prm["mu_next"], prm["w0"], prm["w_up"], prm["a0"], prm["a_up"],
              prm["k_k"], prm["k_a"], prm["r_k"], prm["seg"]]
    out_spec = pl.BlockSpec((1, tm, RW_WIDTH), lambda b, i: (b, i, 0))
    shape = lambda dt: jax.ShapeDtypeStruct((bsz, n_rows, RW_WIDTH), dt)
    outs = pl.pallas_call(
        _rw_prep_kernel,
        grid=(bsz, n_rows // tm),
        in_specs=[main, prev, nxt] + [full(a) for a in consts],
        out_specs=[out_spec] * 10,
        out_shape=[shape(BF16)] * 4 + [shape(F32)] * 2 + [shape(BF16)] * 4,
        compiler_params=_cparams("parallel", "arbitrary"),
        name="rwkv_prep",
    )(p_rw, p_rw, p_rw, *consts)
    r, v, kk, bonus, lw0, lw1, b0, b1, kd0, kd1 = outs
    return dict(r=r, v=v, kk=kk, bonus=bonus, lw=(lw0, lw1), b=(b0, b1), kd=(kd0, kd1))


def _rwkv_scan_kernel(rf_ref, vf_ref, kkf_ref, lw0_ref, b0_ref, kd0_ref,
                      rb_ref, vb_ref, kkb_ref, lw1_ref, b1_ref, kd1_ref, s0_ref,
                      of_ref, ob_ref, sfin_ref, st_ref):
    j = pl.program_id(1)

    @pl.when(j == 0)
    def _():
        st_ref[...] = s0_ref[0]

    c = rf_ref.shape[1]
    c2 = 2 * c
    dir_refs = ((rf_ref, vf_ref, kkf_ref, lw0_ref, b0_ref, kd0_ref, of_ref),
                (rb_ref, vb_ref, kkb_ref, lw1_ref, b1_ref, kd1_ref, ob_ref))
    ti = lax.broadcasted_iota(jnp.int32, (c, c), 0)
    si = lax.broadcasted_iota(jnp.int32, (c, c), 1)
    t2 = lax.broadcasted_iota(jnp.int32, (c2, c2), 0)
    s2 = lax.broadcasted_iota(jnp.int32, (c2, c2), 1)
    same = jnp.where((t2 >= c) == (s2 >= c), 1, 0)
    eye2 = jnp.where(t2 == s2, 1.0, 0.0)
    head0 = lax.broadcasted_iota(jnp.int32, (c, RW_PAIR), 1) < RW_HEAD
    masks = []
    for rev in (False, True):
        tri = jnp.where((si >= ti) if rev else (si <= ti), 1.0, 0.0)
        before = jnp.where((s2 > t2) if rev else (s2 < t2), same, 0) == 1
        upto = jnp.where((s2 >= t2) if rev else (s2 <= t2), same, 0) == 1
        masks.append((tri, before, upto))

    def stack(t):
        return jnp.concatenate([jnp.where(head0, t, 0.0), jnp.where(head0, 0.0, t)], axis=0)

    def unstack(t):
        return t[:c] + t[c:]

    mm = functools.partial(_mm, mode="bf16")
    items = [(d, p) for d in range(2) for p in range(N_PAIRS)]
    pair_sl = lambda p: slice(p * RW_PAIR, (p + 1) * RW_PAIR)

    lams = [_mm(masks[d][0], dir_refs[d][3][0], mode="f32") for d in range(2)]

    ops = []
    for d, p in items:
        r_ref, v_ref, kk_ref, lw_ref, b_ref, kd_ref, _ = dir_refs[d]
        sl = pair_sl(p)
        lam = lams[d][:, sl]
        lam_tot = lam[0:1] if d == 1 else lam[c - 1:c]
        e_neg = jnp.exp(-lam)
        e_tail = jnp.exp(lam_tot - lam)
        ld = lambda ref: ref[0, :, sl].astype(F32)
        b, kd = ld(b_ref), ld(kd_ref)
        ops.append(dict(
            at2=stack(-ld(kk_ref) * jnp.exp(lam - lw_ref[0, :, sl])),
            rt2=stack(ld(r_ref) * jnp.exp(lam)),
            bh2=stack(b * e_neg), kh2=stack(kd * e_neg),
            bt2=stack(b * e_tail), kt2=stack(kd * e_tail),
            v2=stack(ld(v_ref)),
            decay_tot=jnp.where(eye2 == 1.0, jnp.exp(lam_tot), 0.0)))

    for (d, p), o in zip(items, ops):
        _, before, upto = masks[d]
        x = mm(jnp.concatenate([o["at2"], o["rt2"]], axis=0),
               jnp.concatenate([o["bh2"], o["kh2"]], axis=0), _NT)
        o["a_ab"] = jnp.where(before, x[:c2, :c2], 0.0)
        o["a_ak"] = jnp.where(before, x[:c2, c2:], 0.0)
        o["a_rb"] = jnp.where(upto, x[c2:, :c2], 0.0)
        o["a_rk"] = jnp.where(upto, x[c2:, c2:], 0.0)

    for o in ops:
        o["t_inv"] = eye2 + o["a_ab"]
        o["pw"] = mm(o["a_ab"], o["a_ab"])
        o["w"] = mm(o["a_ak"], o["v2"])
    for _ in range(int(math.log2(c)) - 2):
        for o in ops:
            y = mm(jnp.concatenate([o["pw"], o["t_inv"]], axis=0), o["pw"])
            o["pw"] = y[:c2]
            o["t_inv"] = o["t_inv"] + y[c2:]
    for o in ops:
        o["t_inv"] = o["t_inv"] + mm(o["t_inv"], o["pw"])

    for o in ops:
        o["z"] = mm(o["t_inv"], jnp.concatenate([o["at2"], o["w"]], axis=1))
    for (d, p), o in zip(items, ops):
        y1 = mm(o["a_rb"], o["z"])
        y2 = mm(o["bt2"].T, o["z"])
        y3 = mm(jnp.concatenate([o["a_rk"], o["kt2"].T], axis=0), o["v2"])
        rp = unstack(o["rt2"] + y1[:, :c2])
        o0 = unstack(y1[:, c2:] + y3[:c2])
        m_bd = o["decay_tot"] + y2[:, :c2]
        n_bd = y2[:, c2:] + y3[c2:]
        rows = slice(d * RW_PAIR, (d + 1) * RW_PAIR)
        sl = pair_sl(p)
        state = st_ref[rows, sl]
        dir_refs[d][6][0, :, sl] = (mm(rp, state) + o0).astype(BF16)
        st_ref[rows, sl] = mm(m_bd, state) + n_bd

    @pl.when(j == pl.num_programs(1) - 1)
    def _():
        sfin_ref[0] = st_ref[...]


def _rwkv_scan(st, s0):
    bsz, n_rows, w = st["r"].shape
    nch = n_rows // CHUNK
    fwd = pl.BlockSpec((1, CHUNK, w), lambda b, j: (b, j, 0))
    bwd = pl.BlockSpec((1, CHUNK, w), lambda b, j: (b, nch - 1 - j, 0))
    sspec = pl.BlockSpec((1, 2 * RW_PAIR, w), lambda b, j: (b, 0, 0))
    row_shape = jax.ShapeDtypeStruct((bsz, n_rows, w), BF16)
    return pl.pallas_call(
        _rwkv_scan_kernel,
        grid=(bsz, nch),
        in_specs=[fwd] * 6 + [bwd] * 6 + [sspec],
        out_specs=[fwd, bwd, sspec],
        out_shape=[row_shape, row_shape, jax.ShapeDtypeStruct((bsz, 2 * RW_PAIR, w), F32)],
        scratch_shapes=[pltpu.VMEM((2 * RW_PAIR, w), F32)],
        compiler_params=_cparams("parallel", "arbitrary"),
        name="rwkv_scan",
    )(st["r"], st["v"], st["kk"], st["lw"][0], st["b"][0], st["kd"][0],
      st["r"], st["v"], st["kk"], st["lw"][1], st["b"][1], st["kd"][1], s0)


def _even_out_kernel(x_ref, gate_ref, om_ref, of_ref, ob_ref, bonus_ref, g_ref, lnw_ref, lnb_ref,
                     seg_ref, w_ref, o_ref):
    seg = seg_ref[...]
    f32 = lambda ref: ref[0].astype(F32)
    o = f32(of_ref) + f32(ob_ref)
    mu = _mm(o, seg, mode="lhs2") * (1.0 / RW_HEAD)
    dlt = o - mu
    var = _mm(dlt * dlt, seg, mode="lhs2") * (1.0 / RW_HEAD)
    o_rw = dlt * lax.rsqrt(var + RW_GN_EPS) * lnw_ref[...] + lnb_ref[...] + f32(bonus_ref)
    g = f32(g_ref)
    z_m = (f32(om_ref) * _silu(g[:, :MLA_WIDTH])).astype(BF16)
    z_r = (o_rw * _silu(g[:, MLA_WIDTH:])).astype(BF16)
    y = (jnp.dot(z_m, w_ref[:MLA_WIDTH, :], preferred_element_type=F32)
         + jnp.dot(z_r, w_ref[MLA_WIDTH:, :], preferred_element_type=F32))
    o_ref[0] = x_ref[0] + gate_ref[0] * y


def _even_out(x, gate, o_mla, o_f, o_b, bonus, g, ln_w, ln_b, seg, w_out):
    bsz, n_rows, d = x.shape
    tm = min(ROW_TILE, n_rows)
    rows = lambda width: pl.BlockSpec((1, tm, width), lambda b, i: (b, i, 0))
    full = lambda a: pl.BlockSpec(a.shape, lambda b, i: (0,) * a.ndim)
    return pl.pallas_call(
        _even_out_kernel,
        grid=(bsz, n_rows // tm),
        in_specs=[rows(d), pl.BlockSpec((1, 1, d), lambda b, i: (b, 0, 0)),
                  rows(MLA_WIDTH), rows(RW_WIDTH), rows(RW_WIDTH), rows(RW_WIDTH), rows(d),
                  full(ln_w), full(ln_b), full(seg), full(w_out)],
        out_specs=rows(d),
        out_shape=jax.ShapeDtypeStruct(x.shape, F32),
        compiler_params=_cparams("parallel", "arbitrary"),
        name="even_out",
    )(x, gate, o_mla, o_f, o_b, bonus, g, ln_w, ln_b, seg, w_out)


def _conv3_kernel(u_ref, prev_ref, next_ref, w_ref, b_ref, v_ref, x0_ref):
    ub = u_ref[0]
    u = ub.astype(F32)
    prev, nxt = _shifted_rows(ub, prev_ref, next_ref, on_mxu=True)
    cv = prev * w_ref[0:1, :] + u * w_ref[1:2, :] + nxt * w_ref[2:3, :] + b_ref[...]
    hw = HY_WIDTH
    x0_ref[0] = cv[:, :hw].astype(BF16)
    v_ref[0] = (cv[:, 2 * hw:] * cv[:, hw:2 * hw]).astype(BF16)


def _conv3(u, conv_w, conv_b):
    bsz, n_rows, width = u.shape
    tm = min(ROW_TILE, n_rows)
    main, prev, nxt = _halo_specs(tm, width, n_rows)
    out_spec = pl.BlockSpec((1, tm, HY_WIDTH), lambda b, i: (b, i, 0))
    out_shape = jax.ShapeDtypeStruct((bsz, n_rows, HY_WIDTH), BF16)
    return pl.pallas_call(
        _conv3_kernel,
        grid=(bsz, n_rows // tm),
        in_specs=[main, prev, nxt,
                  pl.BlockSpec(conv_w.shape, lambda b, i: (0, 0)),
                  pl.BlockSpec((1, width), lambda b, i: (0, 0))],
        out_specs=[out_spec, out_spec],
        out_shape=[out_shape, out_shape],
        compiler_params=_cparams("parallel", "arbitrary"),
        name="hyena_conv3",
    )(u, u, u, conv_w, conv_b.reshape(1, width))


def _hy_hidden_kernel(w1_ref, b1_ref, w2_ref, b2_ref, freq_ref, h_ref):
    n_pos = h_ref.shape[0]
    pos = lax.broadcasted_iota(jnp.int32, (n_pos, 1), 0).astype(F32)
    lane = lax.broadcasted_iota(jnp.int32, (1, LANES), 1)
    band_idx = jnp.where(lane <= HY_BANDS, lane - 1, lane - 1 - HY_BANDS).astype(F32)
    band = 1e-4 + band_idx * ((HY_BANDS - 1 - 1e-4) / (HY_BANDS - 1))
    ang = pos * (2.0 * math.pi / n_pos) * band
    z = jnp.where(lane == 0, pos / (n_pos - 1),
                  jnp.where(lane <= HY_BANDS, jnp.cos(ang),
                            jnp.where(lane <= 2 * HY_BANDS, -jnp.sin(ang), 0.0)))
    freq = freq_ref[...]
    hdn = jnp.sin(freq * (_mm(z, w1_ref[...], mode="f32") + b1_ref[...]))
    for j in range(HY_INNER):
        hdn = jnp.sin(freq * (_mm(hdn, w2_ref[j], mode="f32") + b2_ref[j]))
    h_ref[...] = hdn


def _hy_filter_kernel(h_ref, w0_ref, w1_ref, dl_ref, fs_ref, fd_ref):
    n_pos = h_ref.shape[0]
    pos = lax.broadcasted_iota(jnp.int32, (n_pos, 1), 0)
    t = pos.astype(F32) / (n_pos - 1)
    dec = jnp.exp(-t * dl_ref[...])
    hdn = h_ref[...]
    f_fwd = _mm(hdn, w0_ref[...], mode="f32") * dec
    f_bwd = jnp.where(pos == 0, 0.0, _mm(hdn, w1_ref[...], mode="f32") * dec)
    inv = 1.0 / (jnp.sum(jnp.abs(f_fwd), axis=0, keepdims=True)
                 + jnp.sum(jnp.abs(f_bwd), axis=0, keepdims=True))
    fs_ref[...] = (f_fwd + f_bwd) * inv
    fd_ref[...] = (f_bwd - f_fwd) * inv


def _hyena_filters(n_pos, f_w1, f_b1, f_w2, f_b2, f_wout, freq, deltas):
    w1 = jnp.zeros((LANES, HY_ORDER), F32).at[:HY_EMB].set(f_w1)
    hdn = pl.pallas_call(
        _hy_hidden_kernel,
        out_shape=jax.ShapeDtypeStruct((n_pos, HY_ORDER), F32),
        name="hyena_filter_hidden",
    )(w1, f_b1.reshape(1, HY_ORDER), f_w2, f_b2.reshape(HY_INNER, 1, HY_ORDER),
      freq.reshape(1, HY_ORDER))
    tn = 256
    cspec = pl.BlockSpec((HY_ORDER, tn), lambda j: (0, j))
    ospec = pl.BlockSpec((n_pos, tn), lambda j: (0, j))
    oshape = jax.ShapeDtypeStruct((n_pos, HY_WIDTH), F32)
    return pl.pallas_call(
        _hy_filter_kernel,
        grid=(HY_WIDTH // tn,),
        in_specs=[pl.BlockSpec((n_pos, HY_ORDER), lambda j: (0, 0)), cspec, cspec,
                  pl.BlockSpec((1, tn), lambda j: (0, j))],
        out_specs=[ospec, ospec],
        out_shape=[oshape, oshape],
        compiler_params=_cparams("arbitrary"),
        name="hyena_filter",
    )(hdn, f_wout[:, :HY_WIDTH], f_wout[:, HY_WIDTH:], deltas)


def _dft_tables_kernel(ec_ref, es_ref, oc_ref, os_ref, oct_ref, ost_ref):
    tr, kq = ec_ref.shape
    n = 4 * kq
    r = lax.broadcasted_iota(jnp.int32, (tr, kq), 0) + pl.program_id(0) * tr
    c = lax.broadcasted_iota(jnp.int32, (tr, kq), 1)

    def cos_sin(phase):
        ph = phase & (n - 1)
        ph = jnp.where(ph >= n // 2, ph - n, ph)
        ang = ph.astype(F32) * (2.0 * math.pi / n)
        return jnp.cos(ang).astype(BF16), jnp.sin(ang).astype(BF16)

    ec_ref[...], es_ref[...] = cos_sin(2 * r * c)
    oc_ref[...], os_ref[...] = cos_sin(r * (2 * c + 1))
    oct_ref[...], ost_ref[...] = cos_sin(c * (2 * r + 1))


def _dft_tables(n_pos):
    kq = n_pos // 2
    tr = min(256, kq)
    spec = pl.BlockSpec((tr, kq), lambda i: (i, 0))
    shape = jax.ShapeDtypeStruct((kq, kq), BF16)
    return pl.pallas_call(
        _dft_tables_kernel,
        grid=(kq // tr,),
        out_specs=[spec] * 6,
        out_shape=[shape] * 6,
        compiler_params=_cparams("arbitrary"),
        name="dft_tables",
    )()


def _alt_sign(n_pos):
    pos = lax.broadcasted_iota(jnp.int32, (n_pos, 1), 0)
    return jnp.where((pos & 1) == 0, 1.0, -1.0), pos


def _spectrum_kernel(fse_ref, fso_ref, fde_ref, fdo_ref, ec_ref, es_ref, oc_ref, os_ref,
                     hare_ref, haim_ref, hbre_ref, hbim_ref, hk_ref):
    kq = fse_ref.shape[0]
    alt, pos = _alt_sign(kq)

    def mm2(tab_ref, f):
        fh = f.astype(BF16)
        fl = (f - fh.astype(F32)).astype(BF16)
        return (jnp.dot(tab_ref[...], fh, preferred_element_type=F32)
                + jnp.dot(tab_ref[...], fl, preferred_element_type=F32))

    fse, fdo = fse_ref[...], fdo_ref[...]
    ce, co = mm2(ec_ref, fse), mm2(oc_ref, fso_ref[...])
    se, so = mm2(es_ref, fde_ref[...]), mm2(os_ref, fdo)
    scale = 0.5 / kq
    scale_re = jnp.where(pos == 0, 0.5 * scale, scale)
    hare_ref[...] = (ce + co) * scale_re
    hbre_ref[...] = (ce - co) * scale_re
    haim_ref[...] = (se + so) * scale
    hbim_ref[...] = (so - se) * scale
    hk_ref[0:1, :] = jnp.sum(fse * alt, axis=0, keepdims=True) * scale
    hk_ref[1:2, :] = jnp.sum(fdo * alt, axis=0, keepdims=True) * scale


def _table_specs(tabs, nd):
    imap = (lambda j: (0, 0)) if nd == 1 else (lambda b, j: (0, 0))
    return [pl.BlockSpec(t.shape, imap, pipeline_mode=pl.Buffered(1)) for t in tabs]


def _spectrum(fs, fd, tabs):
    n_pos, width = fs.shape
    kq, tc, nct = n_pos // 2, DFT_TC, width // DFT_TC
    even = pl.BlockSpec((kq, tc), lambda j: (0, j))
    odd = pl.BlockSpec((kq, tc), lambda j: (0, nct + j))
    fs_v, fd_v = fs.reshape(kq, 2 * width), fd.reshape(kq, 2 * width)
    return pl.pallas_call(
        _spectrum_kernel,
        grid=(nct,),
        in_specs=[even, odd, even, odd] + _table_specs(tabs[:4], 1),
        out_specs=[even] * 4 + [pl.BlockSpec((2, tc), lambda j: (0, j))],
        out_shape=[jax.ShapeDtypeStruct((kq, width), F32)] * 4
        + [jax.ShapeDtypeStruct((2, width), F32)],
        compiler_params=_cparams("arbitrary"),
        name="hyena_spectrum",
    )(fs_v, fs_v, fd_v, fd_v, *tabs[:4])


def _dft_conv_kernel(ve_ref, vo_ref, hare_ref, haim_ref, hbre_ref, hbim_ref, hk_ref,
                     ec_ref, es_ref, oc_ref, os_ref, oct_ref, ost_ref, y_ref,
                     pe_scr, qe_scr, po_scr, qo_scr):
    kq, tc = ve_ref.shape[1], ve_ref.shape[2]
    tm = min(DFT_TM, kq)
    alt, _ = _alt_sign(kq)
    ve, vo = ve_ref[0], vo_ref[0]
    dot = functools.partial(jnp.dot, preferred_element_type=F32)
    for r0 in range(0, kq, tm):
        rows = slice(r0, r0 + tm)
        ce, co = dot(ec_ref[rows, :], ve), dot(oc_ref[rows, :], vo)
        se, so = dot(es_ref[rows, :], ve), dot(os_ref[rows, :], vo)
        va_re, va_s, vb_re, vb_s = ce + co, se + so, ce - co, so - se
        hare, haim = hare_ref[rows, :], haim_ref[rows, :]
        hbre, hbim = hbre_ref[rows, :], hbim_ref[rows, :]
        ya_re, ya_s = va_re * hare + va_s * haim, va_s * hare - va_re * haim
        yb_re, yb_s = vb_re * hbre + vb_s * hbim, vb_s * hbre - vb_re * hbim
        pe_scr[rows, :] = (ya_re + yb_re).astype(BF16)
        qe_scr[rows, :] = (ya_s - yb_s).astype(BF16)
        po_scr[rows, :] = (ya_re - yb_re).astype(BF16)
        qo_scr[rows, :] = (ya_s + yb_s).astype(BF16)
    vk_re = jnp.sum(ve.astype(F32) * alt, axis=0, keepdims=True)
    vk_s = jnp.sum(vo.astype(F32) * alt, axis=0, keepdims=True)
    hk_re, hk_im = hk_ref[0:1, :], hk_ref[1:2, :]
    yk_re = vk_re * hk_re + vk_s * hk_im
    yk_s = vk_s * hk_re - vk_re * hk_im
    pe, qe, po, qo = pe_scr[...], qe_scr[...], po_scr[...], qo_scr[...]
    for r0 in range(0, kq, tm):
        rows = slice(r0, r0 + tm)
        y_ref[0, rows, :tc] = (dot(ec_ref[rows, :], pe) + dot(es_ref[rows, :], qe)
                               + alt[rows] * yk_re).astype(BF16)
        y_ref[0, rows, tc:] = (dot(oct_ref[rows, :], po) + dot(ost_ref[rows, :], qo)
                               + alt[rows] * yk_s).astype(BF16)


def _dft_conv(v, spec, tabs):
    bsz, n_pos, width = v.shape
    kq, tc, nct = n_pos // 2, DFT_TC, width // DFT_TC
    v_view = v.reshape(bsz, kq, 2 * width)
    even = pl.BlockSpec((1, kq, tc), lambda b, j: (b, 0, j))
    odd = pl.BlockSpec((1, kq, tc), lambda b, j: (b, 0, nct + j))
    col = pl.BlockSpec((kq, tc), lambda b, j: (0, j))
    return pl.pallas_call(
        _dft_conv_kernel,
        grid=(bsz, nct),
        in_specs=[even, odd, col, col, col, col, pl.BlockSpec((2, tc), lambda b, j: (0, j))]
        + _table_specs(tabs, 2),
        out_specs=pl.BlockSpec((1, kq, 2 * tc), lambda b, j: (b, 0, j)),
        out_shape=jax.ShapeDtypeStruct((bsz, kq, 2 * width), BF16),
        scratch_shapes=[pltpu.VMEM((kq, tc), BF16)] * 4,
        compiler_params=_cparams("parallel", "arbitrary"),
        name="hyena_dft_conv",
    )(v_view, v_view, *spec, *tabs)


def _hy_out_kernel(x_ref, gate_ref, y_ref, v_ref, x0_ref, g_ref, bias_ref, w_ref, o_ref):
    d = w_ref.shape[0]
    half = x_ref.shape[1]
    zs = []
    for par in range(2):
        lanes = slice(par * d, (par + 1) * d)
        f32 = lambda ref: ref[0, :, lanes].astype(F32)
        y = jnp.concatenate([y_ref[0, :, (2 * j + par) * DFT_TC:(2 * j + par + 1) * DFT_TC]
                             for j in range(d // DFT_TC)], axis=1).astype(F32)
        z = (y + f32(v_ref) * bias_ref[...]) * f32(x0_ref) * _silu(f32(g_ref))
        zs.append(z.astype(BF16))
    out = jnp.dot(jnp.concatenate(zs, axis=0), w_ref[...], preferred_element_type=F32)
    for par in range(2):
        lanes = slice(par * d, (par + 1) * d)
        o_ref[0, :, lanes] = x_ref[0, :, lanes] + gate_ref[0] * out[par * half:(par + 1) * half]


def _hy_out(x, gate, y_view, v, x0, g, bias_d, w_out):
    bsz, n_rows, d = x.shape
    half_rows = n_rows // 2
    tm = min(ROW_TILE // 2, half_rows)
    view = lambda a: a.reshape(bsz, half_rows, 2 * d)
    rows = pl.BlockSpec((1, tm, 2 * d), lambda b, i: (b, i, 0))
    out = pl.pallas_call(
        _hy_out_kernel,
        grid=(bsz, half_rows // tm),
        in_specs=[rows, pl.BlockSpec((1, 1, d), lambda b, i: (b, 0, 0)), rows, rows, rows, rows,
                  pl.BlockSpec((1, d), lambda b, i: (0, 0)),
                  pl.BlockSpec(w_out.shape, lambda b, i: (0, 0))],
        out_specs=rows,
        out_shape=jax.ShapeDtypeStruct((bsz, half_rows, 2 * d), F32),
        compiler_params=_cparams("parallel", "arbitrary"),
        name="hyena_out",
    )(view(x), gate, y_view, view(v), view(x0), view(g), bias_d.reshape(1, d), w_out)
    return out.reshape(bsz, n_rows, d)


def _rope_perm():
    pairs = np.arange(MLA_ROPE // 2)
    return np.concatenate([np.arange(MLA_NOPE), MLA_NOPE + 2 * pairs, MLA_NOPE + 2 * pairs + 1])


def _rope_tables(n_pos):
    rows = n_pos // GRID_W
    row = jnp.repeat(jnp.arange(rows, dtype=F32), GRID_W)
    col = jnp.tile(jnp.arange(GRID_W, dtype=F32), rows)
    n_freq = MLA_ROPE // 4
    inv = ROPE_BASE ** (-jnp.arange(n_freq, dtype=F32) / n_freq)
    ang = jnp.concatenate([row[:, None] * inv, col[:, None] * inv], axis=-1)
    cos, sin = jnp.cos(ang), jnp.sin(ang)
    half = MLA_ROPE // 2
    ones = jnp.ones((n_pos, MLA_NOPE), F32)
    zeros = jnp.zeros((n_pos, MLA_NOPE), F32)
    pad1 = jnp.ones((n_pos, HEAD_PAD - MLA_QK), F32)
    pad0 = jnp.zeros((n_pos, HEAD_PAD - MLA_QK), F32)
    z16 = jnp.zeros((n_pos, half), F32)
    cos_f = jnp.concatenate([ones, cos, cos, pad1], axis=-1)
    sin_a = jnp.concatenate([zeros, z16, sin, pad0], axis=-1)
    sin_b = jnp.concatenate([zeros, -sin, z16, pad0], axis=-1)
    return cos_f, sin_a, sin_b


def _pad_heads(w, width):
    k, h, _ = w.shape
    return jnp.zeros((k, h, HEAD_PAD), w.dtype).at[:, :, :width].set(w).reshape(k, h * HEAD_PAD)


def _even_weights(e, ev_w_in, ev_w_out, mla_q_a_norm, mla_w_uq, mla_kv_a_norm, mla_w_ukv,
                  mla_q_norm, mla_k_norm, rwkv_mu_prev, rwkv_mu_next, rwkv_w0, rwkv_w_up, rwkv_a0,
                  rwkv_a_up, rwkv_k_k, rwkv_k_a, rwkv_r_k, rwkv_ln_w, rwkv_ln_b):
    perm = _rope_perm()
    w_in = ev_w_in[e]
    d = w_in.shape[0]
    o_dkv = EV_DQ
    o_rw = EV_DQ + EV_DKV
    o_g = o_rw + RW_SHIFT
    w_dkv = jnp.zeros((d, 2 * LANES), F32)
    w_dkv = w_dkv.at[:, :MLA_KV_LORA].set(w_in[:, o_dkv:o_dkv + MLA_KV_LORA])
    rope_cols = o_dkv + MLA_KV_LORA + (perm[MLA_NOPE:] - MLA_NOPE)
    w_dkv = w_dkv.at[:, LANES + MLA_NOPE:LANES + MLA_QK].set(w_in[:, rope_cols])
    proj = [w_in[:, :EV_DQ].astype(BF16), w_dkv.astype(BF16),
            w_in[:, o_rw:o_g].astype(BF16), w_in[:, o_g:].astype(BF16)]
    w_ukv = mla_w_ukv[e]
    pad_gain = lambda g: jnp.zeros((1, HEAD_PAD), F32).at[0, :MLA_QK].set(g[perm])
    seg_id = np.arange(RW_WIDTH) // RW_HEAD
    zero_up = lambda up, dd: jnp.zeros((2 * RW_LORA_W, RW_WIDTH), F32).at[
        dd * RW_LORA_W:(dd + 1) * RW_LORA_W].set(up[dd])
    rw = dict(
        mu_prev=rwkv_mu_prev[e].reshape(1, RW_SHIFT), mu_next=rwkv_mu_next[e].reshape(1, RW_SHIFT),
        w0=rwkv_w0[e], a0=rwkv_a0[e],
        w_up=jnp.stack([zero_up(rwkv_w_up[e], 0), zero_up(rwkv_w_up[e], 1)]),
        a_up=jnp.stack([zero_up(rwkv_a_up[e], 0), zero_up(rwkv_a_up[e], 1)]),
        k_k=rwkv_k_k[e].reshape(1, RW_WIDTH), k_a=rwkv_k_a[e].reshape(1, RW_WIDTH),
        r_k=rwkv_r_k[e].reshape(2, RW_WIDTH),
        seg=jnp.asarray((seg_id[:, None] == seg_id[None, :]).astype(np.float32)),
    )
    head_par = np.arange(MLA_HEADS) % 2
    v_lane = (np.arange(HEAD_PAD)[None, :] >= MLA_V) == (head_par[:, None] == 1)
    v_ones = (~v_lane).astype(np.float32)
    w_vh = w_ukv[:, :, MLA_NOPE:].reshape(MLA_KV_LORA, MLA_HEADS // 2, 2, MLA_V)
    zero_v = jnp.zeros_like(w_vh[:, :, 0])
    w_v = jnp.stack([w_vh[:, :, 0], zero_v, zero_v, w_vh[:, :, 1]], axis=2)
    return dict(
        proj=proj,
        q_a_norm=mla_q_a_norm[e], kv_a_norm=mla_kv_a_norm[e],
        w_uq=_pad_heads(mla_w_uq[e][:, :, perm], MLA_QK).astype(BF16),
        w_k=_pad_heads(w_ukv[:, :, :MLA_NOPE], MLA_NOPE).astype(BF16),
        w_v=w_v.reshape(MLA_KV_LORA, MLA_HEADS * HEAD_PAD).astype(BF16),
        v_ones=jnp.asarray(v_ones.reshape(1, MLA_HEADS * HEAD_PAD)),
        q_gain=pad_gain(mla_q_norm[e]) * (MLA_QK ** -0.5 * math.log2(math.e)),
        k_gain=pad_gain(mla_k_norm[e]),
        rw=rw,
        ln_w=rwkv_ln_w[e].reshape(1, RW_WIDTH), ln_b=rwkv_ln_b[e].reshape(1, RW_WIDTH),
        w_out=ev_w_out[e].astype(BF16),
    )


def _rwkv_branch(st_c, st, want_ctx):
    bsz = st["r"].shape[0]
    zero_state = jnp.zeros((bsz, 2 * RW_PAIR, RW_WIDTH), F32)
    of_c, ob_c, s_c = _rwkv_scan(st_c, zero_state)
    o_f, o_b, _ = _rwkv_scan(st, s_c)
    return [o_f, o_b], ([of_c, ob_c] if want_ctx else None)


def _even_layer(x, xc, mod, mod_c, g_norm, wts, rope_tabs, ctx_out):
    shift, scale1p, gate = mod
    shift_c, scale1p_c, gate_c = mod_c
    q, k, v, p_rw, g = _even_in(x, g_norm, scale1p, shift, wts, rope_tabs)
    qc, kc, vc, pc_rw, gc = _even_in(xc, g_norm, scale1p_c, shift_c, wts, None)
    o_mla = _attention(q, [k, kc], [v, vc])
    st_c = _rw_prep(pc_rw, wts["rw"])
    st = _rw_prep(p_rw, wts["rw"])
    outs, outs_c = _rwkv_branch(st_c, st, ctx_out)
    seg = wts["rw"]["seg"]
    x_new = _even_out(x, gate, o_mla, outs[0], outs[1], st["bonus"], g, wts["ln_w"], wts["ln_b"],
                      seg, wts["w_out"])
    if not ctx_out:
        return x_new, None
    oc_mla = _attention(qc, [kc], [vc])
    xc_new = _even_out(xc, gate_c, oc_mla, outs_c[0], outs_c[1], st_c["bonus"], gc, wts["ln_w"],
                       wts["ln_b"], seg, wts["w_out"])
    return x_new, xc_new


def _hyena_layer(x, mod, g_norm, wts, tables):
    shift, scale1p, gate = mod
    n_pos = x.shape[1]
    u, g = _norm_proj(x, g_norm, scale1p, shift, wts["proj"])
    v, x0 = _conv3(u, wts["conv_w"], wts["conv_b"])
    fs, fd = _hyena_filters(n_pos, *wts["filt"])
    spec = _spectrum(fs, fd, tables)
    y_view = _dft_conv(v, spec, tables)
    return _hy_out(x, gate, y_view, v, x0, g, wts["bias_d"], wts["w_out"])


def kernel(x, c, ctx, c_ctx, mod_w, mod_b, norm_g, ev_w_in, ev_w_out, mla_q_a_norm, mla_w_uq, mla_kv_a_norm, mla_w_ukv, mla_q_norm, mla_k_norm, rwkv_mu_prev, rwkv_mu_next, rwkv_w0, rwkv_w_up, rwkv_a0, rwkv_a_up, rwkv_k_k, rwkv_k_a, rwkv_r_k, rwkv_ln_w, rwkv_ln_b, od_w_in, od_w_out, hy_conv_w, hy_conv_b, hy_bias_d, hy_f_w1, hy_f_b1, hy_f_w2, hy_f_b2, hy_f_wout, hy_freq):
    bsz, n_lat, d = x.shape
    n_ctx = ctx.shape[1]
    assert n_lat % max(CHUNK, GRID_W) == 0 and n_ctx % CHUNK == 0 and d == D_MODEL
    assert CHUNK == RW_HEAD and 2 * CHUNK == RW_PAIR

    n_rows = -(-(bsz + 1) // 16) * 16
    cvec = jnp.zeros((n_rows, d), F32).at[:bsz].set(c).at[bsz].set(c_ctx)
    mods = _modulation(cvec, mod_w, mod_b)

    def split_mod(i, lo, hi, reps):
        m = mods[i, lo:hi]
        m = jnp.broadcast_to(m, (reps, 3 * d)) if hi - lo == 1 else m
        m = m[:, None, :]
        return m[..., :d], 1.0 + m[..., d:2 * d], m[..., 2 * d:]

    rope_tabs = _rope_tables(n_lat)
    deltas = jnp.abs(jnp.linspace(math.log(HY_TARGET) / HY_FAST_DECAY,
                                  math.log(HY_TARGET) / HY_SLOW_DECAY, HY_WIDTH,
                                  dtype=F32)).reshape(1, HY_WIDTH)
    tables = {n_lat: _dft_tables(n_lat)}

    xc = ctx
    for i in range(DEPTH):
        ctx_needed_later = any(j > i and j % 2 == 0 for j in range(DEPTH))
        mod = split_mod(i, 0, bsz, bsz)
        mod_c = split_mod(i, bsz, bsz + 1, bsz)
        if i % 2 == 0:
            wts = _even_weights(i // 2, ev_w_in, ev_w_out, mla_q_a_norm, mla_w_uq, mla_kv_a_norm,
                                mla_w_ukv, mla_q_norm, mla_k_norm, rwkv_mu_prev, rwkv_mu_next,
                                rwkv_w0, rwkv_w_up, rwkv_a0, rwkv_a_up, rwkv_k_k, rwkv_k_a,
                                rwkv_r_k, rwkv_ln_w, rwkv_ln_b)
            x, xc_new = _even_layer(x, xc, mod, mod_c, norm_g[i], wts, rope_tabs, ctx_needed_later)
            xc = xc_new if ctx_needed_later else xc
        else:
            o = i // 2
            w_in = od_w_in[o]
            wts = dict(
                proj=[w_in[:, :3 * HY_WIDTH].astype(BF16), w_in[:, 3 * HY_WIDTH:].astype(BF16)],
                conv_w=hy_conv_w[o], conv_b=hy_conv_b[o], bias_d=hy_bias_d[o],
                filt=(hy_f_w1[o], hy_f_b1[o], hy_f_w2[o], hy_f_b2[o], hy_f_wout[o], hy_freq[o],
                      deltas),
                w_out=od_w_out[o].astype(BF16),
            )
            if ctx_needed_later:
                if n_ctx not in tables:
                    tables[n_ctx] = _dft_tables(n_ctx)
                xc = _hyena_layer(xc, mod_c, norm_g[i], wts, tables[n_ctx])
            x = _hyena_layer(x, mod, norm_g[i], wts, tables[n_lat])
    return x
```

```python
import functools
import math

import numpy as np
import jax
import jax.numpy as jnp
from jax import lax
from jax.experimental import pallas as pl
from jax.experimental.pallas import tpu as pltpu

F32 = jnp.float32
BF16 = jnp.bfloat16
HIGHEST = lax.Precision.HIGHEST

D_MODEL = 1024
DEPTH = 4
GRID_W = 64
NORM_EPS = 1e-6
MLA_HEADS = 8
MLA_NOPE = 64
MLA_ROPE = 32
MLA_QK = MLA_NOPE + MLA_ROPE
MLA_V = 64
MLA_Q_LORA = 256
MLA_KV_LORA = 128
MLA_WIDTH = MLA_HEADS * MLA_V
ROPE_BASE = 10000.0
RW_HEADS = 8
RW_HEAD = 64
RW_WIDTH = RW_HEADS * RW_HEAD
RW_LORA_W = 64
RW_LORA_A = 64
RW_SHIFT = 3 * RW_WIDTH + 2 * RW_LORA_W + 2 * RW_LORA_A
RW_GN_EPS = 64e-5
EV_DQ = MLA_Q_LORA
EV_DKV = MLA_KV_LORA + MLA_ROPE
HY_WIDTH = D_MODEL
HY_ORDER = 64
HY_BANDS = 16
HY_EMB = 1 + 2 * HY_BANDS
HY_INNER = 2
HY_FAST_DECAY = 0.3
HY_SLOW_DECAY = 1.5
HY_TARGET = 1e-2

LANES = 128
HEAD_PAD = 128
RW_PAIR = 2 * RW_HEAD
N_PAIRS = RW_WIDTH // RW_PAIR
CHUNK = 64
ROW_TILE = 256
ATTN_TQ = 256
ATTN_TK = 256
DFT_TC = 256
DFT_TM = 256
HALO_ROWS = 16
SUB_ROWS = 128
VMEM_LIMIT = 56 * 1024 * 1024

_NN = (((1,), (0,)), ((), ()))
_NT = (((1,), (1,)), ((), ()))


def _mm(a, b, dn=_NN, mode="bf16"):
    if mode == "f32":
        return lax.dot_general(a, b, dn, precision=HIGHEST, preferred_element_type=F32)
    dg = functools.partial(lax.dot_general, dimension_numbers=dn, preferred_element_type=F32)
    ah = a.astype(BF16)
    bh = b.astype(BF16)
    if mode == "bf16":
        return dg(ah, bh)
    al = (a - ah.astype(F32)).astype(BF16)
    if mode == "lhs2":
        return dg(ah, bh) + dg(al, bh)
    bl = (b - bh.astype(F32)).astype(BF16)
    return dg(ah, bh) + (dg(ah, bl) + dg(al, bh))


def _cparams(*sem):
    return pltpu.CompilerParams(dimension_semantics=sem, vmem_limit_bytes=VMEM_LIMIT)


def _silu(t):
    return t * jax.nn.sigmoid(t)


def _shifted_rows(pb, prev_ref, next_ref, on_mxu):
    tm = pb.shape[0]
    i = pl.program_id(1)
    last = pl.num_programs(1) - 1
    prev_row = jnp.where(i > 0, prev_ref[0, HALO_ROWS - 1:HALO_ROWS, :].astype(F32), 0.0)
    next_row = jnp.where(i < last, next_ref[0, 0:1, :].astype(F32), 0.0)
    if on_mxu:
        out_row = lax.broadcasted_iota(jnp.int32, (2 * tm, tm), 0)
        src_row = lax.broadcasted_iota(jnp.int32, (2 * tm, tm), 1)
        want = jnp.where(out_row < tm, out_row - 1, out_row - tm + 1)
        both = jnp.dot(jnp.where(src_row == want, 1.0, 0.0).astype(BF16), pb,
                       preferred_element_type=F32)
        down, up = both[:tm], both[tm:]
    else:
        p = pb.astype(F32)
        down, up = pltpu.roll(p, 1, axis=0), pltpu.roll(p, tm - 1, axis=0)
    sub = lax.broadcasted_iota(jnp.int32, (8, 1), 0)
    prev = jnp.concatenate([jnp.where(sub == 0, prev_row, down[:8]), down[8:]], axis=0)
    nxt = jnp.concatenate([up[:tm - 8], jnp.where(sub == 7, next_row, up[tm - 8:])], axis=0)
    return prev, nxt


def _split_rows(val, scr):
    rows, width = val.shape
    half = rows // 2
    for j in range(width // LANES):
        scr[j] = val[:, j * LANES:(j + 1) * LANES]
    pick = lambda start: jnp.concatenate(
        [scr[j, pl.ds(start, half, stride=2), :] for j in range(width // LANES)], axis=1)
    return pick(0), pick(1)


def _merge_rows(even, odd, scr):
    half, width = even.shape
    for j in range(width // LANES):
        cols = slice(j * LANES, (j + 1) * LANES)
        scr[j, pl.ds(0, half, stride=2), :] = even[:, cols]
        scr[j, pl.ds(1, half, stride=2), :] = odd[:, cols]
    return jnp.concatenate([scr[j] for j in range(width // LANES)], axis=1)


def _row_split_scratch(rows, width):
    return pltpu.VMEM((width // LANES, rows, LANES), F32)


def _halo_specs(tm, width, n_rows):
    th = tm // HALO_ROWS
    last_h = n_rows // HALO_ROWS - 1
    main = pl.BlockSpec((1, tm, width), lambda b, i: (b, i, 0))
    prev = pl.BlockSpec((1, HALO_ROWS, width), lambda b, i: (b, jnp.maximum(i * th - 1, 0), 0))
    nxt = pl.BlockSpec((1, HALO_ROWS, width),
                       lambda b, i: (b, jnp.minimum((i + 1) * th, last_h), 0))
    return main, prev, nxt


def _mod_kernel(c_ref, w_ref, b_ref, o_ref):
    o_ref[0] = _mm(_silu(c_ref[...]), w_ref[0], mode="x3") + b_ref[0]


def _modulation(cvec, mod_w, mod_b):
    rows, d = cvec.shape
    n = mod_w.shape[-1]
    tn = 1024
    return pl.pallas_call(
        _mod_kernel,
        grid=(DEPTH, n // tn),
        in_specs=[pl.BlockSpec((rows, d), lambda i, j: (0, 0)),
                  pl.BlockSpec((1, d, tn), lambda i, j: (i, 0, j)),
                  pl.BlockSpec((1, 1, tn), lambda i, j: (i, 0, j))],
        out_specs=pl.BlockSpec((1, rows, tn), lambda i, j: (i, 0, j)),
        out_shape=jax.ShapeDtypeStruct((DEPTH, rows, n), F32),
        compiler_params=_cparams("arbitrary", "arbitrary"),
        name="modulation",
    )(cvec, mod_w, mod_b.reshape(DEPTH, 1, n))


def _norm_proj_kernel(nw, x_ref, g_ref, sc_ref, sh_ref, *refs):
    x = x_ref[0]
    h = x * lax.rsqrt(jnp.mean(x * x, axis=-1, keepdims=True) + NORM_EPS) * g_ref[...]
    hb = (h * sc_ref[0] + sh_ref[0]).astype(BF16)
    for w_ref, o_ref in zip(refs[:nw], refs[nw:]):
        o_ref[0] = jnp.dot(hb, w_ref[...], preferred_element_type=F32).astype(BF16)


def _norm_proj(x, g, scale1p, shift, weights):
    bsz, n_rows, d = x.shape
    tm = min(ROW_TILE, n_rows)
    vec = pl.BlockSpec((1, 1, d), lambda b, i: (b, 0, 0))
    in_specs = [pl.BlockSpec((1, tm, d), lambda b, i: (b, i, 0)),
                pl.BlockSpec((1, d), lambda b, i: (0, 0)), vec, vec]
    in_specs += [pl.BlockSpec(w.shape, lambda b, i: (0, 0)) for w in weights]
    return pl.pallas_call(
        functools.partial(_norm_proj_kernel, len(weights)),
        grid=(bsz, n_rows // tm),
        in_specs=in_specs,
        out_specs=[pl.BlockSpec((1, tm, w.shape[1]), lambda b, i: (b, i, 0)) for w in weights],
        out_shape=[jax.ShapeDtypeStruct((bsz, n_rows, w.shape[1]), BF16) for w in weights],
        compiler_params=_cparams("parallel", "arbitrary"),
        name="norm_proj",
    )(x, g.reshape(1, d), scale1p, shift, *weights)


def _head_norm_rope(t, gain, tabs):
    ms = jnp.sum(t * t, axis=-1, keepdims=True) * (1.0 / MLA_QK)
    t = t * lax.rsqrt(ms + NORM_EPS) * gain
    if tabs is not None:
        cos_f, sin_a, sin_b = tabs
        t = (t * cos_f + pltpu.roll(t, MLA_ROPE // 2, axis=1) * sin_a
             + pltpu.roll(t, HEAD_PAD - MLA_ROPE // 2, axis=1) * sin_b)
    return t


def _even_in_kernel(rope, x_ref, g_ref, sc_ref, sh_ref, wdq_ref, wdkv_ref, wrw_ref, wg_ref,
                    qan_ref, wuq_ref, qgn_ref, kan_ref, wk_ref, wv_ref, vone_ref, kgn_ref, *refs):
    q_ref, k_ref, v_ref, prw_ref, go_ref = refs[-5:]
    q_gain, k_gain = qgn_ref[...], kgn_ref[...]

    def rms(t, gain_ref):
        return t * lax.rsqrt(jnp.mean(t * t, axis=-1, keepdims=True) + NORM_EPS) * gain_ref[...]

    tm = x_ref.shape[1]
    sub = min(SUB_ROWS, tm)
    for r0 in range(0, tm, sub):
        rows = slice(r0, r0 + sub)
        tabs = tuple(r[rows, :] for r in refs[:3]) if rope else None
        hb = (rms(x_ref[0, rows, :], g_ref) * sc_ref[0] + sh_ref[0]).astype(BF16)
        prw_ref[0, rows, :] = jnp.dot(hb, wrw_ref[...], preferred_element_type=F32).astype(BF16)
        go_ref[0, rows, :] = jnp.dot(hb, wg_ref[...], preferred_element_type=F32).astype(BF16)
        p_dq = jnp.dot(hb, wdq_ref[...], preferred_element_type=F32)
        q = jnp.dot(rms(p_dq, qan_ref).astype(BF16), wuq_ref[...], preferred_element_type=F32)
        p_dkv = jnp.dot(hb, wdkv_ref[...], preferred_element_type=F32)
        k_rope = p_dkv[:, MLA_KV_LORA:]
        ab = rms(p_dkv[:, :MLA_KV_LORA], kan_ref).astype(BF16)
        k_nope = jnp.dot(ab, wk_ref[...], preferred_element_type=F32)
        v_ref[0, rows, :] = (jnp.dot(ab, wv_ref[...], preferred_element_type=F32)
                             + vone_ref[...]).astype(BF16)
        for hd in range(MLA_HEADS):
            sl = slice(hd * HEAD_PAD, (hd + 1) * HEAD_PAD)
            q_ref[0, rows, sl] = _head_norm_rope(q[:, sl], q_gain, tabs).astype(BF16)
            k_ref[0, rows, sl] = _head_norm_rope(k_nope[:, sl] + k_rope, k_gain,
                                                 tabs).astype(BF16)


def _even_in(x, g, scale1p, shift, wts, tabs):
    bsz, n_rows, d = x.shape
    tm = min(ROW_TILE, n_rows)
    rope = tabs is not None
    kw = MLA_HEADS * HEAD_PAD
    vec = pl.BlockSpec((1, 1, d), lambda b, i: (b, 0, 0))
    full = lambda a: pl.BlockSpec(a.shape, lambda b, i: (0,) * a.ndim)
    consts = list(wts["proj"]) + [wts["q_a_norm"].reshape(1, MLA_Q_LORA), wts["w_uq"], wts["q_gain"],
                                  wts["kv_a_norm"].reshape(1, MLA_KV_LORA), wts["w_k"], wts["w_v"],
                                  wts["v_ones"], wts["k_gain"]]
    in_specs = [pl.BlockSpec((1, tm, d), lambda b, i: (b, i, 0)),
                pl.BlockSpec((1, d), lambda b, i: (0, 0)), vec, vec] + [full(a) for a in consts]
    args = [x, g.reshape(1, d), scale1p, shift] + consts
    if rope:
        in_specs += [pl.BlockSpec((tm, HEAD_PAD), lambda b, i: (i, 0))] * 3
        args += list(tabs)
    widths = [kw, kw, kw, RW_SHIFT, d]
    return pl.pallas_call(
        functools.partial(_even_in_kernel, rope),
        grid=(bsz, n_rows // tm),
        in_specs=in_specs,
        out_specs=[pl.BlockSpec((1, tm, w), lambda b, i: (b, i, 0)) for w in widths],
        out_shape=[jax.ShapeDtypeStruct((bsz, n_rows, w), BF16) for w in widths],
        compiler_params=_cparams("parallel", "arbitrary"),
        name="even_in",
    )(*args)


def _attn_kernel(nseg, q_ref, *refs):
    k_refs, v_refs, o_ref = refs[:nseg], refs[nseg:2 * nseg], refs[2 * nseg]
    tq = q_ref.shape[1]
    lane = lax.broadcasted_iota(jnp.int32, (tq, LANES), 1)
    for hp in range(MLA_HEADS // 2):
        psl = slice(2 * hp * HEAD_PAD, (2 * hp + 2) * HEAD_PAD)
        outs = []
        for h in (2 * hp, 2 * hp + 1):
            sl = slice(h * HEAD_PAD, (h + 1) * HEAD_PAD)
            q = q_ref[0, :, sl]
            ss = [lax.dot_general(q, k_ref[0, :, sl], _NT, preferred_element_type=F32)
                  for k_ref in k_refs]
            m = functools.reduce(jnp.maximum, [jnp.max(s, axis=-1, keepdims=True) for s in ss])
            acc = None
            for s, v_ref in zip(ss, v_refs):
                for k0 in range(0, s.shape[1], ATTN_TK):
                    pv = jnp.dot(jnp.exp2(s[:, k0:k0 + ATTN_TK] - m).astype(BF16),
                                 v_ref[0, k0:k0 + ATTN_TK, psl], preferred_element_type=F32)
                    acc = pv if acc is None else pv + acc
            acc = acc[:, :HEAD_PAD] if h % 2 == 0 else acc[:, HEAD_PAD:]
            ones_lane = MLA_V if h % 2 == 0 else 0
            outs.append(acc * (1.0 / acc[:, ones_lane:ones_lane + 1]))
        o_ref[0, :, hp * LANES:(hp + 1) * LANES] = jnp.where(lane < MLA_V, outs[0],
                                                             outs[1]).astype(BF16)


def _attention(q, ks, vs):
    bsz, n_q, qw = q.shape
    tq = min(ATTN_TQ, n_q)
    nseg = len(ks)
    in_specs = [pl.BlockSpec((1, tq, qw), lambda b, i: (b, i, 0))]
    in_specs += [pl.BlockSpec((1,) + k.shape[1:], lambda b, i: (b, 0, 0)) for k in ks]
    in_specs += [pl.BlockSpec((1,) + v.shape[1:], lambda b, i: (b, 0, 0)) for v in vs]
    return pl.pallas_call(
        functools.partial(_attn_kernel, nseg),
        grid=(bsz, n_q // tq),
        in_specs=in_specs,
        out_specs=pl.BlockSpec((1, tq, MLA_WIDTH), lambda b, i: (b, i, 0)),
        out_shape=jax.ShapeDtypeStruct((bsz, n_q, MLA_WIDTH), BF16),
        compiler_params=_cparams("parallel", "arbitrary"),
        name="mla_attention",
    )(q, *ks, *vs)


def _rw_prep_kernel(p_ref, prev_ref, next_ref, mup_ref, mun_ref, w0_ref, wup_ref, a0_ref, aup_ref,
                    kk_ref, ka_ref, rk_ref, seg_ref,
                    r_out, v_out, kkn_out, bonus_out, lw0_out, lw1_out, b0_out, b1_out,
                    kd0_out, kd1_out):
    pb = p_ref[0]
    p = pb.astype(F32)
    prev, nxt = _shifted_rows(pb, prev_ref, next_ref, on_mxu=False)
    mu_p, mu_n = mup_ref[...], mun_ref[...]
    ps = p * (1.0 - mu_p - mu_n) + prev * mu_p + nxt * mu_n
    w = RW_WIDTH
    r, k, v = ps[:, :w], ps[:, w:2 * w], ps[:, 2 * w:3 * w]
    wd = jnp.tanh(ps[:, 3 * w:3 * w + 2 * RW_LORA_W])
    ad = ps[:, 3 * w + 2 * RW_LORA_W:]
    seg = seg_ref[...]
    kq = k * kk_ref[...]
    kk = kq * lax.rsqrt(jnp.maximum(_mm(kq * kq, seg, mode="lhs2"), 1e-24))
    r_out[0] = r.astype(BF16)
    v_out[0] = v.astype(BF16)
    kkn_out[0] = kk.astype(BF16)
    bonus_in = jnp.zeros_like(r)
    for d, (lw_out, b_out, kd_out) in enumerate(((lw0_out, b0_out, kd0_out),
                                                  (lw1_out, b1_out, kd1_out))):
        z = w0_ref[d:d + 1, :] + _mm(wd, wup_ref[d], mode="x3")
        lw_out[0] = -math.exp(-0.5) * jax.nn.sigmoid(z)
        a = jax.nn.sigmoid(a0_ref[d:d + 1, :] + _mm(ad, aup_ref[d], mode="x3"))
        kd = k * (1.0 + (a - 1.0) * ka_ref[...])
        b_out[0] = (kk * a).astype(BF16)
        kd_out[0] = kd.astype(BF16)
        bonus_in = bonus_in + r * kd * rk_ref[d:d + 1, :]
    bonus_out[0] = (_mm(bonus_in, seg, mode="lhs2") * v).astype(BF16)


def _rw_prep(p_rw, prm):
    bsz, n_rows, width = p_rw.shape
    tm = min(ROW_TILE, n_rows)
    main, prev, nxt = _halo_specs(tm, width, n_rows)
    full = lambda a: pl.BlockSpec(a.shape, lambda b, i: (0,) * a.ndim)
    consts = [prm["mu_prev"], prm["mu_next"], prm["w0"], prm["w_up"], prm["a0"], prm["a_up"],
              prm["k_k"], prm["k_a"], prm["r_k"], prm["seg"]]
    out_spec = pl.BlockSpec((1, tm, RW_WIDTH), lambda b, i: (b, i, 0))
    shape = lambda dt: jax.ShapeDtypeStruct((bsz, n_rows, RW_WIDTH), dt)
    outs = pl.pallas_call(
        _rw_prep_kernel,
        grid=(bsz, n_rows // tm),
        in_specs=[main, prev, nxt] + [full(a) for a in consts],
        out_specs=[out_spec] * 10,
        out_shape=[shape(BF16)] * 4 + [shape(F32)] * 2 + [shape(BF16)] * 4,
        compiler_params=_cparams("parallel", "arbitrary"),
        name="rwkv_prep",
    )(p_rw, p_rw, p_rw, *consts)
    r, v, kk, bonus, lw0, lw1, b0, b1, kd0, kd1 = outs
    return dict(r=r, v=v, kk=kk, bonus=bonus, lw=(lw0, lw1), b=(b0, b1), kd=(kd0, kd1))


def _rwkv_scan_kernel(rf_ref, vf_ref, kkf_ref, lw0_ref, b0_ref, kd0_ref,
                      rb_ref, vb_ref, kkb_ref, lw1_ref, b1_ref, kd1_ref, s0_ref,
                      of_ref, ob_ref, sfin_ref, st_ref):
    j = pl.program_id(1)

    @pl.when(j == 0)
    def _():
        st_ref[...] = s0_ref[0]

    c = rf_ref.shape[1]
    c2 = 2 * c
    dir_refs = ((rf_ref, vf_ref, kkf_ref, lw0_ref, b0_ref, kd0_ref, of_ref),
                (rb_ref, vb_ref, kkb_ref, lw1_ref, b1_ref, kd1_ref, ob_ref))
    ti = lax.broadcasted_iota(jnp.int32, (c, c), 0)
    si = lax.broadcasted_iota(jnp.int32, (c, c), 1)
    t2 = lax.broadcasted_iota(jnp.int32, (c2, c2), 0)
    s2 = lax.broadcasted_iota(jnp.int32, (c2, c2), 1)
    same = jnp.where((t2 >= c) == (s2 >= c), 1, 0)
    eye2 = jnp.where(t2 == s2, 1.0, 0.0)
    head0 = lax.broadcasted_iota(jnp.int32, (c, RW_PAIR), 1) < RW_HEAD
    masks = []
    for rev in (False, True):
        tri = jnp.where((si >= ti) if rev else (si <= ti), 1.0, 0.0)
        before = jnp.where((s2 > t2) if rev else (s2 < t2), same, 0) == 1
        upto = jnp.where((s2 >= t2) if rev else (s2 <= t2), same, 0) == 1
        masks.append((tri, before, upto))

    def stack(t):
        return jnp.concatenate([jnp.where(head0, t, 0.0), jnp.where(head0, 0.0, t)], axis=0)

    def unstack(t):
        return t[:c] + t[c:]

    mm = functools.partial(_mm, mode="bf16")
    items = [(d, p) for d in range(2) for p in range(N_PAIRS)]
    pair_sl = lambda p: slice(p * RW_PAIR, (p + 1) * RW_PAIR)

    lams = [_mm(masks[d][0], dir_refs[d][3][0], mode="f32") for d in range(2)]

    ops = []
    for d, p in items:
        r_ref, v_ref, kk_ref, lw_ref, b_ref, kd_ref, _ = dir_refs[d]
        sl = pair_sl(p)
        lam = lams[d][:, sl]
        lam_tot = lam[0:1] if d == 1 else lam[c - 1:c]
        e_neg = jnp.exp(-lam)
        e_tail = jnp.exp(lam_tot - lam)
        ld = lambda ref: ref[0, :, sl].astype(F32)
        b, kd = ld(b_ref), ld(kd_ref)
        ops.append(dict(
            at2=stack(-ld(kk_ref) * jnp.exp(lam - lw_ref[0, :, sl])),
            rt2=stack(ld(r_ref) * jnp.exp(lam)),
            bh2=stack(b * e_neg), kh2=stack(kd * e_neg),
            bt2=stack(b * e_tail), kt2=stack(kd * e_tail),
            v2=stack(ld(v_ref)),
            decay_tot=jnp.where(eye2 == 1.0, jnp.exp(lam_tot), 0.0)))

    for (d, p), o in zip(items, ops):
        _, before, upto = masks[d]
        x = mm(jnp.concatenate([o["at2"], o["rt2"]], axis=0),
               jnp.concatenate([o["bh2"], o["kh2"]], axis=0), _NT)
        o["a_ab"] = jnp.where(before, x[:c2, :c2], 0.0)
        o["a_ak"] = jnp.where(before, x[:c2, c2:], 0.0)
        o["a_rb"] = jnp.where(upto, x[c2:, :c2], 0.0)
        o["a_rk"] = jnp.where(upto, x[c2:, c2:], 0.0)

    for o in ops:
        o["t_inv"] = eye2 + o["a_ab"]
        o["pw"] = mm(o["a_ab"], o["a_ab"])
        o["w"] = mm(o["a_ak"], o["v2"])
    for _ in range(int(math.log2(c)) - 2):
        for o in ops:
            y = mm(jnp.concatenate([o["pw"], o["t_inv"]], axis=0), o["pw"])
            o["pw"] = y[:c2]
            o["t_inv"] = o["t_inv"] + y[c2:]
    for o in ops:
        o["t_inv"] = o["t_inv"] + mm(o["t_inv"], o["pw"])

    for o in ops:
        o["z"] = mm(o["t_inv"], jnp.concatenate([o["at2"], o["w"]], axis=1))
    for (d, p), o in zip(items, ops):
        y1 = mm(o["a_rb"], o["z"])
        y2 = mm(o["bt2"].T, o["z"])
        y3 = mm(jnp.concatenate([o["a_rk"], o["kt2"].T], axis=0), o["v2"])
        rp = unstack(o["rt2"] + y1[:, :c2])
        o0 = unstack(y1[:, c2:] + y3[:c2])
        m_bd = o["decay_tot"] + y2[:, :c2]
        n_bd = y2[:, c2:] + y3[c2:]
        rows = slice(d * RW_PAIR, (d + 1) * RW_PAIR)
        sl = pair_sl(p)
        state = st_ref[rows, sl]
        dir_refs[d][6][0, :, sl] = (mm(rp, state) + o0).astype(BF16)
        st_ref[rows, sl] = mm(m_bd, state) + n_bd

    @pl.when(j == pl.num_programs(1) - 1)
    def _():
        sfin_ref[0] = st_ref[...]


def _rwkv_scan(st, s0):
    bsz, n_rows, w = st["r"].shape
    nch = n_rows // CHUNK
    fwd = pl.BlockSpec((1, CHUNK, w), lambda b, j: (b, j, 0))
    bwd = pl.BlockSpec((1, CHUNK, w), lambda b, j: (b, nch - 1 - j, 0))
    sspec = pl.BlockSpec((1, 2 * RW_PAIR, w), lambda b, j: (b, 0, 0))
    row_shape = jax.ShapeDtypeStruct((bsz, n_rows, w), BF16)
    return pl.pallas_call(
        _rwkv_scan_kernel,
        grid=(bsz, nch),
        in_specs=[fwd] * 6 + [bwd] * 6 + [sspec],
        out_specs=[fwd, bwd, sspec],
        out_shape=[row_shape, row_shape, jax.ShapeDtypeStruct((bsz, 2 * RW_PAIR, w), F32)],
        scratch_shapes=[pltpu.VMEM((2 * RW_PAIR, w), F32)],
        compiler_params=_cparams("parallel", "arbitrary"),
        name="rwkv_scan",
    )(st["r"], st["v"], st["kk"], st["lw"][0], st["b"][0], st["kd"][0],
      st["r"], st["v"], st["kk"], st["lw"][1], st["b"][1], st["kd"][1], s0)


def _even_out_kernel(x_ref, gate_ref, om_ref, of_ref, ob_ref, bonus_ref, g_ref, lnw_ref, lnb_ref,
                     seg_ref, w_ref, o_ref):
    seg = seg_ref[...]
    f32 = lambda ref: ref[0].astype(F32)
    o = f32(of_ref) + f32(ob_ref)
    mu = _mm(o, seg, mode="lhs2") * (1.0 / RW_HEAD)
    dlt = o - mu
    var = _mm(dlt * dlt, seg, mode="lhs2") * (1.0 / RW_HEAD)
    o_rw = dlt * lax.rsqrt(var + RW_GN_EPS) * lnw_ref[...] + lnb_ref[...] + f32(bonus_ref)
    g = f32(g_ref)
    z_m = (f32(om_ref) * _silu(g[:, :MLA_WIDTH])).astype(BF16)
    z_r = (o_rw * _silu(g[:, MLA_WIDTH:])).astype(BF16)
    y = (jnp.dot(z_m, w_ref[:MLA_WIDTH, :], preferred_element_type=F32)
         + jnp.dot(z_r, w_ref[MLA_WIDTH:, :], preferred_element_type=F32))
    o_ref[0] = x_ref[0] + gate_ref[0] * y


def _even_out(x, gate, o_mla, o_f, o_b, bonus, g, ln_w, ln_b, seg, w_out):
    bsz, n_rows, d = x.shape
    tm = min(ROW_TILE, n_rows)
    rows = lambda width: pl.BlockSpec((1, tm, width), lambda b, i: (b, i, 0))
    full = lambda a: pl.BlockSpec(a.shape, lambda b, i: (0,) * a.ndim)
    return pl.pallas_call(
        _even_out_kernel,
        grid=(bsz, n_rows // tm),
        in_specs=[rows(d), pl.BlockSpec((1, 1, d), lambda b, i: (b, 0, 0)),
                  rows(MLA_WIDTH), rows(RW_WIDTH), rows(RW_WIDTH), rows(RW_WIDTH), rows(d),
                  full(ln_w), full(ln_b), full(seg), full(w_out)],
        out_specs=rows(d),
        out_shape=jax.ShapeDtypeStruct(x.shape, F32),
        compiler_params=_cparams("parallel", "arbitrary"),
        name="even_out",
    )(x, gate, o_mla, o_f, o_b, bonus, g, ln_w, ln_b, seg, w_out)


def _conv3_kernel(u_ref, prev_ref, next_ref, w_ref, b_ref, ve_ref, vo_ref, x0_ref, v_scr):
    ub = u_ref[0]
    u = ub.astype(F32)
    prev, nxt = _shifted_rows(ub, prev_ref, next_ref, on_mxu=True)
    cv = prev * w_ref[0:1, :] + u * w_ref[1:2, :] + nxt * w_ref[2:3, :] + b_ref[...]
    hw = HY_WIDTH
    x0_ref[0] = cv[:, :hw].astype(BF16)
    v_even, v_odd = _split_rows(cv[:, 2 * hw:] * cv[:, hw:2 * hw], v_scr)
    ve_ref[0] = v_even.astype(BF16)
    vo_ref[0] = v_odd.astype(BF16)


def _conv3(u, conv_w, conv_b):
    bsz, n_rows, width = u.shape
    tm = min(ROW_TILE, n_rows)
    main, prev, nxt = _halo_specs(tm, width, n_rows)
    half_spec = pl.BlockSpec((1, tm // 2, HY_WIDTH), lambda b, i: (b, i, 0))
    half_shape = jax.ShapeDtypeStruct((bsz, n_rows // 2, HY_WIDTH), BF16)
    return pl.pallas_call(
        _conv3_kernel,
        grid=(bsz, n_rows // tm),
        in_specs=[main, prev, nxt,
                  pl.BlockSpec(conv_w.shape, lambda b, i: (0, 0)),
                  pl.BlockSpec((1, width), lambda b, i: (0, 0))],
        out_specs=[half_spec, half_spec, pl.BlockSpec((1, tm, HY_WIDTH), lambda b, i: (b, i, 0))],
        out_shape=[half_shape, half_shape, jax.ShapeDtypeStruct((bsz, n_rows, HY_WIDTH), BF16)],
        scratch_shapes=[_row_split_scratch(tm, HY_WIDTH)],
        compiler_params=_cparams("parallel", "arbitrary"),
        name="hyena_conv3",
    )(u, u, u, conv_w, conv_b.reshape(1, width))


def _hy_hidden_kernel(w1_ref, b1_ref, w2_ref, b2_ref, freq_ref, h_ref):
    n_pos = h_ref.shape[0]
    pos = lax.broadcasted_iota(jnp.int32, (n_pos, 1), 0).astype(F32)
    lane = lax.broadcasted_iota(jnp.int32, (1, LANES), 1)
    band_idx = jnp.where(lane <= HY_BANDS, lane - 1, lane - 1 - HY_BANDS).astype(F32)
    band = 1e-4 + band_idx * ((HY_BANDS - 1 - 1e-4) / (HY_BANDS - 1))
    ang = pos * (2.0 * math.pi / n_pos) * band
    z = jnp.where(lane == 0, pos / (n_pos - 1),
                  jnp.where(lane <= HY_BANDS, jnp.cos(ang),
                            jnp.where(lane <= 2 * HY_BANDS, -jnp.sin(ang), 0.0)))
    freq = freq_ref[...]
    hdn = jnp.sin(freq * (_mm(z, w1_ref[...], mode="f32") + b1_ref[...]))
    for j in range(HY_INNER):
        hdn = jnp.sin(freq * (_mm(hdn, w2_ref[j], mode="f32") + b2_ref[j]))
    h_ref[...] = hdn


def _hy_filter_kernel(h_ref, w0_ref, w1_ref, dl_ref, fse_ref, fso_ref, fde_ref, fdo_ref, scr):
    n_pos = h_ref.shape[0]
    pos = lax.broadcasted_iota(jnp.int32, (n_pos, 1), 0)
    t = pos.astype(F32) / (n_pos - 1)
    dec = jnp.exp(-t * dl_ref[...])
    hdn = h_ref[...]
    f_fwd = _mm(hdn, w0_ref[...], mode="f32") * dec
    f_bwd = jnp.where(pos == 0, 0.0, _mm(hdn, w1_ref[...], mode="f32") * dec)
    inv = 1.0 / (jnp.sum(jnp.abs(f_fwd), axis=0, keepdims=True)
                 + jnp.sum(jnp.abs(f_bwd), axis=0, keepdims=True))
    fse_ref[...], fso_ref[...] = _split_rows((f_fwd + f_bwd) * inv, scr)
    fde_ref[...], fdo_ref[...] = _split_rows((f_bwd - f_fwd) * inv, scr)


def _hyena_filters(n_pos, f_w1, f_b1, f_w2, f_b2, f_wout, freq, deltas):
    w1 = jnp.zeros((LANES, HY_ORDER), F32).at[:HY_EMB].set(f_w1)
    hdn = pl.pallas_call(
        _hy_hidden_kernel,
        out_shape=jax.ShapeDtypeStruct((n_pos, HY_ORDER), F32),
        name="hyena_filter_hidden",
    )(w1, f_b1.reshape(1, HY_ORDER), f_w2, f_b2.reshape(HY_INNER, 1, HY_ORDER),
      freq.reshape(1, HY_ORDER))
    tn = 256
    cspec = pl.BlockSpec((HY_ORDER, tn), lambda j: (0, j))
    ospec = pl.BlockSpec((n_pos // 2, tn), lambda j: (0, j))
    oshape = jax.ShapeDtypeStruct((n_pos // 2, HY_WIDTH), F32)
    return pl.pallas_call(
        _hy_filter_kernel,
        grid=(HY_WIDTH // tn,),
        in_specs=[pl.BlockSpec((n_pos, HY_ORDER), lambda j: (0, 0)), cspec, cspec,
                  pl.BlockSpec((1, tn), lambda j: (0, j))],
        out_specs=[ospec] * 4,
        out_shape=[oshape] * 4,
        scratch_shapes=[_row_split_scratch(n_pos, tn)],
        compiler_params=_cparams("arbitrary"),
        name="hyena_filter",
    )(hdn, f_wout[:, :HY_WIDTH], f_wout[:, HY_WIDTH:], deltas)


def _dft_tables_kernel(ec_ref, es_ref, oc_ref, os_ref, oct_ref, ost_ref):
    tr, kq = ec_ref.shape
    n = 4 * kq
    r = lax.broadcasted_iota(jnp.int32, (tr, kq), 0) + pl.program_id(0) * tr
    c = lax.broadcasted_iota(jnp.int32, (tr, kq), 1)

    def cos_sin(phase):
        ph = phase & (n - 1)
        ph = jnp.where(ph >= n // 2, ph - n, ph)
        ang = ph.astype(F32) * (2.0 * math.pi / n)
        return jnp.cos(ang).astype(BF16), jnp.sin(ang).astype(BF16)

    ec_ref[...], es_ref[...] = cos_sin(2 * r * c)
    oc_ref[...], os_ref[...] = cos_sin(r * (2 * c + 1))
    oct_ref[...], ost_ref[...] = cos_sin(c * (2 * r + 1))


def _dft_tables(n_pos):
    kq = n_pos // 2
    tr = min(256, kq)
    spec = pl.BlockSpec((tr, kq), lambda i: (i, 0))
    shape = jax.ShapeDtypeStruct((kq, kq), BF16)
    return pl.pallas_call(
        _dft_tables_kernel,
        grid=(kq // tr,),
        out_specs=[spec] * 6,
        out_shape=[shape] * 6,
        compiler_params=_cparams("arbitrary"),
        name="dft_tables",
    )()


def _alt_sign(n_pos):
    pos = lax.broadcasted_iota(jnp.int32, (n_pos, 1), 0)
    return jnp.where((pos & 1) == 0, 1.0, -1.0), pos


def _spectrum_kernel(fse_ref, fso_ref, fde_ref, fdo_ref, ec_ref, es_ref, oc_ref, os_ref,
                     hare_ref, haim_ref, hbre_ref, hbim_ref, hk_ref):
    kq = fse_ref.shape[0]
    alt, pos = _alt_sign(kq)

    def mm2(tab_ref, f):
        fh = f.astype(BF16)
        fl = (f - fh.astype(F32)).astype(BF16)
        return (jnp.dot(tab_ref[...], fh, preferred_element_type=F32)
                + jnp.dot(tab_ref[...], fl, preferred_element_type=F32))

    fse, fdo = fse_ref[...], fdo_ref[...]
    ce, co = mm2(ec_ref, fse), mm2(oc_ref, fso_ref[...])
    se, so = mm2(es_ref, fde_ref[...]), mm2(os_ref, fdo)
    scale = 0.5 / kq
    scale_re = jnp.where(pos == 0, 0.5 * scale, scale)
    hare_ref[...] = (ce + co) * scale_re
    hbre_ref[...] = (ce - co) * scale_re
    haim_ref[...] = (se + so) * scale
    hbim_ref[...] = (so - se) * scale
    hk_ref[0:1, :] = jnp.sum(fse * alt, axis=0, keepdims=True) * scale
    hk_ref[1:2, :] = jnp.sum(fdo * alt, axis=0, keepdims=True) * scale


def _table_specs(tabs, nd):
    imap = (lambda j: (0, 0)) if nd == 1 else (lambda b, j: (0, 0))
    return [pl.BlockSpec(t.shape, imap, pipeline_mode=pl.Buffered(1)) for t in tabs]


def _spectrum(filters, tabs):
    kq, width = filters[0].shape
    tc = DFT_TC
    half = pl.BlockSpec((kq, tc), lambda j: (0, j))
    return pl.pallas_call(
        _spectrum_kernel,
        grid=(width // tc,),
        in_specs=[half] * 4 + _table_specs(tabs[:4], 1),
        out_specs=[half] * 4 + [pl.BlockSpec((2, tc), lambda j: (0, j))],
        out_shape=[jax.ShapeDtypeStruct((kq, width), F32)] * 4
        + [jax.ShapeDtypeStruct((2, width), F32)],
        compiler_params=_cparams("arbitrary"),
        name="hyena_spectrum",
    )(*filters, *tabs[:4])


def _dft_conv_kernel(ve_ref, vo_ref, hare_ref, haim_ref, hbre_ref, hbim_ref, hk_ref,
                     ec_ref, es_ref, oc_ref, os_ref, oct_ref, ost_ref, ye_ref, yo_ref,
                     pe_scr, qe_scr, po_scr, qo_scr):
    kq, tc = ve_ref.shape[1], ve_ref.shape[2]
    tm = min(DFT_TM, kq)
    alt, _ = _alt_sign(kq)
    ve, vo = ve_ref[0], vo_ref[0]
    dot = functools.partial(jnp.dot, preferred_element_type=F32)
    for r0 in range(0, kq, tm):
        rows = slice(r0, r0 + tm)
        ce, co = dot(ec_ref[rows, :], ve), dot(oc_ref[rows, :], vo)
        se, so = dot(es_ref[rows, :], ve), dot(os_ref[rows, :], vo)
        va_re, va_s, vb_re, vb_s = ce + co, se + so, ce - co, so - se
        hare, haim = hare_ref[rows, :], haim_ref[rows, :]
        hbre, hbim = hbre_ref[rows, :], hbim_ref[rows, :]
        ya_re, ya_s = va_re * hare + va_s * haim, va_s * hare - va_re * haim
        yb_re, yb_s = vb_re * hbre + vb_s * hbim, vb_s * hbre - vb_re * hbim
        pe_scr[rows, :] = (ya_re + yb_re).astype(BF16)
        qe_scr[rows, :] = (ya_s - yb_s).astype(BF16)
        po_scr[rows, :] = (ya_re - yb_re).astype(BF16)
        qo_scr[rows, :] = (ya_s + yb_s).astype(BF16)
    vk_re = jnp.sum(ve.astype(F32) * alt, axis=0, keepdims=True)
    vk_s = jnp.sum(vo.astype(F32) * alt, axis=0, keepdims=True)
    hk_re, hk_im = hk_ref[0:1, :], hk_ref[1:2, :]
    yk_re = vk_re * hk_re + vk_s * hk_im
    yk_s = vk_s * hk_re - vk_re * hk_im
    pe, qe, po, qo = pe_scr[...], qe_scr[...], po_scr[...], qo_scr[...]
    for r0 in range(0, kq, tm):
        rows = slice(r0, r0 + tm)
        ye_ref[0, rows, :] = (dot(ec_ref[rows, :], pe) + dot(es_ref[rows, :], qe)
                              + alt[rows] * yk_re).astype(BF16)
        yo_ref[0, rows, :] = (dot(oct_ref[rows, :], po) + dot(ost_ref[rows, :], qo)
                              + alt[rows] * yk_s).astype(BF16)


def _dft_conv(v_even, v_odd, spec, tabs):
    bsz, kq, width = v_even.shape
    tc = DFT_TC
    half = pl.BlockSpec((1, kq, tc), lambda b, j: (b, 0, j))
    col = pl.BlockSpec((kq, tc), lambda b, j: (0, j))
    return pl.pallas_call(
        _dft_conv_kernel,
        grid=(bsz, width // tc),
        in_specs=[half, half, col, col, col, col, pl.BlockSpec((2, tc), lambda b, j: (0, j))]
        + _table_specs(tabs, 2),
        out_specs=[half, half],
        out_shape=[jax.ShapeDtypeStruct(v_even.shape, BF16)] * 2,
        scratch_shapes=[pltpu.VMEM((kq, tc), BF16)] * 4,
        compiler_params=_cparams("parallel", "arbitrary"),
        name="hyena_dft_conv",
    )(v_even, v_odd, *spec, *tabs)


def _hy_out_kernel(x_ref, gate_ref, ye_ref, yo_ref, ve_ref, vo_ref, x0_ref, g_ref, bias_ref, w_ref,
                   o_ref, t_scr):
    f32 = lambda ref: ref[0].astype(F32)
    bias = bias_ref[...]
    t = _merge_rows(f32(ye_ref) + f32(ve_ref) * bias, f32(yo_ref) + f32(vo_ref) * bias, t_scr)
    z = t * f32(x0_ref) * _silu(f32(g_ref))
    o_ref[0] = x_ref[0] + gate_ref[0] * jnp.dot(z.astype(BF16), w_ref[...],
                                                preferred_element_type=F32)


def _hy_out(x, gate, y_even, y_odd, v_even, v_odd, x0, g, bias_d, w_out):
    bsz, n_rows, d = x.shape
    tm = min(ROW_TILE, n_rows)
    rows = pl.BlockSpec((1, tm, d), lambda b, i: (b, i, 0))
    half = pl.BlockSpec((1, tm // 2, d), lambda b, i: (b, i, 0))
    return pl.pallas_call(
        _hy_out_kernel,
        grid=(bsz, n_rows // tm),
        in_specs=[rows, pl.BlockSpec((1, 1, d), lambda b, i: (b, 0, 0)), half, half, half, half,
                  rows, rows, pl.BlockSpec((1, d), lambda b, i: (0, 0)),
                  pl.BlockSpec(w_out.shape, lambda b, i: (0, 0))],
        out_specs=rows,
        out_shape=jax.ShapeDtypeStruct(x.shape, F32),
        scratch_shapes=[_row_split_scratch(tm, d)],
        compiler_params=_cparams("parallel", "arbitrary"),
        name="hyena_out",
    )(x, gate, y_even, y_odd, v_even, v_odd, x0, g, bias_d.reshape(1, d), w_out)


def _rope_perm():
    pairs = np.arange(MLA_ROPE // 2)
    return np.concatenate([np.arange(MLA_NOPE), MLA_NOPE + 2 * pairs, MLA_NOPE + 2 * pairs + 1])


def _rope_tables(n_pos):
    rows = n_pos // GRID_W
    row = jnp.repeat(jnp.arange(rows, dtype=F32), GRID_W)
    col = jnp.tile(jnp.arange(GRID_W, dtype=F32), rows)
    n_freq = MLA_ROPE // 4
    inv = ROPE_BASE ** (-jnp.arange(n_freq, dtype=F32) / n_freq)
    ang = jnp.concatenate([row[:, None] * inv, col[:, None] * inv], axis=-1)
    cos, sin = jnp.cos(ang), jnp.sin(ang)
    half = MLA_ROPE // 2
    ones = jnp.ones((n_pos, MLA_NOPE), F32)
    zeros = jnp.zeros((n_pos, MLA_NOPE), F32)
    pad1 = jnp.ones((n_pos, HEAD_PAD - MLA_QK), F32)
    pad0 = jnp.zeros((n_pos, HEAD_PAD - MLA_QK), F32)
    z16 = jnp.zeros((n_pos, half), F32)
    cos_f = jnp.concatenate([ones, cos, cos, pad1], axis=-1)
    sin_a = jnp.concatenate([zeros, z16, sin, pad0], axis=-1)
    sin_b = jnp.concatenate([zeros, -sin, z16, pad0], axis=-1)
    return cos_f, sin_a, sin_b


def _pad_heads(w, width):
    k, h, _ = w.shape
    return jnp.zeros((k, h, HEAD_PAD), w.dtype).at[:, :, :width].set(w).reshape(k, h * HEAD_PAD)


def _even_weights(e, ev_w_in, ev_w_out, mla_q_a_norm, mla_w_uq, mla_kv_a_norm, mla_w_ukv,
                  mla_q_norm, mla_k_norm, rwkv_mu_prev, rwkv_mu_next, rwkv_w0, rwkv_w_up, rwkv_a0,
                  rwkv_a_up, rwkv_k_k, rwkv_k_a, rwkv_r_k, rwkv_ln_w, rwkv_ln_b):
    perm = _rope_perm()
    w_in = ev_w_in[e]
    d = w_in.shape[0]
    o_dkv = EV_DQ
    o_rw = EV_DQ + EV_DKV
    o_g = o_rw + RW_SHIFT
    w_dkv = jnp.zeros((d, 2 * LANES), F32)
    w_dkv = w_dkv.at[:, :MLA_KV_LORA].set(w_in[:, o_dkv:o_dkv + MLA_KV_LORA])
    rope_cols = o_dkv + MLA_KV_LORA + (perm[MLA_NOPE:] - MLA_NOPE)
    w_dkv = w_dkv.at[:, LANES + MLA_NOPE:LANES + MLA_QK].set(w_in[:, rope_cols])
    proj = [w_in[:, :EV_DQ].astype(BF16), w_dkv.astype(BF16),
            w_in[:, o_rw:o_g].astype(BF16), w_in[:, o_g:].astype(BF16)]
    w_ukv = mla_w_ukv[e]
    pad_gain = lambda g: jnp.zeros((1, HEAD_PAD), F32).at[0, :MLA_QK].set(g[perm])
    seg_id = np.arange(RW_WIDTH) // RW_HEAD
    zero_up = lambda up, dd: jnp.zeros((2 * RW_LORA_W, RW_WIDTH), F32).at[
        dd * RW_LORA_W:(dd + 1) * RW_LORA_W].set(up[dd])
    rw = dict(
        mu_prev=rwkv_mu_prev[e].reshape(1, RW_SHIFT), mu_next=rwkv_mu_next[e].reshape(1, RW_SHIFT),
        w0=rwkv_w0[e], a0=rwkv_a0[e],
        w_up=jnp.stack([zero_up(rwkv_w_up[e], 0), zero_up(rwkv_w_up[e], 1)]),
        a_up=jnp.stack([zero_up(rwkv_a_up[e], 0), zero_up(rwkv_a_up[e], 1)]),
        k_k=rwkv_k_k[e].reshape(1, RW_WIDTH), k_a=rwkv_k_a[e].reshape(1, RW_WIDTH),
        r_k=rwkv_r_k[e].reshape(2, RW_WIDTH),
        seg=jnp.asarray((seg_id[:, None] == seg_id[None, :]).astype(np.float32)),
    )
    head_par = np.arange(MLA_HEADS) % 2
    v_lane = (np.arange(HEAD_PAD)[None, :] >= MLA_V) == (head_par[:, None] == 1)
    v_ones = (~v_lane).astype(np.float32)
    w_vh = w_ukv[:, :, MLA_NOPE:].reshape(MLA_KV_LORA, MLA_HEADS // 2, 2, MLA_V)
    zero_v = jnp.zeros_like(w_vh[:, :, 0])
    w_v = jnp.stack([w_vh[:, :, 0], zero_v, zero_v, w_vh[:, :, 1]], axis=2)
    return dict(
        proj=proj,
        q_a_norm=mla_q_a_norm[e], kv_a_norm=mla_kv_a_norm[e],
        w_uq=_pad_heads(mla_w_uq[e][:, :, perm], MLA_QK).astype(BF16),
        w_k=_pad_heads(w_ukv[:, :, :MLA_NOPE], MLA_NOPE).astype(BF16),
        w_v=w_v.reshape(MLA_KV_LORA, MLA_HEADS * HEAD_PAD).astype(BF16),
        v_ones=jnp.asarray(v_ones.reshape(1, MLA_HEADS * HEAD_PAD)),
        q_gain=pad_gain(mla_q_norm[e]) * (MLA_QK ** -0.5 * math.log2(math.e)),
        k_gain=pad_gain(mla_k_norm[e]),
        rw=rw,
        ln_w=rwkv_ln_w[e].reshape(1, RW_WIDTH), ln_b=rwkv_ln_b[e].reshape(1, RW_WIDTH),
        w_out=ev_w_out[e].astype(BF16),
    )


def _rwkv_branch(st_c, st, want_ctx):
    bsz = st["r"].shape[0]
    zero_state = jnp.zeros((bsz, 2 * RW_PAIR, RW_WIDTH), F32)
    of_c, ob_c, s_c = _rwkv_scan(st_c, zero_state)
    o_f, o_b, _ = _rwkv_scan(st, s_c)
    return [o_f, o_b], ([of_c, ob_c] if want_ctx else None)


def _even_layer(x, xc, mod, mod_c, g_norm, wts, rope_tabs, ctx_out):
    shift, scale1p, gate = mod
    shift_c, scale1p_c, gate_c = mod_c
    q, k, v, p_rw, g = _even_in(x, g_norm, scale1p, shift, wts, rope_tabs)
    qc, kc, vc, pc_rw, gc = _even_in(xc, g_norm, scale1p_c, shift_c, wts, None)
    o_mla = _attention(q, [k, kc], [v, vc])
    st_c = _rw_prep(pc_rw, wts["rw"])
    st = _rw_prep(p_rw, wts["rw"])
    outs, outs_c = _rwkv_branch(st_c, st, ctx_out)
    seg = wts["rw"]["seg"]
    x_new = _even_out(x, gate, o_mla, outs[0], outs[1], st["bonus"], g, wts["ln_w"], wts["ln_b"],
                      seg, wts["w_out"])
    if not ctx_out:
        return x_new, None
    oc_mla = _attention(qc, [kc], [vc])
    xc_new = _even_out(xc, gate_c, oc_mla, outs_c[0], outs_c[1], st_c["bonus"], gc, wts["ln_w"],
                       wts["ln_b"], seg, wts["w_out"])
    return x_new, xc_new


def _hyena_layer(x, mod, g_norm, wts, tables):
    shift, scale1p, gate = mod
    n_pos = x.shape[1]
    u, g = _norm_proj(x, g_norm, scale1p, shift, wts["proj"])
    v_even, v_odd, x0 = _conv3(u, wts["conv_w"], wts["conv_b"])
    spec = _spectrum(_hyena_filters(n_pos, *wts["filt"]), tables)
    y_even, y_odd = _dft_conv(v_even, v_odd, spec, tables)
    return _hy_out(x, gate, y_even, y_odd, v_even, v_odd, x0, g, wts["bias_d"], wts["w_out"])


def kernel(x, c, ctx, c_ctx, mod_w, mod_b, norm_g, ev_w_in, ev_w_out, mla_q_a_norm, mla_w_uq, mla_kv_a_norm, mla_w_ukv, mla_q_norm, mla_k_norm, rwkv_mu_prev, rwkv_mu_next, rwkv_w0, rwkv_w_up, rwkv_a0, rwkv_a_up, rwkv_k_k, rwkv_k_a, rwkv_r_k, rwkv_ln_w, rwkv_ln_b, od_w_in, od_w_out, hy_conv_w, hy_conv_b, hy_bias_d, hy_f_w1, hy_f_b1, hy_f_w2, hy_f_b2, hy_f_wout, hy_freq):
    bsz, n_lat, d = x.shape
    n_ctx = ctx.shape[1]
    assert n_lat % max(CHUNK, GRID_W) == 0 and n_ctx % CHUNK == 0 and d == D_MODEL
    assert CHUNK == RW_HEAD and 2 * CHUNK == RW_PAIR

    n_rows = -(-(bsz + 1) // 16) * 16
    cvec = jnp.zeros((n_rows, d), F32).at[:bsz].set(c).at[bsz].set(c_ctx)
    mods = _modulation(cvec, mod_w, mod_b)

    def split_mod(i, lo, hi, reps):
        m = mods[i, lo:hi]
        m = jnp.broadcast_to(m, (reps, 3 * d)) if hi - lo == 1 else m
        m = m[:, None, :]
        return m[..., :d], 1.0 + m[..., d:2 * d], m[..., 2 * d:]

    rope_tabs = _rope_tables(n_lat)
    deltas = jnp.abs(jnp.linspace(math.log(HY_TARGET) / HY_FAST_DECAY,
                                  math.log(HY_TARGET) / HY_SLOW_DECAY, HY_WIDTH,
                                  dtype=F32)).reshape(1, HY_WIDTH)
    tables = {n_lat: _dft_tables(n_lat)}

    xc = ctx
    for i in range(DEPTH):
        ctx_needed_later = any(j > i and j % 2 == 0 for j in range(DEPTH))
        mod = split_mod(i, 0, bsz, bsz)
        mod_c = split_mod(i, bsz, bsz + 1, bsz)
        if i % 2 == 0:
            wts = _even_weights(i // 2, ev_w_in, ev_w_out, mla_q_a_norm, mla_w_uq, mla_kv_a_norm,
                                mla_w_ukv, mla_q_norm, mla_k_norm, rwkv_mu_prev, rwkv_mu_next,
                                rwkv_w0, rwkv_w_up, rwkv_a0, rwkv_a_up, rwkv_k_k, rwkv_k_a,
                                rwkv_r_k, rwkv_ln_w, rwkv_ln_b)
            x, xc_new = _even_layer(x, xc, mod, mod_c, norm_g[i], wts, rope_tabs, ctx_needed_later)
            xc = xc_new if ctx_needed_later else xc
        else:
            o = i // 2
            w_in = od_w_in[o]
            wts = dict(
                proj=[w_in[:, :3 * HY_WIDTH].astype(BF16), w_in[:, 3 * HY_WIDTH:].astype(BF16)],
                conv_w=hy_conv_w[o], conv_b=hy_conv_b[o], bias_d=hy_bias_d[o],
                filt=(hy_f_w1[o], hy_f_b1[o], hy_f_w2[o], hy_f_b2[o], hy_f_wout[o], hy_freq[o],
                      deltas),
                w_out=od_w_out[o].astype(BF16),
            )
            if ctx_needed_later:
                if n_ctx not in tables:
                    tables[n_ctx] = _dft_tables(n_ctx)
                xc = _hyena_layer(xc, mod_c, norm_g[i], wts, tables[n_ctx])
            x = _hyena_layer(x, mod, norm_g[i], wts, tables[n_lat])
    return x
```

```python
import functools
import math

import numpy as np
import jax
import jax.numpy as jnp
from jax import lax
from jax.experimental import pallas as pl
from jax.experimental.pallas import tpu as pltpu

F32 = jnp.float32
BF16 = jnp.bfloat16
HIGHEST = lax.Precision.HIGHEST

D_MODEL = 1024
DEPTH = 4
GRID_W = 64
NORM_EPS = 1e-6
MLA_HEADS = 8
MLA_NOPE = 64
MLA_ROPE = 32
MLA_QK = MLA_NOPE + MLA_ROPE
MLA_V = 64
MLA_Q_LORA = 256
MLA_KV_LORA = 128
MLA_WIDTH = MLA_HEADS * MLA_V
ROPE_BASE = 10000.0
RW_HEADS = 8
RW_HEAD = 64
RW_WIDTH = RW_HEADS * RW_HEAD
RW_LORA_W = 64
RW_LORA_A = 64
RW_SHIFT = 3 * RW_WIDTH + 2 * RW_LORA_W + 2 * RW_LORA_A
RW_GN_EPS = 64e-5
EV_DQ = MLA_Q_LORA
EV_DKV = MLA_KV_LORA + MLA_ROPE
HY_WIDTH = D_MODEL
HY_ORDER = 64
HY_BANDS = 16
HY_EMB = 1 + 2 * HY_BANDS
HY_INNER = 2
HY_FAST_DECAY = 0.3
HY_SLOW_DECAY = 1.5
HY_TARGET = 1e-2

LANES = 128
HEAD_PAD = 128
RW_PAIR = 2 * RW_HEAD
N_PAIRS = RW_WIDTH // RW_PAIR
CHUNK = 64
SCAN_CHUNKS = 2
ROW_TILE = 256
ATTN_TQ = 256
ATTN_TK = 256
ATTN_SAFE_BITS = 40.0
DFT_TC = 256
DFT_TM = 256
HALO_ROWS = 16
EVEN_IN_TILE = 256
SUB_ROWS = 128
VMEM_LIMIT = 56 * 1024 * 1024

_NN = (((1,), (0,)), ((), ()))
_NT = (((1,), (1,)), ((), ()))


def _mm(a, b, dn=_NN, mode="bf16"):
    if mode == "f32":
        return lax.dot_general(a, b, dn, precision=HIGHEST, preferred_element_type=F32)
    dg = functools.partial(lax.dot_general, dimension_numbers=dn, preferred_element_type=F32)
    ah = a.astype(BF16)
    bh = b.astype(BF16)
    if mode == "bf16":
        return dg(ah, bh)
    al = (a - ah.astype(F32)).astype(BF16)
    if mode == "lhs2":
        return dg(ah, bh) + dg(al, bh)
    bl = (b - bh.astype(F32)).astype(BF16)
    return dg(ah, bh) + (dg(ah, bl) + dg(al, bh))


def _cparams(*sem):
    return pltpu.CompilerParams(dimension_semantics=sem, vmem_limit_bytes=VMEM_LIMIT)


def _silu(t):
    return t * jax.nn.sigmoid(t)


def _shifted_rows(pb, prev_ref, next_ref, on_mxu):
    tm = pb.shape[0]
    i = pl.program_id(1)
    last = pl.num_programs(1) - 1
    prev_row = jnp.where(i > 0, prev_ref[0, HALO_ROWS - 1:HALO_ROWS, :].astype(F32), 0.0)
    next_row = jnp.where(i < last, next_ref[0, 0:1, :].astype(F32), 0.0)
    if on_mxu:
        out_row = lax.broadcasted_iota(jnp.int32, (2 * tm, tm), 0)
        src_row = lax.broadcasted_iota(jnp.int32, (2 * tm, tm), 1)
        want = jnp.where(out_row < tm, out_row - 1, out_row - tm + 1)
        both = jnp.dot(jnp.where(src_row == want, 1.0, 0.0).astype(BF16), pb,
                       preferred_element_type=F32)
        down, up = both[:tm], both[tm:]
    else:
        p = pb.astype(F32)
        down, up = pltpu.roll(p, 1, axis=0), pltpu.roll(p, tm - 1, axis=0)
    sub = lax.broadcasted_iota(jnp.int32, (8, 1), 0)
    prev = jnp.concatenate([jnp.where(sub == 0, prev_row, down[:8]), down[8:]], axis=0)
    nxt = jnp.concatenate([up[:tm - 8], jnp.where(sub == 7, next_row, up[tm - 8:])], axis=0)
    return prev, nxt


def _split_rows(val, scr):
    rows, width = val.shape
    half = rows // 2
    for j in range(width // LANES):
        scr[j] = val[:, j * LANES:(j + 1) * LANES]
    pick = lambda start: jnp.concatenate(
        [scr[j, pl.ds(start, half, stride=2), :] for j in range(width // LANES)], axis=1)
    return pick(0), pick(1)


def _merge_rows(even, odd, scr):
    half, width = even.shape
    for j in range(width // LANES):
        cols = slice(j * LANES, (j + 1) * LANES)
        scr[j, pl.ds(0, half, stride=2), :] = even[:, cols]
        scr[j, pl.ds(1, half, stride=2), :] = odd[:, cols]
    return jnp.concatenate([scr[j] for j in range(width // LANES)], axis=1)


def _row_split_scratch(rows, width):
    return pltpu.VMEM((width // LANES, rows, LANES), F32)


def _halo_specs(tm, width, n_rows):
    th = tm // HALO_ROWS
    last_h = n_rows // HALO_ROWS - 1
    main = pl.BlockSpec((1, tm, width), lambda b, i: (b, i, 0))
    prev = pl.BlockSpec((1, HALO_ROWS, width), lambda b, i: (b, jnp.maximum(i * th - 1, 0), 0))
    nxt = pl.BlockSpec((1, HALO_ROWS, width),
                       lambda b, i: (b, jnp.minimum((i + 1) * th, last_h), 0))
    return main, prev, nxt


def _mod_kernel(c_ref, w_ref, b_ref, o_ref):
    o_ref[0] = _mm(_silu(c_ref[...]), w_ref[0], mode="x3") + b_ref[0]


def _modulation(cvec, mod_w, mod_b):
    rows, d = cvec.shape
    n = mod_w.shape[-1]
    tn = 1024
    return pl.pallas_call(
        _mod_kernel,
        grid=(DEPTH, n // tn),
        in_specs=[pl.BlockSpec((rows, d), lambda i, j: (0, 0)),
                  pl.BlockSpec((1, d, tn), lambda i, j: (i, 0, j)),
                  pl.BlockSpec((1, 1, tn), lambda i, j: (i, 0, j))],
        out_specs=pl.BlockSpec((1, rows, tn), lambda i, j: (i, 0, j)),
        out_shape=jax.ShapeDtypeStruct((DEPTH, rows, n), F32),
        compiler_params=_cparams("arbitrary", "arbitrary"),
        name="modulation",
    )(cvec, mod_w, mod_b.reshape(DEPTH, 1, n))


def _norm_proj_kernel(nw, x_ref, g_ref, sc_ref, sh_ref, *refs):
    x = x_ref[0]
    h = x * lax.rsqrt(jnp.mean(x * x, axis=-1, keepdims=True) + NORM_EPS) * g_ref[...]
    hb = (h * sc_ref[0] + sh_ref[0]).astype(BF16)
    for w_ref, o_ref in zip(refs[:nw], refs[nw:]):
        o_ref[0] = jnp.dot(hb, w_ref[...], preferred_element_type=F32).astype(BF16)


def _norm_proj(x, g, scale1p, shift, weights):
    bsz, n_rows, d = x.shape
    tm = min(ROW_TILE, n_rows)
    vec = pl.BlockSpec((1, 1, d), lambda b, i: (b, 0, 0))
    in_specs = [pl.BlockSpec((1, tm, d), lambda b, i: (b, i, 0)),
                pl.BlockSpec((1, d), lambda b, i: (0, 0)), vec, vec]
    in_specs += [pl.BlockSpec(w.shape, lambda b, i: (0, 0)) for w in weights]
    return pl.pallas_call(
        functools.partial(_norm_proj_kernel, len(weights)),
        grid=(bsz, n_rows // tm),
        in_specs=in_specs,
        out_specs=[pl.BlockSpec((1, tm, w.shape[1]), lambda b, i: (b, i, 0)) for w in weights],
        out_shape=[jax.ShapeDtypeStruct((bsz, n_rows, w.shape[1]), BF16) for w in weights],
        compiler_params=_cparams("parallel", "arbitrary"),
        name="norm_proj",
    )(x, g.reshape(1, d), scale1p, shift, *weights)


def _head_norm_rope(t, gain, tabs):
    ms = jnp.sum(t * t, axis=-1, keepdims=True) * (1.0 / MLA_QK)
    t = t * lax.rsqrt(ms + NORM_EPS) * gain
    if tabs is not None:
        cos_f, sin_a, sin_b = tabs
        t = (t * cos_f + pltpu.roll(t, MLA_ROPE // 2, axis=1) * sin_a
             + pltpu.roll(t, HEAD_PAD - MLA_ROPE // 2, axis=1) * sin_b)
    return t


def _even_in_kernel(rope, x_ref, g_ref, sc_ref, sh_ref, wdq_ref, wdkv_ref, wrw_ref, wg_ref,
                    qan_ref, wuq_ref, qgn_ref, kan_ref, wk_ref, wv_ref, vone_ref, kgn_ref, *refs):
    q_ref, k_ref, v_ref, prw_ref, go_ref = refs[-5:]
    q_gain, k_gain = qgn_ref[...], kgn_ref[...]

    def rms(t, gain_ref):
        return t * lax.rsqrt(jnp.mean(t * t, axis=-1, keepdims=True) + NORM_EPS) * gain_ref[...]

    tm = x_ref.shape[1]
    sub = min(SUB_ROWS, tm)
    for r0 in range(0, tm, sub):
        rows = slice(r0, r0 + sub)
        tabs = tuple(r[rows, :] for r in refs[:3]) if rope else None
        hb = (rms(x_ref[0, rows, :], g_ref) * sc_ref[0] + sh_ref[0]).astype(BF16)
        prw_ref[0, rows, :] = jnp.dot(hb, wrw_ref[...], preferred_element_type=F32).astype(BF16)
        go_ref[0, rows, :] = jnp.dot(hb, wg_ref[...], preferred_element_type=F32).astype(BF16)
        p_dq = jnp.dot(hb, wdq_ref[...], preferred_element_type=F32)
        q = jnp.dot(rms(p_dq, qan_ref).astype(BF16), wuq_ref[...], preferred_element_type=F32)
        p_dkv = jnp.dot(hb, wdkv_ref[...], preferred_element_type=F32)
        k_rope = p_dkv[:, MLA_KV_LORA:]
        ab = rms(p_dkv[:, :MLA_KV_LORA], kan_ref).astype(BF16)
        k_nope = jnp.dot(ab, wk_ref[...], preferred_element_type=F32)
        v_ref[0, rows, :] = (jnp.dot(ab, wv_ref[...], preferred_element_type=F32)
                             + vone_ref[...]).astype(BF16)
        for hd in range(MLA_HEADS):
            sl = slice(hd * HEAD_PAD, (hd + 1) * HEAD_PAD)
            q_ref[0, rows, sl] = _head_norm_rope(q[:, sl], q_gain, tabs).astype(BF16)
            k_ref[0, rows, sl] = _head_norm_rope(k_nope[:, sl] + k_rope, k_gain,
                                                 tabs).astype(BF16)


def _even_in(x, g, scale1p, shift, wts, tabs):
    bsz, n_rows, d = x.shape
    tm = min(EVEN_IN_TILE, n_rows)
    rope = tabs is not None
    kw = MLA_HEADS * HEAD_PAD
    vec = pl.BlockSpec((1, 1, d), lambda b, i: (b, 0, 0))
    full = lambda a: pl.BlockSpec(a.shape, lambda b, i: (0,) * a.ndim)
    consts = list(wts["proj"]) + [wts["q_a_norm"].reshape(1, MLA_Q_LORA), wts["w_uq"], wts["q_gain"],
                                  wts["kv_a_norm"].reshape(1, MLA_KV_LORA), wts["w_k"], wts["w_v"],
                                  wts["v_ones"], wts["k_gain"]]
    in_specs = [pl.BlockSpec((1, tm, d), lambda b, i: (b, i, 0)),
                pl.BlockSpec((1, d), lambda b, i: (0, 0)), vec, vec] + [full(a) for a in consts]
    args = [x, g.reshape(1, d), scale1p, shift] + consts
    if rope:
        in_specs += [pl.BlockSpec((tm, HEAD_PAD), lambda b, i: (i, 0))] * 3
        args += list(tabs)
    widths = [kw, kw, kw, RW_SHIFT, d]
    return pl.pallas_call(
        functools.partial(_even_in_kernel, rope),
        grid=(bsz, n_rows // tm),
        in_specs=in_specs,
        out_specs=[pl.BlockSpec((1, tm, w), lambda b, i: (b, i, 0)) for w in widths],
        out_shape=[jax.ShapeDtypeStruct((bsz, n_rows, w), BF16) for w in widths],
        compiler_params=_cparams("parallel", "arbitrary"),
        name="even_in",
    )(*args)


def _attn_body(q_ref, k_refs, v_refs, o_ref, bound):
    tq = q_ref.shape[1]
    lane = lax.broadcasted_iota(jnp.int32, (tq, LANES), 1)
    for hp in range(MLA_HEADS // 2):
        psl = slice(2 * hp * HEAD_PAD, (2 * hp + 2) * HEAD_PAD)
        outs = []
        for h in (2 * hp, 2 * hp + 1):
            sl = slice(h * HEAD_PAD, (h + 1) * HEAD_PAD)
            q = q_ref[0, :, sl]
            score = lambda k_ref, k0, k1: lax.dot_general(q, k_ref[0, k0:k1, sl], _NT,
                                                          preferred_element_type=F32)
            if bound is None:
                ss = [score(k_ref, 0, k_ref.shape[1]) for k_ref in k_refs]
                m = functools.reduce(jnp.maximum, [jnp.max(s, axis=-1, keepdims=True) for s in ss])
            acc = None
            for seg, (k_ref, v_ref) in enumerate(zip(k_refs, v_refs)):
                for k0 in range(0, k_ref.shape[1], ATTN_TK):
                    k1 = k0 + ATTN_TK
                    p = (jnp.exp2(ss[seg][:, k0:k1] - m) if bound is None
                         else jnp.exp2(score(k_ref, k0, k1) - bound))
                    pv = jnp.dot(p.astype(BF16), v_ref[0, k0:k1, psl], preferred_element_type=F32)
                    acc = pv if acc is None else pv + acc
            acc = acc[:, :HEAD_PAD] if h % 2 == 0 else acc[:, HEAD_PAD:]
            ones_lane = MLA_V if h % 2 == 0 else 0
            outs.append(acc * (1.0 / acc[:, ones_lane:ones_lane + 1]))
        o_ref[0, :, hp * LANES:(hp + 1) * LANES] = jnp.where(lane < MLA_V, outs[0],
                                                             outs[1]).astype(BF16)


def _attn_kernel(nseg, bound_ref, q_ref, *refs):
    k_refs, v_refs, o_ref = refs[:nseg], refs[nseg:2 * nseg], refs[2 * nseg]
    bound = bound_ref[0]

    @pl.when(bound <= ATTN_SAFE_BITS)
    def _():
        _attn_body(q_ref, k_refs, v_refs, o_ref, bound)

    @pl.when(bound > ATTN_SAFE_BITS)
    def _():
        _attn_body(q_ref, k_refs, v_refs, o_ref, None)


def _attention(q, ks, vs, bound):
    bsz, n_q, qw = q.shape
    tq = min(ATTN_TQ, n_q)
    nseg = len(ks)
    in_specs = [pl.BlockSpec(memory_space=pltpu.SMEM),
                pl.BlockSpec((1, tq, qw), lambda b, i: (b, i, 0))]
    in_specs += [pl.BlockSpec((1,) + k.shape[1:], lambda b, i: (b, 0, 0)) for k in ks]
    in_specs += [pl.BlockSpec((1,) + v.shape[1:], lambda b, i: (b, 0, 0)) for v in vs]
    return pl.pallas_call(
        functools.partial(_attn_kernel, nseg),
        grid=(bsz, n_q // tq),
        in_specs=in_specs,
        out_specs=pl.BlockSpec((1, tq, MLA_WIDTH), lambda b, i: (b, i, 0)),
        out_shape=jax.ShapeDtypeStruct((bsz, n_q, MLA_WIDTH), BF16),
        compiler_params=_cparams("parallel", "arbitrary"),
        name="mla_attention",
    )(bound, q, *ks, *vs)


def _rw_prep_kernel(p_ref, prev_ref, next_ref, mup_ref, mun_ref, w0_ref, wup_ref, a0_ref, aup_ref,
                    kk_ref, ka_ref, rk_ref, seg_ref,
                    r_out, v_out, kkn_out, bonus_out, lw0_out, lw1_out, b0_out, b1_out,
                    kd0_out, kd1_out):
    pb = p_ref[0]
    p = pb.astype(F32)
    prev, nxt = _shifted_rows(pb, prev_ref, next_ref, on_mxu=False)
    mu_p, mu_n = mup_ref[...], mun_ref[...]
    ps = p * (1.0 - mu_p - mu_n) + prev * mu_p + nxt * mu_n
    w = RW_WIDTH
    r, k, v = ps[:, :w], ps[:, w:2 * w], ps[:, 2 * w:3 * w]
    wd = jnp.tanh(ps[:, 3 * w:3 * w + 2 * RW_LORA_W])
    ad = ps[:, 3 * w + 2 * RW_LORA_W:]
    seg = seg_ref[...]
    kq = k * kk_ref[...]
    kk = kq * lax.rsqrt(jnp.maximum(_mm(kq * kq, seg, mode="lhs2"), 1e-24))
    r_out[0] = r.astype(BF16)
    v_out[0] = v.astype(BF16)
    kkn_out[0] = kk.astype(BF16)
    bonus_in = jnp.zeros_like(r)
    for d, (lw_out, b_out, kd_out) in enumerate(((lw0_out, b0_out, kd0_out),
                                                  (lw1_out, b1_out, kd1_out))):
        z = w0_ref[d:d + 1, :] + _mm(wd, wup_ref[d], mode="x3")
        lw_out[0] = -math.exp(-0.5) * jax.nn.sigmoid(z)
        a = jax.nn.sigmoid(a0_ref[d:d + 1, :] + _mm(ad, aup_ref[d], mode="x3"))
        kd = k * (1.0 + (a - 1.0) * ka_ref[...])
        b_out[0] = (kk * a).astype(BF16)
        kd_out[0] = kd.astype(BF16)
        bonus_in = bonus_in + r * kd * rk_ref[d:d + 1, :]
    bonus_out[0] = (_mm(bonus_in, seg, mode="lhs2") * v).astype(BF16)


def _rw_prep(p_rw, prm):
    bsz, n_rows, width = p_rw.shape
    tm = min(ROW_TILE, n_rows)
    main, prev, nxt = _halo_specs(tm, width, n_rows)
    full = lambda a: pl.BlockSpec(a.shape, lambda b, i: (0,) * a.ndim)
    consts = [prm["mu_prev"], prm["mu_next"], prm["w0"], prm["w_up"], prm["a0"], prm["a_up"],
              prm["k_k"], prm["k_a"], prm["r_k"], prm["seg"]]
    out_spec = pl.BlockSpec((1, tm, RW_WIDTH), lambda b, i: (b, i, 0))
    shape = lambda dt: jax.ShapeDtypeStruct((bsz, n_rows, RW_WIDTH), dt)
    outs = pl.pallas_call(
        _rw_prep_kernel,
        grid=(bsz, n_rows // tm),
        in_specs=[main, prev, nxt] + [full(a) for a in consts],
        out_specs=[out_spec] * 10,
        out_shape=[shape(BF16)] * 4 + [shape(F32)] * 2 + [shape(BF16)] * 4,
        compiler_params=_cparams("parallel", "arbitrary"),
        name="rwkv_prep",
    )(p_rw, p_rw, p_rw, *consts)
    r, v, kk, bonus, lw0, lw1, b0, b1, kd0, kd1 = outs
    return dict(r=r, v=v, kk=kk, bonus=bonus, lw=(lw0, lw1), b=(b0, b1), kd=(kd0, kd1))


def _rwkv_scan_kernel(rf_ref, vf_ref, kkf_ref, lw0_ref, b0_ref, kd0_ref,
                      rb_ref, vb_ref, kkb_ref, lw1_ref, b1_ref, kd1_ref, s0_ref,
                      of_ref, ob_ref, sfin_ref, st_ref):
    j = pl.program_id(1)

    @pl.when(j == 0)
    def _():
        st_ref[...] = s0_ref[0]

    c = CHUNK
    c2 = 2 * c
    nsub = rf_ref.shape[1] // c
    dir_refs = ((rf_ref, vf_ref, kkf_ref, lw0_ref, b0_ref, kd0_ref, of_ref),
                (rb_ref, vb_ref, kkb_ref, lw1_ref, b1_ref, kd1_ref, ob_ref))
    ti = lax.broadcasted_iota(jnp.int32, (c, c), 0)
    si = lax.broadcasted_iota(jnp.int32, (c, c), 1)
    t2 = lax.broadcasted_iota(jnp.int32, (c2, c2), 0)
    s2 = lax.broadcasted_iota(jnp.int32, (c2, c2), 1)
    same = jnp.where((t2 >= c) == (s2 >= c), 1, 0)
    eye2 = jnp.where(t2 == s2, 1.0, 0.0)
    head0 = lax.broadcasted_iota(jnp.int32, (c, RW_PAIR), 1) < RW_HEAD
    masks = []
    for rev in (False, True):
        tri = jnp.where((si >= ti) if rev else (si <= ti), 1.0, 0.0)
        before = jnp.where((s2 > t2) if rev else (s2 < t2), same, 0) == 1
        upto = jnp.where((s2 >= t2) if rev else (s2 <= t2), same, 0) == 1
        masks.append((tri, before, upto))

    def stack(t):
        return jnp.concatenate([jnp.where(head0, t, 0.0), jnp.where(head0, 0.0, t)], axis=0)

    def unstack(t):
        return t[:c] + t[c:]

    mm = functools.partial(_mm, mode="bf16")
    items = [(d, p, s) for d in range(2) for p in range(N_PAIRS) for s in range(nsub)]
    pair_sl = lambda p: slice(p * RW_PAIR, (p + 1) * RW_PAIR)
    chunk_rows = lambda s: slice(s * c, (s + 1) * c)

    lams = [[_mm(masks[d][0], dir_refs[d][3][0, chunk_rows(s), :], mode="f32") for s in range(nsub)]
            for d in range(2)]

    ops = []
    for d, p, s in items:
        r_ref, v_ref, kk_ref, lw_ref, b_ref, kd_ref, _ = dir_refs[d]
        sl, rs = pair_sl(p), chunk_rows(s)
        lam = lams[d][s][:, sl]
        lam_tot = lam[0:1] if d == 1 else lam[c - 1:c]
        e_neg = jnp.exp(-lam)
        e_tail = jnp.exp(lam_tot - lam)
        ld = lambda ref: ref[0, rs, sl].astype(F32)
        b, kd = ld(b_ref), ld(kd_ref)
        ops.append(dict(
            at2=stack(-ld(kk_ref) * jnp.exp(lam - lw_ref[0, rs, sl])),
            rt2=stack(ld(r_ref) * jnp.exp(lam)),
            bh2=stack(b * e_neg), kh2=stack(kd * e_neg),
            bt2=stack(b * e_tail), kt2=stack(kd * e_tail),
            v2=stack(ld(v_ref)),
            decay_tot=jnp.where(eye2 == 1.0, jnp.exp(lam_tot), 0.0)))

    for (d, p, s), o in zip(items, ops):
        _, before, upto = masks[d]
        x = mm(jnp.concatenate([o["at2"], o["rt2"]], axis=0),
               jnp.concatenate([o["bh2"], o["kh2"]], axis=0), _NT)
        o["a_ab"] = jnp.where(before, x[:c2, :c2], 0.0)
        o["a_ak"] = jnp.where(before, x[:c2, c2:], 0.0)
        o["a_rb"] = jnp.where(upto, x[c2:, :c2], 0.0)
        o["a_rk"] = jnp.where(upto, x[c2:, c2:], 0.0)

    for o in ops:
        o["t_inv"] = eye2 + o["a_ab"]
        o["pw"] = mm(o["a_ab"], o["a_ab"])
        o["w"] = mm(o["a_ak"], o["v2"])
    for _ in range(int(math.log2(c)) - 2):
        for o in ops:
            y = mm(jnp.concatenate([o["pw"], o["t_inv"]], axis=0), o["pw"])
            o["pw"] = y[:c2]
            o["t_inv"] = o["t_inv"] + y[c2:]
    for o in ops:
        o["t_inv"] = o["t_inv"] + mm(o["t_inv"], o["pw"])

    for o in ops:
        o["z"] = mm(o["t_inv"], jnp.concatenate([o["at2"], o["w"]], axis=1))
    for o in ops:
        y1 = mm(o["a_rb"], o["z"])
        y2 = mm(o["bt2"].T, o["z"])
        y3 = mm(jnp.concatenate([o["a_rk"], o["kt2"].T], axis=0), o["v2"])
        o["rp"] = unstack(o["rt2"] + y1[:, :c2])
        o["o0"] = unstack(y1[:, c2:] + y3[:c2])
        o["m_bd"] = o["decay_tot"] + y2[:, :c2]
        o["n_bd"] = y2[:, c2:] + y3[c2:]

    by_item = dict(zip(items, ops))
    for d in range(2):
        rows = slice(d * RW_PAIR, (d + 1) * RW_PAIR)
        for p in range(N_PAIRS):
            sl = pair_sl(p)
            state = st_ref[rows, sl]
            for s in (range(nsub) if d == 0 else reversed(range(nsub))):
                o = by_item[(d, p, s)]
                dir_refs[d][6][0, chunk_rows(s), sl] = (mm(o["rp"], state) + o["o0"]).astype(BF16)
                state = mm(o["m_bd"], state) + o["n_bd"]
            st_ref[rows, sl] = state

    @pl.when(j == pl.num_programs(1) - 1)
    def _():
        sfin_ref[0] = st_ref[...]


def _rwkv_scan(st, s0):
    bsz, n_rows, w = st["r"].shape
    rows = SCAN_CHUNKS * CHUNK
    nblk = n_rows // rows
    fwd = pl.BlockSpec((1, rows, w), lambda b, j: (b, j, 0))
    bwd = pl.BlockSpec((1, rows, w), lambda b, j: (b, nblk - 1 - j, 0))
    sspec = pl.BlockSpec((1, 2 * RW_PAIR, w), lambda b, j: (b, 0, 0))
    row_shape = jax.ShapeDtypeStruct((bsz, n_rows, w), BF16)
    return pl.pallas_call(
        _rwkv_scan_kernel,
        grid=(bsz, nblk),
        in_specs=[fwd] * 6 + [bwd] * 6 + [sspec],
        out_specs=[fwd, bwd, sspec],
        out_shape=[row_shape, row_shape, jax.ShapeDtypeStruct((bsz, 2 * RW_PAIR, w), F32)],
        scratch_shapes=[pltpu.VMEM((2 * RW_PAIR, w), F32)],
        compiler_params=_cparams("parallel", "arbitrary"),
        name="rwkv_scan",
    )(st["r"], st["v"], st["kk"], st["lw"][0], st["b"][0], st["kd"][0],
      st["r"], st["v"], st["kk"], st["lw"][1], st["b"][1], st["kd"][1], s0)


def _even_out_kernel(x_ref, gate_ref, om_ref, of_ref, ob_ref, bonus_ref, g_ref, lnw_ref, lnb_ref,
                     seg_ref, w_ref, o_ref):
    seg = seg_ref[...]
    f32 = lambda ref: ref[0].astype(F32)
    o = f32(of_ref) + f32(ob_ref)
    mu = _mm(o, seg, mode="lhs2") * (1.0 / RW_HEAD)
    dlt = o - mu
    var = _mm(dlt * dlt, seg, mode="lhs2") * (1.0 / RW_HEAD)
    o_rw = dlt * lax.rsqrt(var + RW_GN_EPS) * lnw_ref[...] + lnb_ref[...] + f32(bonus_ref)
    g = f32(g_ref)
    z_m = (f32(om_ref) * _silu(g[:, :MLA_WIDTH])).astype(BF16)
    z_r = (o_rw * _silu(g[:, MLA_WIDTH:])).astype(BF16)
    y = (jnp.dot(z_m, w_ref[:MLA_WIDTH, :], preferred_element_type=F32)
         + jnp.dot(z_r, w_ref[MLA_WIDTH:, :], preferred_element_type=F32))
    o_ref[0] = x_ref[0] + gate_ref[0] * y


def _even_out(x, gate, o_mla, o_f, o_b, bonus, g, ln_w, ln_b, seg, w_out):
    bsz, n_rows, d = x.shape
    tm = min(ROW_TILE, n_rows)
    rows = lambda width: pl.BlockSpec((1, tm, width), lambda b, i: (b, i, 0))
    full = lambda a: pl.BlockSpec(a.shape, lambda b, i: (0,) * a.ndim)
    return pl.pallas_call(
        _even_out_kernel,
        grid=(bsz, n_rows // tm),
        in_specs=[rows(d), pl.BlockSpec((1, 1, d), lambda b, i: (b, 0, 0)),
                  rows(MLA_WIDTH), rows(RW_WIDTH), rows(RW_WIDTH), rows(RW_WIDTH), rows(d),
                  full(ln_w), full(ln_b), full(seg), full(w_out)],
        out_specs=rows(d),
        out_shape=jax.ShapeDtypeStruct(x.shape, F32),
        compiler_params=_cparams("parallel", "arbitrary"),
        name="even_out",
    )(x, gate, o_mla, o_f, o_b, bonus, g, ln_w, ln_b, seg, w_out)


def _conv3_kernel(u_ref, prev_ref, next_ref, w_ref, b_ref, ve_ref, vo_ref, x0_ref, v_scr):
    ub = u_ref[0]
    u = ub.astype(F32)
    prev, nxt = _shifted_rows(ub, prev_ref, next_ref, on_mxu=True)
    cv = prev * w_ref[0:1, :] + u * w_ref[1:2, :] + nxt * w_ref[2:3, :] + b_ref[...]
    hw = HY_WIDTH
    x0_ref[0] = cv[:, :hw].astype(BF16)
    v_even, v_odd = _split_rows(cv[:, 2 * hw:] * cv[:, hw:2 * hw], v_scr)
    ve_ref[0] = v_even.astype(BF16)
    vo_ref[0] = v_odd.astype(BF16)


def _conv3(u, conv_w, conv_b):
    bsz, n_rows, width = u.shape
    tm = min(ROW_TILE, n_rows)
    main, prev, nxt = _halo_specs(tm, width, n_rows)
    half_spec = pl.BlockSpec((1, tm // 2, HY_WIDTH), lambda b, i: (b, i, 0))
    half_shape = jax.ShapeDtypeStruct((bsz, n_rows // 2, HY_WIDTH), BF16)
    return pl.pallas_call(
        _conv3_kernel,
        grid=(bsz, n_rows // tm),
        in_specs=[main, prev, nxt,
                  pl.BlockSpec(conv_w.shape, lambda b, i: (0, 0)),
                  pl.BlockSpec((1, width), lambda b, i: (0, 0))],
        out_specs=[half_spec, half_spec, pl.BlockSpec((1, tm, HY_WIDTH), lambda b, i: (b, i, 0))],
        out_shape=[half_shape, half_shape, jax.ShapeDtypeStruct((bsz, n_rows, HY_WIDTH), BF16)],
        scratch_shapes=[_row_split_scratch(tm, HY_WIDTH)],
        compiler_params=_cparams("parallel", "arbitrary"),
        name="hyena_conv3",
    )(u, u, u, conv_w, conv_b.reshape(1, width))


def _hy_hidden_kernel(w1_ref, b1_ref, w2_ref, b2_ref, freq_ref, h_ref):
    n_pos = h_ref.shape[0]
    pos = lax.broadcasted_iota(jnp.int32, (n_pos, 1), 0).astype(F32)
    lane = lax.broadcasted_iota(jnp.int32, (1, LANES), 1)
    band_idx = jnp.where(lane <= HY_BANDS, lane - 1, lane - 1 - HY_BANDS).astype(F32)
    band = 1e-4 + band_idx * ((HY_BANDS - 1 - 1e-4) / (HY_BANDS - 1))
    ang = pos * (2.0 * math.pi / n_pos) * band
    z = jnp.where(lane == 0, pos / (n_pos - 1),
                  jnp.where(lane <= HY_BANDS, jnp.cos(ang),
                            jnp.where(lane <= 2 * HY_BANDS, -jnp.sin(ang), 0.0)))
    freq = freq_ref[...]
    hdn = jnp.sin(freq * (_mm(z, w1_ref[...], mode="f32") + b1_ref[...]))
    for j in range(HY_INNER):
        hdn = jnp.sin(freq * (_mm(hdn, w2_ref[j], mode="f32") + b2_ref[j]))
    h_ref[...] = hdn


def _hy_filter_kernel(h_ref, w0_ref, w1_ref, dl_ref, fse_ref, fso_ref, fde_ref, fdo_ref, scr):
    n_pos = h_ref.shape[0]
    pos = lax.broadcasted_iota(jnp.int32, (n_pos, 1), 0)
    t = pos.astype(F32) / (n_pos - 1)
    dec = jnp.exp(-t * dl_ref[...])
    hdn = h_ref[...]
    f_fwd = _mm(hdn, w0_ref[...], mode="f32") * dec
    f_bwd = jnp.where(pos == 0, 0.0, _mm(hdn, w1_ref[...], mode="f32") * dec)
    inv = 1.0 / (jnp.sum(jnp.abs(f_fwd), axis=0, keepdims=True)
                 + jnp.sum(jnp.abs(f_bwd), axis=0, keepdims=True))
    fse_ref[...], fso_ref[...] = _split_rows((f_fwd + f_bwd) * inv, scr)
    fde_ref[...], fdo_ref[...] = _split_rows((f_bwd - f_fwd) * inv, scr)


def _hyena_filters(n_pos, f_w1, f_b1, f_w2, f_b2, f_wout, freq, deltas):
    w1 = jnp.zeros((LANES, HY_ORDER), F32).at[:HY_EMB].set(f_w1)
    hdn = pl.pallas_call(
        _hy_hidden_kernel,
        out_shape=jax.ShapeDtypeStruct((n_pos, HY_ORDER), F32),
        name="hyena_filter_hidden",
    )(w1, f_b1.reshape(1, HY_ORDER), f_w2, f_b2.reshape(HY_INNER, 1, HY_ORDER),
      freq.reshape(1, HY_ORDER))
    tn = 256
    cspec = pl.BlockSpec((HY_ORDER, tn), lambda j: (0, j))
    ospec = pl.BlockSpec((n_pos // 2, tn), lambda j: (0, j))
    oshape = jax.ShapeDtypeStruct((n_pos // 2, HY_WIDTH), F32)
    return pl.pallas_call(
        _hy_filter_kernel,
        grid=(HY_WIDTH // tn,),
        in_specs=[pl.BlockSpec((n_pos, HY_ORDER), lambda j: (0, 0)), cspec, cspec,
                  pl.BlockSpec((1, tn), lambda j: (0, j))],
        out_specs=[ospec] * 4,
        out_shape=[oshape] * 4,
        scratch_shapes=[_row_split_scratch(n_pos, tn)],
        compiler_params=_cparams("arbitrary"),
        name="hyena_filter",
    )(hdn, f_wout[:, :HY_WIDTH], f_wout[:, HY_WIDTH:], deltas)


def _dft_tables_kernel(ec_ref, es_ref, oc_ref, os_ref, oct_ref, ost_ref):
    tr, kq = ec_ref.shape
    n = 4 * kq
    r = lax.broadcasted_iota(jnp.int32, (tr, kq), 0) + pl.program_id(0) * tr
    c = lax.broadcasted_iota(jnp.int32, (tr, kq), 1)

    def cos_sin(phase):
        ph = phase & (n - 1)
        ph = jnp.where(ph >= n // 2, ph - n, ph)
        ang = ph.astype(F32) * (2.0 * math.pi / n)
        return jnp.cos(ang).astype(BF16), jnp.sin(ang).astype(BF16)

    ec_ref[...], es_ref[...] = cos_sin(2 * r * c)
    oc_ref[...], os_ref[...] = cos_sin(r * (2 * c + 1))
    oct_ref[...], ost_ref[...] = cos_sin(c * (2 * r + 1))


def _dft_tables(n_pos):
    kq = n_pos // 2
    tr = min(256, kq)
    spec = pl.BlockSpec((tr, kq), lambda i: (i, 0))
    shape = jax.ShapeDtypeStruct((kq, kq), BF16)
    return pl.pallas_call(
        _dft_tables_kernel,
        grid=(kq // tr,),
        out_specs=[spec] * 6,
        out_shape=[shape] * 6,
        compiler_params=_cparams("arbitrary"),
        name="dft_tables",
    )()


def _alt_sign(n_pos):
    pos = lax.broadcasted_iota(jnp.int32, (n_pos, 1), 0)
    return jnp.where((pos & 1) == 0, 1.0, -1.0), pos


def _spectrum_kernel(fse_ref, fso_ref, fde_ref, fdo_ref, ec_ref, es_ref, oc_ref, os_ref,
                     hare_ref, haim_ref, hbre_ref, hbim_ref, hk_ref):
    kq = fse_ref.shape[0]
    alt, pos = _alt_sign(kq)

    def mm2(tab_ref, f):
        fh = f.astype(BF16)
        fl = (f - fh.astype(F32)).astype(BF16)
        return (jnp.dot(tab_ref[...], fh, preferred_element_type=F32)
                + jnp.dot(tab_ref[...], fl, preferred_element_type=F32))

    fse, fdo = fse_ref[...], fdo_ref[...]
    ce, co = mm2(ec_ref, fse), mm2(oc_ref, fso_ref[...])
    se, so = mm2(es_ref, fde_ref[...]), mm2(os_ref, fdo)
    scale = 0.5 / kq
    scale_re = jnp.where(pos == 0, 0.5 * scale, scale)
    hare_ref[...] = (ce + co) * scale_re
    hbre_ref[...] = (ce - co) * scale_re
    haim_ref[...] = (se + so) * scale
    hbim_ref[...] = (so - se) * scale
    hk_ref[0:1, :] = jnp.sum(fse * alt, axis=0, keepdims=True) * scale
    hk_ref[1:2, :] = jnp.sum(fdo * alt, axis=0, keepdims=True) * scale


def _table_specs(tabs, nd):
    imap = (lambda j: (0, 0)) if nd == 1 else (lambda b, j: (0, 0))
    return [pl.BlockSpec(t.shape, imap, pipeline_mode=pl.Buffered(1)) for t in tabs]


def _spectrum(filters, tabs):
    kq, width = filters[0].shape
    tc = DFT_TC
    half = pl.BlockSpec((kq, tc), lambda j: (0, j))
    return pl.pallas_call(
        _spectrum_kernel,
        grid=(width // tc,),
        in_specs=[half] * 4 + _table_specs(tabs[:4], 1),
        out_specs=[half] * 4 + [pl.BlockSpec((2, tc), lambda j: (0, j))],
        out_shape=[jax.ShapeDtypeStruct((kq, width), F32)] * 4
        + [jax.ShapeDtypeStruct((2, width), F32)],
        compiler_params=_cparams("arbitrary"),
        name="hyena_spectrum",
    )(*filters, *tabs[:4])


def _dft_conv_kernel(ve_ref, vo_ref, hare_ref, haim_ref, hbre_ref, hbim_ref, hk_ref,
                     ec_ref, es_ref, oc_ref, os_ref, oct_ref, ost_ref, ye_ref, yo_ref,
                     pe_scr, qe_scr, po_scr, qo_scr):
    kq, tc = ve_ref.shape[1], ve_ref.shape[2]
    tm = min(DFT_TM, kq)
    alt, _ = _alt_sign(kq)
    ve, vo = ve_ref[0], vo_ref[0]
    dot = functools.partial(jnp.dot, preferred_element_type=F32)
    for r0 in range(0, kq, tm):
        rows = slice(r0, r0 + tm)
        ce, co = dot(ec_ref[rows, :], ve), dot(oc_ref[rows, :], vo)
        se, so = dot(es_ref[rows, :], ve), dot(os_ref[rows, :], vo)
        va_re, va_s, vb_re, vb_s = ce + co, se + so, ce - co, so - se
        hare, haim = hare_ref[rows, :], haim_ref[rows, :]
        hbre, hbim = hbre_ref[rows, :], hbim_ref[rows, :]
        ya_re, ya_s = va_re * hare + va_s * haim, va_s * hare - va_re * haim
        yb_re, yb_s = vb_re * hbre + vb_s * hbim, vb_s * hbre - vb_re * hbim
        pe_scr[rows, :] = (ya_re + yb_re).astype(BF16)
        qe_scr[rows, :] = (ya_s - yb_s).astype(BF16)
        po_scr[rows, :] = (ya_re - yb_re).astype(BF16)
        qo_scr[rows, :] = (ya_s + yb_s).astype(BF16)
    vk_re = jnp.sum(ve.astype(F32) * alt, axis=0, keepdims=True)
    vk_s = jnp.sum(vo.astype(F32) * alt, axis=0, keepdims=True)
    hk_re, hk_im = hk_ref[0:1, :], hk_ref[1:2, :]
    yk_re = vk_re * hk_re + vk_s * hk_im
    yk_s = vk_s * hk_re - vk_re * hk_im
    pe, qe, po, qo = pe_scr[...], qe_scr[...], po_scr[...], qo_scr[...]
    for r0 in range(0, kq, tm):
        rows = slice(r0, r0 + tm)
        ye_ref[0, rows, :] = (dot(ec_ref[rows, :], pe) + dot(es_ref[rows, :], qe)
                              + alt[rows] * yk_re).astype(BF16)
        yo_ref[0, rows, :] = (dot(oct_ref[rows, :], po) + dot(ost_ref[rows, :], qo)
                              + alt[rows] * yk_s).astype(BF16)


def _dft_conv(v_even, v_odd, spec, tabs):
    bsz, kq, width = v_even.shape
    tc = DFT_TC
    half = pl.BlockSpec((1, kq, tc), lambda b, j: (b, 0, j))
    col = pl.BlockSpec((kq, tc), lambda b, j: (0, j))
    return pl.pallas_call(
        _dft_conv_kernel,
        grid=(bsz, width // tc),
        in_specs=[half, half, col, col, col, col, pl.BlockSpec((2, tc), lambda b, j: (0, j))]
        + _table_specs(tabs, 2),
        out_specs=[half, half],
        out_shape=[jax.ShapeDtypeStruct(v_even.shape, BF16)] * 2,
        scratch_shapes=[pltpu.VMEM((kq, tc), BF16)] * 4,
        compiler_params=_cparams("parallel", "arbitrary"),
        name="hyena_dft_conv",
    )(v_even, v_odd, *spec, *tabs)


def _hy_out_kernel(x_ref, gate_ref, ye_ref, yo_ref, ve_ref, vo_ref, x0_ref, g_ref, bias_ref, w_ref,
                   o_ref, t_scr):
    f32 = lambda ref: ref[0].astype(F32)
    bias = bias_ref[...]
    t = _merge_rows(f32(ye_ref) + f32(ve_ref) * bias, f32(yo_ref) + f32(vo_ref) * bias, t_scr)
    z = t * f32(x0_ref) * _silu(f32(g_ref))
    o_ref[0] = x_ref[0] + gate_ref[0] * jnp.dot(z.astype(BF16), w_ref[...],
                                                preferred_element_type=F32)


def _hy_out(x, gate, y_even, y_odd, v_even, v_odd, x0, g, bias_d, w_out):
    bsz, n_rows, d = x.shape
    tm = min(ROW_TILE, n_rows)
    rows = pl.BlockSpec((1, tm, d), lambda b, i: (b, i, 0))
    half = pl.BlockSpec((1, tm // 2, d), lambda b, i: (b, i, 0))
    return pl.pallas_call(
        _hy_out_kernel,
        grid=(bsz, n_rows // tm),
        in_specs=[rows, pl.BlockSpec((1, 1, d), lambda b, i: (b, 0, 0)), half, half, half, half,
                  rows, rows, pl.BlockSpec((1, d), lambda b, i: (0, 0)),
                  pl.BlockSpec(w_out.shape, lambda b, i: (0, 0))],
        out_specs=rows,
        out_shape=jax.ShapeDtypeStruct(x.shape, F32),
        scratch_shapes=[_row_split_scratch(tm, d)],
        compiler_params=_cparams("parallel", "arbitrary"),
        name="hyena_out",
    )(x, gate, y_even, y_odd, v_even, v_odd, x0, g, bias_d.reshape(1, d), w_out)


def _rope_perm():
    pairs = np.arange(MLA_ROPE // 2)
    return np.concatenate([np.arange(MLA_NOPE), MLA_NOPE + 2 * pairs, MLA_NOPE + 2 * pairs + 1])


def _rope_tables(n_pos):
    rows = n_pos // GRID_W
    row = jnp.repeat(jnp.arange(rows, dtype=F32), GRID_W)
    col = jnp.tile(jnp.arange(GRID_W, dtype=F32), rows)
    n_freq = MLA_ROPE // 4
    inv = ROPE_BASE ** (-jnp.arange(n_freq, dtype=F32) / n_freq)
    ang = jnp.concatenate([row[:, None] * inv, col[:, None] * inv], axis=-1)
    cos, sin = jnp.cos(ang), jnp.sin(ang)
    half = MLA_ROPE // 2
    ones = jnp.ones((n_pos, MLA_NOPE), F32)
    zeros = jnp.zeros((n_pos, MLA_NOPE), F32)
    pad1 = jnp.ones((n_pos, HEAD_PAD - MLA_QK), F32)
    pad0 = jnp.zeros((n_pos, HEAD_PAD - MLA_QK), F32)
    z16 = jnp.zeros((n_pos, half), F32)
    cos_f = jnp.concatenate([ones, cos, cos, pad1], axis=-1)
    sin_a = jnp.concatenate([zeros, z16, sin, pad0], axis=-1)
    sin_b = jnp.concatenate([zeros, -sin, z16, pad0], axis=-1)
    return cos_f, sin_a, sin_b


def _pad_heads(w, width):
    k, h, _ = w.shape
    return jnp.zeros((k, h, HEAD_PAD), w.dtype).at[:, :, :width].set(w).reshape(k, h * HEAD_PAD)


def _even_weights(e, ev_w_in, ev_w_out, mla_q_a_norm, mla_w_uq, mla_kv_a_norm, mla_w_ukv,
                  mla_q_norm, mla_k_norm, rwkv_mu_prev, rwkv_mu_next, rwkv_w0, rwkv_w_up, rwkv_a0,
                  rwkv_a_up, rwkv_k_k, rwkv_k_a, rwkv_r_k, rwkv_ln_w, rwkv_ln_b):
    perm = _rope_perm()
    w_in = ev_w_in[e]
    d = w_in.shape[0]
    o_dkv = EV_DQ
    o_rw = EV_DQ + EV_DKV
    o_g = o_rw + RW_SHIFT
    w_dkv = jnp.zeros((d, 2 * LANES), F32)
    w_dkv = w_dkv.at[:, :MLA_KV_LORA].set(w_in[:, o_dkv:o_dkv + MLA_KV_LORA])
    rope_cols = o_dkv + MLA_KV_LORA + (perm[MLA_NOPE:] - MLA_NOPE)
    w_dkv = w_dkv.at[:, LANES + MLA_NOPE:LANES + MLA_QK].set(w_in[:, rope_cols])
    proj = [w_in[:, :EV_DQ].astype(BF16), w_dkv.astype(BF16),
            w_in[:, o_rw:o_g].astype(BF16), w_in[:, o_g:].astype(BF16)]
    w_ukv = mla_w_ukv[e]
    pad_gain = lambda g: jnp.zeros((1, HEAD_PAD), F32).at[0, :MLA_QK].set(g[perm])
    seg_id = np.arange(RW_WIDTH) // RW_HEAD
    zero_up = lambda up, dd: jnp.zeros((2 * RW_LORA_W, RW_WIDTH), F32).at[
        dd * RW_LORA_W:(dd + 1) * RW_LORA_W].set(up[dd])
    rw = dict(
        mu_prev=rwkv_mu_prev[e].reshape(1, RW_SHIFT), mu_next=rwkv_mu_next[e].reshape(1, RW_SHIFT),
        w0=rwkv_w0[e], a0=rwkv_a0[e],
        w_up=jnp.stack([zero_up(rwkv_w_up[e], 0), zero_up(rwkv_w_up[e], 1)]),
        a_up=jnp.stack([zero_up(rwkv_a_up[e], 0), zero_up(rwkv_a_up[e], 1)]),
        k_k=rwkv_k_k[e].reshape(1, RW_WIDTH), k_a=rwkv_k_a[e].reshape(1, RW_WIDTH),
        r_k=rwkv_r_k[e].reshape(2, RW_WIDTH),
        seg=jnp.asarray((seg_id[:, None] == seg_id[None, :]).astype(np.float32)),
    )
    head_par = np.arange(MLA_HEADS) % 2
    v_lane = (np.arange(HEAD_PAD)[None, :] >= MLA_V) == (head_par[:, None] == 1)
    v_ones = (~v_lane).astype(np.float32)
    w_vh = w_ukv[:, :, MLA_NOPE:].reshape(MLA_KV_LORA, MLA_HEADS // 2, 2, MLA_V)
    zero_v = jnp.zeros_like(w_vh[:, :, 0])
    w_v = jnp.stack([w_vh[:, :, 0], zero_v, zero_v, w_vh[:, :, 1]], axis=2)
    return dict(
        proj=proj,
        q_a_norm=mla_q_a_norm[e], kv_a_norm=mla_kv_a_norm[e],
        w_uq=_pad_heads(mla_w_uq[e][:, :, perm], MLA_QK).astype(BF16),
        w_k=_pad_heads(w_ukv[:, :, :MLA_NOPE], MLA_NOPE).astype(BF16),
        w_v=w_v.reshape(MLA_KV_LORA, MLA_HEADS * HEAD_PAD).astype(BF16),
        v_ones=jnp.asarray(v_ones.reshape(1, MLA_HEADS * HEAD_PAD)),
        q_gain=pad_gain(mla_q_norm[e]) * (MLA_QK ** -0.5 * math.log2(math.e)),
        k_gain=pad_gain(mla_k_norm[e]),
        attn_bound=(1.01 * MLA_QK * (MLA_QK ** -0.5 * math.log2(math.e))
                    * jnp.max(jnp.abs(mla_q_norm[e])) * jnp.max(jnp.abs(mla_k_norm[e]))
                    ).reshape(1).astype(F32),
        rw=rw,
        ln_w=rwkv_ln_w[e].reshape(1, RW_WIDTH), ln_b=rwkv_ln_b[e].reshape(1, RW_WIDTH),
        w_out=ev_w_out[e].astype(BF16),
    )


def _rwkv_branch(st_c, st, want_ctx):
    bsz = st["r"].shape[0]
    zero_state = jnp.zeros((bsz, 2 * RW_PAIR, RW_WIDTH), F32)
    of_c, ob_c, s_c = _rwkv_scan(st_c, zero_state)
    o_f, o_b, _ = _rwkv_scan(st, s_c)
    return [o_f, o_b], ([of_c, ob_c] if want_ctx else None)


def _even_layer(x, xc, mod, mod_c, g_norm, wts, rope_tabs, ctx_out):
    shift, scale1p, gate = mod
    shift_c, scale1p_c, gate_c = mod_c
    q, k, v, p_rw, g = _even_in(x, g_norm, scale1p, shift, wts, rope_tabs)
    qc, kc, vc, pc_rw, gc = _even_in(xc, g_norm, scale1p_c, shift_c, wts, None)
    o_mla = _attention(q, [k, kc], [v, vc], wts["attn_bound"])
    st_c = _rw_prep(pc_rw, wts["rw"])
    st = _rw_prep(p_rw, wts["rw"])
    outs, outs_c = _rwkv_branch(st_c, st, ctx_out)
    seg = wts["rw"]["seg"]
    x_new = _even_out(x, gate, o_mla, outs[0], outs[1], st["bonus"], g, wts["ln_w"], wts["ln_b"],
                      seg, wts["w_out"])
    if not ctx_out:
        return x_new, None
    oc_mla = _attention(qc, [kc], [vc], wts["attn_bound"])
    xc_new = _even_out(xc, gate_c, oc_mla, outs_c[0], outs_c[1], st_c["bonus"], gc, wts["ln_w"],
                       wts["ln_b"], seg, wts["w_out"])
    return x_new, xc_new


def _hyena_layer(x, mod, g_norm, wts, tables):
    shift, scale1p, gate = mod
    n_pos = x.shape[1]
    u, g = _norm_proj(x, g_norm, scale1p, shift, wts["proj"])
    v_even, v_odd, x0 = _conv3(u, wts["conv_w"], wts["conv_b"])
    spec = _spectrum(_hyena_filters(n_pos, *wts["filt"]), tables)
    y_even, y_odd = _dft_conv(v_even, v_odd, spec, tables)
    return _hy_out(x, gate, y_even, y_odd, v_even, v_odd, x0, g, wts["bias_d"], wts["w_out"])


def kernel(x, c, ctx, c_ctx, mod_w, mod_b, norm_g, ev_w_in, ev_w_out, mla_q_a_norm, mla_w_uq, mla_kv_a_norm, mla_w_ukv, mla_q_norm, mla_k_norm, rwkv_mu_prev, rwkv_mu_next, rwkv_w0, rwkv_w_up, rwkv_a0, rwkv_a_up, rwkv_k_k, rwkv_k_a, rwkv_r_k, rwkv_ln_w, rwkv_ln_b, od_w_in, od_w_out, hy_conv_w, hy_conv_b, hy_bias_d, hy_f_w1, hy_f_b1, hy_f_w2, hy_f_b2, hy_f_wout, hy_freq):
    bsz, n_lat, d = x.shape
    n_ctx = ctx.shape[1]
    scan_rows = SCAN_CHUNKS * CHUNK
    assert n_lat % max(scan_rows, GRID_W) == 0 and n_ctx % scan_rows == 0 and d == D_MODEL
    assert CHUNK == RW_HEAD and 2 * CHUNK == RW_PAIR

    n_rows = -(-(bsz + 1) // 16) * 16
    cvec = jnp.zeros((n_rows, d), F32).at[:bsz].set(c).at[bsz].set(c_ctx)
    mods = _modulation(cvec, mod_w, mod_b)

    def split_mod(i, lo, hi, reps):
        m = mods[i, lo:hi]
        m = jnp.broadcast_to(m, (reps, 3 * d)) if hi - lo == 1 else m
        m = m[:, None, :]
        return m[..., :d], 1.0 + m[..., d:2 * d], m[..., 2 * d:]

    rope_tabs = _rope_tables(n_lat)
    deltas = jnp.abs(jnp.linspace(math.log(HY_TARGET) / HY_FAST_DECAY,
                                  math.log(HY_TARGET) / HY_SLOW_DECAY, HY_WIDTH,
                                  dtype=F32)).reshape(1, HY_WIDTH)
    tables = {n_lat: _dft_tables(n_lat)}

    xc = ctx
    for i in range(DEPTH):
        ctx_needed_later = any(j > i and j % 2 == 0 for j in range(DEPTH))
        mod = split_mod(i, 0, bsz, bsz)
        mod_c = split_mod(i, bsz, bsz + 1, bsz)
        if i % 2 == 0:
            wts = _even_weights(i // 2, ev_w_in, ev_w_out, mla_q_a_norm, mla_w_uq, mla_kv_a_norm,
                                mla_w_ukv, mla_q_norm, mla_k_norm, rwkv_mu_prev, rwkv_mu_next,
                                rwkv_w0, rwkv_w_up, rwkv_a0, rwkv_a_up, rwkv_k_k, rwkv_k_a,
                                rwkv_r_k, rwkv_ln_w, rwkv_ln_b)
            x, xc_new = _even_layer(x, xc, mod, mod_c, norm_g[i], wts, rope_tabs, ctx_needed_later)
            xc = xc_new if ctx_needed_later else xc
        else:
            o = i // 2
            w_in = od_w_in[o]
            wts = dict(
                proj=[w_in[:, :3 * HY_WIDTH].astype(BF16), w_in[:, 3 * HY_WIDTH:].astype(BF16)],
                conv_w=hy_conv_w[o], conv_b=hy_conv_b[o], bias_d=hy_bias_d[o],
                filt=(hy_f_w1[o], hy_f_b1[o], hy_f_w2[o], hy_f_b2[o], hy_f_wout[o], hy_freq[o],
                      deltas),
                w_out=od_w_out[o].astype(BF16),
            )
            if ctx_needed_later:
                if n_ctx not in tables:
                    tables[n_ctx] = _dft_tables(n_ctx)
                xc = _hyena_layer(xc, mod_c, norm_g[i], wts, tables[n_ctx])
            x = _hyena_layer(x, mod, norm_g[i], wts, tables[n_lat])
    return x
```

```python
import functools
import math

import numpy as np
import jax
import jax.numpy as jnp
from jax import lax
from jax.experimental import pallas as pl
from jax.experimental.pallas import tpu as pltpu

F32 = jnp.float32
BF16 = jnp.bfloat16
HIGHEST = lax.Precision.HIGHEST

D_MODEL = 1024
DEPTH = 4
GRID_W = 64
NORM_EPS = 1e-6
MLA_HEADS = 8
MLA_NOPE = 64
MLA_ROPE = 32
MLA_QK = MLA_NOPE + MLA_ROPE
MLA_V = 64
MLA_Q_LORA = 256
MLA_KV_LORA = 128
MLA_WIDTH = MLA_HEADS * MLA_V
ROPE_BASE = 10000.0
RW_HEADS = 8
RW_HEAD = 64
RW_WIDTH = RW_HEADS * RW_HEAD
RW_LORA_W = 64
RW_LORA_A = 64
RW_SHIFT = 3 * RW_WIDTH + 2 * RW_LORA_W + 2 * RW_LORA_A
RW_GN_EPS = 64e-5
EV_DQ = MLA_Q_LORA
EV_DKV = MLA_KV_LORA + MLA_ROPE
HY_WIDTH = D_MODEL
HY_ORDER = 64
HY_BANDS = 16
HY_EMB = 1 + 2 * HY_BANDS
HY_INNER = 2
HY_FAST_DECAY = 0.3
HY_SLOW_DECAY = 1.5
HY_TARGET = 1e-2

LANES = 128
HEAD_PAD = 128
RW_PAIR = 2 * RW_HEAD
N_PAIRS = RW_WIDTH // RW_PAIR
CHUNK = 64
SCAN_CHUNKS = 4
ROW_TILE = 256
ATTN_TQ = 256
ATTN_TK = 256
ATTN_SAFE_BITS = 40.0
DFT_TC = 256
DFT_TM = 256
HALO_ROWS = 16
EVEN_IN_TILE = 256
SUB_ROWS = 128
VMEM_LIMIT = 56 * 1024 * 1024

_NN = (((1,), (0,)), ((), ()))
_NT = (((1,), (1,)), ((), ()))


def _mm(a, b, dn=_NN, mode="bf16"):
    if mode == "f32":
        return lax.dot_general(a, b, dn, precision=HIGHEST, preferred_element_type=F32)
    dg = functools.partial(lax.dot_general, dimension_numbers=dn, preferred_element_type=F32)
    ah = a.astype(BF16)
    bh = b.astype(BF16)
    if mode == "bf16":
        return dg(ah, bh)
    al = (a - ah.astype(F32)).astype(BF16)
    if mode == "lhs2":
        return dg(ah, bh) + dg(al, bh)
    bl = (b - bh.astype(F32)).astype(BF16)
    return dg(ah, bh) + (dg(ah, bl) + dg(al, bh))


def _cparams(*sem):
    return pltpu.CompilerParams(dimension_semantics=sem, vmem_limit_bytes=VMEM_LIMIT)


def _silu(t):
    return t * jax.nn.sigmoid(t)


def _shifted_rows(pb, prev_ref, next_ref, on_mxu):
    tm = pb.shape[0]
    i = pl.program_id(1)
    last = pl.num_programs(1) - 1
    prev_row = jnp.where(i > 0, prev_ref[0, HALO_ROWS - 1:HALO_ROWS, :].astype(F32), 0.0)
    next_row = jnp.where(i < last, next_ref[0, 0:1, :].astype(F32), 0.0)
    if on_mxu:
        out_row = lax.broadcasted_iota(jnp.int32, (2 * tm, tm), 0)
        src_row = lax.broadcasted_iota(jnp.int32, (2 * tm, tm), 1)
        want = jnp.where(out_row < tm, out_row - 1, out_row - tm + 1)
        both = jnp.dot(jnp.where(src_row == want, 1.0, 0.0).astype(BF16), pb,
                       preferred_element_type=F32)
        down, up = both[:tm], both[tm:]
    else:
        p = pb.astype(F32)
        down, up = pltpu.roll(p, 1, axis=0), pltpu.roll(p, tm - 1, axis=0)
    sub = lax.broadcasted_iota(jnp.int32, (8, 1), 0)
    prev = jnp.concatenate([jnp.where(sub == 0, prev_row, down[:8]), down[8:]], axis=0)
    nxt = jnp.concatenate([up[:tm - 8], jnp.where(sub == 7, next_row, up[tm - 8:])], axis=0)
    return prev, nxt


def _split_rows(val, scr):
    rows, width = val.shape
    half = rows // 2
    for j in range(width // LANES):
        scr[j] = val[:, j * LANES:(j + 1) * LANES]
    pick = lambda start: jnp.concatenate(
        [scr[j, pl.ds(start, half, stride=2), :] for j in range(width // LANES)], axis=1)
    return pick(0), pick(1)


def _merge_rows(even, odd, scr):
    half, width = even.shape
    for j in range(width // LANES):
        cols = slice(j * LANES, (j + 1) * LANES)
        scr[j, pl.ds(0, half, stride=2), :] = even[:, cols]
        scr[j, pl.ds(1, half, stride=2), :] = odd[:, cols]
    return jnp.concatenate([scr[j] for j in range(width // LANES)], axis=1)


def _row_split_scratch(rows, width):
    return pltpu.VMEM((width // LANES, rows, LANES), F32)


def _halo_specs(tm, width, n_rows):
    th = tm // HALO_ROWS
    last_h = n_rows // HALO_ROWS - 1
    main = pl.BlockSpec((1, tm, width), lambda b, i: (b, i, 0))
    prev = pl.BlockSpec((1, HALO_ROWS, width), lambda b, i: (b, jnp.maximum(i * th - 1, 0), 0))
    nxt = pl.BlockSpec((1, HALO_ROWS, width),
                       lambda b, i: (b, jnp.minimum((i + 1) * th, last_h), 0))
    return main, prev, nxt


def _mod_kernel(c_ref, w_ref, b_ref, o_ref):
    o_ref[0] = _mm(_silu(c_ref[...]), w_ref[0], mode="x3") + b_ref[0]


def _modulation(cvec, mod_w, mod_b):
    rows, d = cvec.shape
    n = mod_w.shape[-1]
    tn = 1024
    return pl.pallas_call(
        _mod_kernel,
        grid=(DEPTH, n // tn),
        in_specs=[pl.BlockSpec((rows, d), lambda i, j: (0, 0)),
                  pl.BlockSpec((1, d, tn), lambda i, j: (i, 0, j)),
                  pl.BlockSpec((1, 1, tn), lambda i, j: (i, 0, j))],
        out_specs=pl.BlockSpec((1, rows, tn), lambda i, j: (i, 0, j)),
        out_shape=jax.ShapeDtypeStruct((DEPTH, rows, n), F32),
        compiler_params=_cparams("arbitrary", "arbitrary"),
        name="modulation",
    )(cvec, mod_w, mod_b.reshape(DEPTH, 1, n))


def _norm_proj_kernel(nw, x_ref, g_ref, sc_ref, sh_ref, *refs):
    x = x_ref[0]
    h = x * lax.rsqrt(jnp.mean(x * x, axis=-1, keepdims=True) + NORM_EPS) * g_ref[...]
    hb = (h * sc_ref[0] + sh_ref[0]).astype(BF16)
    for w_ref, o_ref in zip(refs[:nw], refs[nw:]):
        o_ref[0] = jnp.dot(hb, w_ref[...], preferred_element_type=F32).astype(BF16)


def _norm_proj(x, g, scale1p, shift, weights):
    bsz, n_rows, d = x.shape
    tm = min(ROW_TILE, n_rows)
    vec = pl.BlockSpec((1, 1, d), lambda b, i: (b, 0, 0))
    in_specs = [pl.BlockSpec((1, tm, d), lambda b, i: (b, i, 0)),
                pl.BlockSpec((1, d), lambda b, i: (0, 0)), vec, vec]
    in_specs += [pl.BlockSpec(w.shape, lambda b, i: (0, 0)) for w in weights]
    return pl.pallas_call(
        functools.partial(_norm_proj_kernel, len(weights)),
        grid=(bsz, n_rows // tm),
        in_specs=in_specs,
        out_specs=[pl.BlockSpec((1, tm, w.shape[1]), lambda b, i: (b, i, 0)) for w in weights],
        out_shape=[jax.ShapeDtypeStruct((bsz, n_rows, w.shape[1]), BF16) for w in weights],
        compiler_params=_cparams("parallel", "arbitrary"),
        name="norm_proj",
    )(x, g.reshape(1, d), scale1p, shift, *weights)


def _head_norm_rope(t, gain, tabs):
    ms = jnp.sum(t * t, axis=-1, keepdims=True) * (1.0 / MLA_QK)
    t = t * lax.rsqrt(ms + NORM_EPS) * gain
    if tabs is not None:
        cos_f, sin_a, sin_b = tabs
        t = (t * cos_f + pltpu.roll(t, MLA_ROPE // 2, axis=1) * sin_a
             + pltpu.roll(t, HEAD_PAD - MLA_ROPE // 2, axis=1) * sin_b)
    return t


def _even_in_kernel(rope, x_ref, g_ref, sc_ref, sh_ref, wdq_ref, wdkv_ref, wrw_ref, wg_ref,
                    qan_ref, wuq_ref, qgn_ref, kan_ref, wk_ref, wv_ref, vone_ref, kgn_ref, *refs):
    q_ref, k_ref, v_ref, prw_ref, go_ref = refs[-5:]
    q_gain, k_gain = qgn_ref[...], kgn_ref[...]

    def rms(t, gain_ref):
        return t * lax.rsqrt(jnp.mean(t * t, axis=-1, keepdims=True) + NORM_EPS) * gain_ref[...]

    tm = x_ref.shape[1]
    sub = min(SUB_ROWS, tm)
    for r0 in range(0, tm, sub):
        rows = slice(r0, r0 + sub)
        tabs = tuple(r[rows, :] for r in refs[:3]) if rope else None
        hb = (rms(x_ref[0, rows, :], g_ref) * sc_ref[0] + sh_ref[0]).astype(BF16)
        prw_ref[0, rows, :] = jnp.dot(hb, wrw_ref[...], preferred_element_type=F32).astype(BF16)
        go_ref[0, rows, :] = jnp.dot(hb, wg_ref[...], preferred_element_type=F32).astype(BF16)
        p_dq = jnp.dot(hb, wdq_ref[...], preferred_element_type=F32)
        q = jnp.dot(rms(p_dq, qan_ref).astype(BF16), wuq_ref[...], preferred_element_type=F32)
        p_dkv = jnp.dot(hb, wdkv_ref[...], preferred_element_type=F32)
        k_rope = p_dkv[:, MLA_KV_LORA:]
        ab = rms(p_dkv[:, :MLA_KV_LORA], kan_ref).astype(BF16)
        k_nope = jnp.dot(ab, wk_ref[...], preferred_element_type=F32)
        v_ref[0, rows, :] = (jnp.dot(ab, wv_ref[...], preferred_element_type=F32)
                             + vone_ref[...]).astype(BF16)
        for hd in range(MLA_HEADS):
            sl = slice(hd * HEAD_PAD, (hd + 1) * HEAD_PAD)
            q_ref[0, rows, sl] = _head_norm_rope(q[:, sl], q_gain, tabs).astype(BF16)
            k_ref[0, rows, sl] = _head_norm_rope(k_nope[:, sl] + k_rope, k_gain,
                                                 tabs).astype(BF16)


def _even_in(x, g, scale1p, shift, wts, tabs):
    bsz, n_rows, d = x.shape
    tm = min(EVEN_IN_TILE, n_rows)
    rope = tabs is not None
    kw = MLA_HEADS * HEAD_PAD
    vec = pl.BlockSpec((1, 1, d), lambda b, i: (b, 0, 0))
    full = lambda a: pl.BlockSpec(a.shape, lambda b, i: (0,) * a.ndim)
    consts = list(wts["proj"]) + [wts["q_a_norm"].reshape(1, MLA_Q_LORA), wts["w_uq"], wts["q_gain"],
                                  wts["kv_a_norm"].reshape(1, MLA_KV_LORA), wts["w_k"], wts["w_v"],
                                  wts["v_ones"], wts["k_gain"]]
    in_specs = [pl.BlockSpec((1, tm, d), lambda b, i: (b, i, 0)),
                pl.BlockSpec((1, d), lambda b, i: (0, 0)), vec, vec] + [full(a) for a in consts]
    args = [x, g.reshape(1, d), scale1p, shift] + consts
    if rope:
        in_specs += [pl.BlockSpec((tm, HEAD_PAD), lambda b, i: (i, 0))] * 3
        args += list(tabs)
    widths = [kw, kw, kw, RW_SHIFT, d]
    return pl.pallas_call(
        functools.partial(_even_in_kernel, rope),
        grid=(bsz, n_rows // tm),
        in_specs=in_specs,
        out_specs=[pl.BlockSpec((1, tm, w), lambda b, i: (b, i, 0)) for w in widths],
        out_shape=[jax.ShapeDtypeStruct((bsz, n_rows, w), BF16) for w in widths],
        compiler_params=_cparams("parallel", "arbitrary"),
        name="even_in",
    )(*args)


def _attn_body(q_ref, k_refs, v_refs, o_ref, bound):
    tq = q_ref.shape[1]
    lane = lax.broadcasted_iota(jnp.int32, (tq, LANES), 1)
    for hp in range(MLA_HEADS // 2):
        psl = slice(2 * hp * HEAD_PAD, (2 * hp + 2) * HEAD_PAD)
        outs = []
        for h in (2 * hp, 2 * hp + 1):
            sl = slice(h * HEAD_PAD, (h + 1) * HEAD_PAD)
            q = q_ref[0, :, sl]
            score = lambda k_ref, k0, k1: lax.dot_general(q, k_ref[0, k0:k1, sl], _NT,
                                                          preferred_element_type=F32)
            if bound is None:
                ss = [score(k_ref, 0, k_ref.shape[1]) for k_ref in k_refs]
                m = functools.reduce(jnp.maximum, [jnp.max(s, axis=-1, keepdims=True) for s in ss])
            acc = None
            for seg, (k_ref, v_ref) in enumerate(zip(k_refs, v_refs)):
                for k0 in range(0, k_ref.shape[1], ATTN_TK):
                    k1 = k0 + ATTN_TK
                    p = (jnp.exp2(ss[seg][:, k0:k1] - m) if bound is None
                         else jnp.exp2(score(k_ref, k0, k1) - bound))
                    pv = jnp.dot(p.astype(BF16), v_ref[0, k0:k1, psl], preferred_element_type=F32)
                    acc = pv if acc is None else pv + acc
            acc = acc[:, :HEAD_PAD] if h % 2 == 0 else acc[:, HEAD_PAD:]
            ones_lane = MLA_V if h % 2 == 0 else 0
            outs.append(acc * (1.0 / acc[:, ones_lane:ones_lane + 1]))
        o_ref[0, :, hp * LANES:(hp + 1) * LANES] = jnp.where(lane < MLA_V, outs[0],
                                                             outs[1]).astype(BF16)


def _attn_kernel(nseg, bound_ref, q_ref, *refs):
    k_refs, v_refs, o_ref = refs[:nseg], refs[nseg:2 * nseg], refs[2 * nseg]
    bound = bound_ref[0]

    @pl.when(bound <= ATTN_SAFE_BITS)
    def _():
        _attn_body(q_ref, k_refs, v_refs, o_ref, bound)

    @pl.when(bound > ATTN_SAFE_BITS)
    def _():
        _attn_body(q_ref, k_refs, v_refs, o_ref, None)


def _attention(q, ks, vs, bound):
    bsz, n_q, qw = q.shape
    tq = min(ATTN_TQ, n_q)
    nseg = len(ks)
    in_specs = [pl.BlockSpec(memory_space=pltpu.SMEM),
                pl.BlockSpec((1, tq, qw), lambda b, i: (b, i, 0))]
    in_specs += [pl.BlockSpec((1,) + k.shape[1:], lambda b, i: (b, 0, 0)) for k in ks]
    in_specs += [pl.BlockSpec((1,) + v.shape[1:], lambda b, i: (b, 0, 0)) for v in vs]
    return pl.pallas_call(
        functools.partial(_attn_kernel, nseg),
        grid=(bsz, n_q // tq),
        in_specs=in_specs,
        out_specs=pl.BlockSpec((1, tq, MLA_WIDTH), lambda b, i: (b, i, 0)),
        out_shape=jax.ShapeDtypeStruct((bsz, n_q, MLA_WIDTH), BF16),
        compiler_params=_cparams("parallel", "arbitrary"),
        name="mla_attention",
    )(bound, q, *ks, *vs)


def _rw_prep_kernel(p_ref, prev_ref, next_ref, mup_ref, mun_ref, w0_ref, wup_ref, a0_ref, aup_ref,
                    kk_ref, ka_ref, rk_ref, seg_ref,
                    r_out, v_out, kkn_out, bonus_out, lw0_out, lw1_out, b0_out, b1_out,
                    kd0_out, kd1_out):
    pb = p_ref[0]
    p = pb.astype(F32)
    prev, nxt = _shifted_rows(pb, prev_ref, next_ref, on_mxu=False)
    mu_p, mu_n = mup_ref[...], mun_ref[...]
    ps = p * (1.0 - mu_p - mu_n) + prev * mu_p + nxt * mu_n
    w = RW_WIDTH
    r, k, v = ps[:, :w], ps[:, w:2 * w], ps[:, 2 * w:3 * w]
    wd = jnp.tanh(ps[:, 3 * w:3 * w + 2 * RW_LORA_W])
    ad = ps[:, 3 * w + 2 * RW_LORA_W:]
    seg = seg_ref[...]
    kq = k * kk_ref[...]
    kk = kq * lax.rsqrt(jnp.maximum(_mm(kq * kq, seg, mode="lhs2"), 1e-24))
    r_out[0] = r.astype(BF16)
    v_out[0] = v.astype(BF16)
    kkn_out[0] = kk.astype(BF16)
    bonus_in = jnp.zeros_like(r)
    for d, (lw_out, b_out, kd_out) in enumerate(((lw0_out, b0_out, kd0_out),
                                                  (lw1_out, b1_out, kd1_out))):
        z = w0_ref[d:d + 1, :] + _mm(wd, wup_ref[d], mode="x3")
        lw_out[0] = -math.exp(-0.5) * jax.nn.sigmoid(z)
        a = jax.nn.sigmoid(a0_ref[d:d + 1, :] + _mm(ad, aup_ref[d], mode="x3"))
        kd = k * (1.0 + (a - 1.0) * ka_ref[...])
        b_out[0] = (kk * a).astype(BF16)
        kd_out[0] = kd.astype(BF16)
        bonus_in = bonus_in + r * kd * rk_ref[d:d + 1, :]
    bonus_out[0] = (_mm(bonus_in, seg, mode="lhs2") * v).astype(BF16)


def _rw_prep(p_rw, prm):
    bsz, n_rows, width = p_rw.shape
    tm = min(ROW_TILE, n_rows)
    main, prev, nxt = _halo_specs(tm, width, n_rows)
    full = lambda a: pl.BlockSpec(a.shape, lambda b, i: (0,) * a.ndim)
    consts = [prm["mu_prev"], prm["mu_next"], prm["w0"], prm["w_up"], prm["a0"], prm["a_up"],
              prm["k_k"], prm["k_a"], prm["r_k"], prm["seg"]]
    out_spec = pl.BlockSpec((1, tm, RW_WIDTH), lambda b, i: (b, i, 0))
    shape = lambda dt: jax.ShapeDtypeStruct((bsz, n_rows, RW_WIDTH), dt)
    outs = pl.pallas_call(
        _rw_prep_kernel,
        grid=(bsz, n_rows // tm),
        in_specs=[main, prev, nxt] + [full(a) for a in consts],
        out_specs=[out_spec] * 10,
        out_shape=[shape(BF16)] * 4 + [shape(F32)] * 2 + [shape(BF16)] * 4,
        compiler_params=_cparams("parallel", "arbitrary"),
        name="rwkv_prep",
    )(p_rw, p_rw, p_rw, *consts)
    r, v, kk, bonus, lw0, lw1, b0, b1, kd0, kd1 = outs
    return dict(r=r, v=v, kk=kk, bonus=bonus, lw=(lw0, lw1), b=(b0, b1), kd=(kd0, kd1))


def _rwkv_scan_kernel(rf_ref, vf_ref, kkf_ref, lw0_ref, b0_ref, kd0_ref,
                      rb_ref, vb_ref, kkb_ref, lw1_ref, b1_ref, kd1_ref, s0_ref,
                      of_ref, ob_ref, sfin_ref, st_ref):
    j = pl.program_id(1)

    @pl.when(j == 0)
    def _():
        st_ref[...] = s0_ref[0]

    c = CHUNK
    c2 = 2 * c
    nsub = rf_ref.shape[1] // c
    dir_refs = ((rf_ref, vf_ref, kkf_ref, lw0_ref, b0_ref, kd0_ref, of_ref),
                (rb_ref, vb_ref, kkb_ref, lw1_ref, b1_ref, kd1_ref, ob_ref))
    ti = lax.broadcasted_iota(jnp.int32, (c, c), 0)
    si = lax.broadcasted_iota(jnp.int32, (c, c), 1)
    tp = lax.broadcasted_iota(jnp.int32, (c, c2), 0)
    sp = lax.broadcasted_iota(jnp.int32, (c, c2), 1) & (c - 1)
    eye_p = jnp.where(tp == sp, 1.0, 0.0)
    head0 = lax.broadcasted_iota(jnp.int32, (c, RW_PAIR), 1) < RW_HEAD
    masks = []
    for rev in (False, True):
        tri = jnp.where((si >= ti) if rev else (si <= ti), 1.0, 0.0)
        before = (sp > tp) if rev else (sp < tp)
        upto = (sp >= tp) if rev else (sp <= tp)
        masks.append((tri, before, upto))

    def stack(t):
        return jnp.concatenate([jnp.where(head0, t, 0.0), jnp.where(head0, 0.0, t)], axis=0)

    def unstack(t):
        return t[:c] + t[c:]

    mm = functools.partial(_mm, mode="bf16")
    items = [(d, p, s) for d in range(2) for p in range(N_PAIRS) for s in range(nsub)]
    pair_sl = lambda p: slice(p * RW_PAIR, (p + 1) * RW_PAIR)
    chunk_rows = lambda s: slice(s * c, (s + 1) * c)

    lams = [[_mm(masks[d][0], dir_refs[d][3][0, chunk_rows(s), :], mode="f32") for s in range(nsub)]
            for d in range(2)]

    ops = {key: {} for key in items}

    def prep(key):
        d, p, s = key
        r_ref, v_ref, kk_ref, lw_ref, b_ref, kd_ref, _ = dir_refs[d]
        sl, rs = pair_sl(p), chunk_rows(s)
        lam = lams[d][s][:, sl]
        lam_tot = lam[0:1] if d == 1 else lam[c - 1:c]
        e_neg = jnp.exp(-lam)
        e_tail = jnp.exp(lam_tot - lam)
        ld = lambda ref: ref[0, rs, sl].astype(F32)
        b, kd = ld(b_ref), ld(kd_ref)
        at = -ld(kk_ref) * jnp.exp(lam - lw_ref[0, rs, sl])
        bt_t = unstack(stack(b * e_tail).T)
        kt_t = unstack(stack(kd * e_tail).T)
        ops[key].update(
            at=at, rt=ld(r_ref) * jnp.exp(lam), at2=stack(at), v2=stack(ld(v_ref)),
            bk2=jnp.concatenate([stack(b * e_neg), stack(kd * e_neg)], axis=0),
            bkt_t=jnp.concatenate([bt_t, kt_t], axis=1), bt_t=bt_t,
            decay_tot=jnp.where(eye_p == 1.0, jnp.exp(lam_tot), 0.0))

    def interactions(key):
        o = ops[key]
        _, before, upto = masks[key[0]]
        x = mm(jnp.concatenate([o["at"], o["rt"]], axis=0), o["bk2"], _NT)
        o["a_ab"] = jnp.where(before, x[:c, :c2], 0.0)
        o["a_ak"] = jnp.where(before, x[:c, c2:], 0.0)
        o["a_rb"] = jnp.where(upto, x[c:, :c2], 0.0)
        o["a_rk"] = jnp.where(upto, x[c:, c2:], 0.0)

    def inverse_first(key):
        o = ops[key]
        o["t_inv"] = eye_p + o["a_ab"]
        o["pw"] = mm(o["a_ab"], stack(o["a_ab"]))
        o["w"] = mm(o["a_ak"], o["v2"])

    def inverse_step(key):
        o = ops[key]
        y = mm(jnp.concatenate([o["pw"], o["t_inv"]], axis=0), stack(o["pw"]))
        o["pw"] = y[:c]
        o["t_inv"] = o["t_inv"] + y[c:]

    def inverse_last(key):
        o = ops[key]
        o["t_inv"] = o["t_inv"] + mm(o["t_inv"], stack(o["pw"]))

    def transforms(key):
        o = ops[key]
        z = mm(o["t_inv"], jnp.concatenate([o["at2"], stack(o["w"])], axis=1))
        ap2, u02 = stack(z[:, :c2]), stack(z[:, c2:])
        ya = mm(jnp.concatenate([o["a_rb"], o["bt_t"]], axis=0), ap2)
        yb = mm(jnp.concatenate([jnp.concatenate([o["a_rb"], o["a_rk"]], axis=1), o["bkt_t"]], axis=0),
                jnp.concatenate([u02, o["v2"]], axis=0))
        o["rp"] = o["rt"] + ya[:c]
        o["o0"] = yb[:c]
        o["m_p"] = o["decay_tot"] + ya[c:]
        o["n_p"] = yb[c:]

    stages = ([prep, interactions, inverse_first] + [inverse_step] * (int(math.log2(c)) - 2)
              + [inverse_last, transforms])
    for stage in stages:
        for key in items:
            stage(key)

    by_item = ops
    chains =[(d, p) for d in range(2) for p in range(N_PAIRS)]
    st_rows = lambda d: slice(d * RW_HEAD, (d + 1) * RW_HEAD)
    states = {(d, p): st_ref[st_rows(d), pair_sl(p)] for d, p in chains}
    for step in range(nsub):
        for d, p in chains:
            s = step if d == 0 else nsub - 1 - step
            o = by_item[(d, p, s)]
            state2 = stack(states[(d, p)])
            dir_refs[d][6][0, chunk_rows(s), pair_sl(p)] = (mm(o["rp"], state2)
                                                            + o["o0"]).astype(BF16)
            states[(d, p)] = mm(o["m_p"], state2) + o["n_p"]
    for d, p in chains:
        st_ref[st_rows(d), pair_sl(p)] = states[(d, p)]

    @pl.when(j == pl.num_programs(1) - 1)
    def _():
        sfin_ref[0] = st_ref[...]


def _rwkv_scan(st, s0):
    bsz, n_rows, w = st["r"].shape
    rows = SCAN_CHUNKS * CHUNK
    nblk = n_rows // rows
    fwd = pl.BlockSpec((1, rows, w), lambda b, j: (b, j, 0))
    bwd = pl.BlockSpec((1, rows, w), lambda b, j: (b, nblk - 1 - j, 0))
    sspec = pl.BlockSpec((1, 2 * RW_HEAD, w), lambda b, j: (b, 0, 0))
    row_shape = jax.ShapeDtypeStruct((bsz, n_rows, w), BF16)
    return pl.pallas_call(
        _rwkv_scan_kernel,
        grid=(bsz, nblk),
        in_specs=[fwd] * 6 + [bwd] * 6 + [sspec],
        out_specs=[fwd, bwd, sspec],
        out_shape=[row_shape, row_shape, jax.ShapeDtypeStruct((bsz, 2 * RW_HEAD, w), F32)],
        scratch_shapes=[pltpu.VMEM((2 * RW_HEAD, w), F32)],
        compiler_params=_cparams("parallel", "arbitrary"),
        name="rwkv_scan",
    )(st["r"], st["v"], st["kk"], st["lw"][0], st["b"][0], st["kd"][0],
      st["r"], st["v"], st["kk"], st["lw"][1], st["b"][1], st["kd"][1], s0)


def _even_out_kernel(x_ref, gate_ref, om_ref, of_ref, ob_ref, bonus_ref, g_ref, lnw_ref, lnb_ref,
                     seg_ref, w_ref, o_ref):
    seg = seg_ref[...]
    f32 = lambda ref: ref[0].astype(F32)
    o = f32(of_ref) + f32(ob_ref)
    mu = _mm(o, seg, mode="lhs2") * (1.0 / RW_HEAD)
    dlt = o - mu
    var = _mm(dlt * dlt, seg, mode="lhs2") * (1.0 / RW_HEAD)
    o_rw = dlt * lax.rsqrt(var + RW_GN_EPS) * lnw_ref[...] + lnb_ref[...] + f32(bonus_ref)
    g = f32(g_ref)
    z_m = (f32(om_ref) * _silu(g[:, :MLA_WIDTH])).astype(BF16)
    z_r = (o_rw * _silu(g[:, MLA_WIDTH:])).astype(BF16)
    y = (jnp.dot(z_m, w_ref[:MLA_WIDTH, :], preferred_element_type=F32)
         + jnp.dot(z_r, w_ref[MLA_WIDTH:, :], preferred_element_type=F32))
    o_ref[0] = x_ref[0] + gate_ref[0] * y


def _even_out(x, gate, o_mla, o_f, o_b, bonus, g, ln_w, ln_b, seg, w_out):
    bsz, n_rows, d = x.shape
    tm = min(ROW_TILE, n_rows)
    rows = lambda width: pl.BlockSpec((1, tm, width), lambda b, i: (b, i, 0))
    full = lambda a: pl.BlockSpec(a.shape, lambda b, i: (0,) * a.ndim)
    return pl.pallas_call(
        _even_out_kernel,
        grid=(bsz, n_rows // tm),
        in_specs=[rows(d), pl.BlockSpec((1, 1, d), lambda b, i: (b, 0, 0)),
                  rows(MLA_WIDTH), rows(RW_WIDTH), rows(RW_WIDTH), rows(RW_WIDTH), rows(d),
                  full(ln_w), full(ln_b), full(seg), full(w_out)],
        out_specs=rows(d),
        out_shape=jax.ShapeDtypeStruct(x.shape, F32),
        compiler_params=_cparams("parallel", "arbitrary"),
        name="even_out",
    )(x, gate, o_mla, o_f, o_b, bonus, g, ln_w, ln_b, seg, w_out)


def _conv3_kernel(u_ref, prev_ref, next_ref, w_ref, b_ref, ve_ref, vo_ref, x0_ref, v_scr):
    ub = u_ref[0]
    u = ub.astype(F32)
    prev, nxt = _shifted_rows(ub, prev_ref, next_ref, on_mxu=True)
    cv = prev * w_ref[0:1, :] + u * w_ref[1:2, :] + nxt * w_ref[2:3, :] + b_ref[...]
    hw = HY_WIDTH
    x0_ref[0] = cv[:, :hw].astype(BF16)
    v_even, v_odd = _split_rows(cv[:, 2 * hw:] * cv[:, hw:2 * hw], v_scr)
    ve_ref[0] = v_even.astype(BF16)
    vo_ref[0] = v_odd.astype(BF16)


def _conv3(u, conv_w, conv_b):
    bsz, n_rows, width = u.shape
    tm = min(ROW_TILE, n_rows)
    main, prev, nxt = _halo_specs(tm, width, n_rows)
    half_spec = pl.BlockSpec((1, tm // 2, HY_WIDTH), lambda b, i: (b, i, 0))
    half_shape = jax.ShapeDtypeStruct((bsz, n_rows // 2, HY_WIDTH), BF16)
    return pl.pallas_call(
        _conv3_kernel,
        grid=(bsz, n_rows // tm),
        in_specs=[main, prev, nxt,
                  pl.BlockSpec(conv_w.shape, lambda b, i: (0, 0)),
                  pl.BlockSpec((1, width), lambda b, i: (0, 0))],
        out_specs=[half_spec, half_spec, pl.BlockSpec((1, tm, HY_WIDTH), lambda b, i: (b, i, 0))],
        out_shape=[half_shape, half_shape, jax.ShapeDtypeStruct((bsz, n_rows, HY_WIDTH), BF16)],
        scratch_shapes=[_row_split_scratch(tm, HY_WIDTH)],
        compiler_params=_cparams("parallel", "arbitrary"),
        name="hyena_conv3",
    )(u, u, u, conv_w, conv_b.reshape(1, width))


def _hy_hidden_kernel(w1_ref, b1_ref, w2_ref, b2_ref, freq_ref, h_ref):
    n_pos = h_ref.shape[0]
    pos = lax.broadcasted_iota(jnp.int32, (n_pos, 1), 0).astype(F32)
    lane = lax.broadcasted_iota(jnp.int32, (1, LANES), 1)
    band_idx = jnp.where(lane <= HY_BANDS, lane - 1, lane - 1 - HY_BANDS).astype(F32)
    band = 1e-4 + band_idx * ((HY_BANDS - 1 - 1e-4) / (HY_BANDS - 1))
    ang = pos * (2.0 * math.pi / n_pos) * band
    z = jnp.where(lane == 0, pos / (n_pos - 1),
                  jnp.where(lane <= HY_BANDS, jnp.cos(ang),
                            jnp.where(lane <= 2 * HY_BANDS, -jnp.sin(ang), 0.0)))
    freq = freq_ref[...]
    hdn = jnp.sin(freq * (_mm(z, w1_ref[...], mode="f32") + b1_ref[...]))
    for j in range(HY_INNER):
        hdn = jnp.sin(freq * (_mm(hdn, w2_ref[j], mode="f32") + b2_ref[j]))
    h_ref[...] = hdn


def _hy_filter_kernel(h_ref, w0_ref, w1_ref, dl_ref, fse_ref, fso_ref, fde_ref, fdo_ref, scr):
    n_pos = h_ref.shape[0]
    pos = lax.broadcasted_iota(jnp.int32, (n_pos, 1), 0)
    t = pos.astype(F32) / (n_pos - 1)
    dec = jnp.exp(-t * dl_ref[...])
    hdn = h_ref[...]
    f_fwd = _mm(hdn, w0_ref[...], mode="f32") * dec
    f_bwd = jnp.where(pos == 0, 0.0, _mm(hdn, w1_ref[...], mode="f32") * dec)
    inv = 1.0 / (jnp.sum(jnp.abs(f_fwd), axis=0, keepdims=True)
                 + jnp.sum(jnp.abs(f_bwd), axis=0, keepdims=True))
    fse_ref[...], fso_ref[...] = _split_rows((f_fwd + f_bwd) * inv, scr)
    fde_ref[...], fdo_ref[...] = _split_rows((f_bwd - f_fwd) * inv, scr)


def _hyena_filters(n_pos, f_w1, f_b1, f_w2, f_b2, f_wout, freq, deltas):
    w1 = jnp.zeros((LANES, HY_ORDER), F32).at[:HY_EMB].set(f_w1)
    hdn = pl.pallas_call(
        _hy_hidden_kernel,
        out_shape=jax.ShapeDtypeStruct((n_pos, HY_ORDER), F32),
        name="hyena_filter_hidden",
    )(w1, f_b1.reshape(1, HY_ORDER), f_w2, f_b2.reshape(HY_INNER, 1, HY_ORDER),
      freq.reshape(1, HY_ORDER))
    tn = 256
    cspec = pl.BlockSpec((HY_ORDER, tn), lambda j: (0, j))
    ospec = pl.BlockSpec((n_pos // 2, tn), lambda j: (0, j))
    oshape = jax.ShapeDtypeStruct((n_pos // 2, HY_WIDTH), F32)
    return pl.pallas_call(
        _hy_filter_kernel,
        grid=(HY_WIDTH // tn,),
        in_specs=[pl.BlockSpec((n_pos, HY_ORDER), lambda j: (0, 0)), cspec, cspec,
                  pl.BlockSpec((1, tn), lambda j: (0, j))],
        out_specs=[ospec] * 4,
        out_shape=[oshape] * 4,
        scratch_shapes=[_row_split_scratch(n_pos, tn)],
        compiler_params=_cparams("arbitrary"),
        name="hyena_filter",
    )(hdn, f_wout[:, :HY_WIDTH], f_wout[:, HY_WIDTH:], deltas)


def _dft_tables_kernel(ec_ref, es_ref, oc_ref, os_ref, oct_ref, ost_ref):
    tr, kq = ec_ref.shape
    n = 4 * kq
    r = lax.broadcasted_iota(jnp.int32, (tr, kq), 0) + pl.program_id(0) * tr
    c = lax.broadcasted_iota(jnp.int32, (tr, kq), 1)

    def cos_sin(phase):
        ph = phase & (n - 1)
        ph = jnp.where(ph >= n // 2, ph - n, ph)
        ang = ph.astype(F32) * (2.0 * math.pi / n)
        return jnp.cos(ang).astype(BF16), jnp.sin(ang).astype(BF16)

    ec_ref[...], es_ref[...] = cos_sin(2 * r * c)
    oc_ref[...], os_ref[...] = cos_sin(r * (2 * c + 1))
    oct_ref[...], ost_ref[...] = cos_sin(c * (2 * r + 1))


def _dft_tables(n_pos):
    kq = n_pos // 2
    tr = min(256, kq)
    spec = pl.BlockSpec((tr, kq), lambda i: (i, 0))
    shape = jax.ShapeDtypeStruct((kq, kq), BF16)
    return pl.pallas_call(
        _dft_tables_kernel,
        grid=(kq // tr,),
        out_specs=[spec] * 6,
        out_shape=[shape] * 6,
        compiler_params=_cparams("arbitrary"),
        name="dft_tables",
    )()


def _alt_sign(n_pos):
    pos = lax.broadcasted_iota(jnp.int32, (n_pos, 1), 0)
    return jnp.where((pos & 1) == 0, 1.0, -1.0), pos


def _spectrum_kernel(fse_ref, fso_ref, fde_ref, fdo_ref, ec_ref, es_ref, oc_ref, os_ref,
                     hare_ref, haim_ref, hbre_ref, hbim_ref, hk_ref):
    kq = fse_ref.shape[0]
    alt, pos = _alt_sign(kq)

    def mm2(tab_ref, f):
        fh = f.astype(BF16)
        fl = (f - fh.astype(F32)).astype(BF16)
        return (jnp.dot(tab_ref[...], fh, preferred_element_type=F32)
                + jnp.dot(tab_ref[...], fl, preferred_element_type=F32))

    fse, fdo = fse_ref[...], fdo_ref[...]
    ce, co = mm2(ec_ref, fse), mm2(oc_ref, fso_ref[...])
    se, so = mm2(es_ref, fde_ref[...]), mm2(os_ref, fdo)
    scale = 0.5 / kq
    scale_re = jnp.where(pos == 0, 0.5 * scale, scale)
    hare_ref[...] = (ce + co) * scale_re
    hbre_ref[...] = (ce - co) * scale_re
    haim_ref[...] = (se + so) * scale
    hbim_ref[...] = (so - se) * scale
    hk_ref[0:1, :] = jnp.sum(fse * alt, axis=0, keepdims=True) * scale
    hk_ref[1:2, :] = jnp.sum(fdo * alt, axis=0, keepdims=True) * scale


def _table_specs(tabs, nd):
    imap = (lambda j: (0, 0)) if nd == 1 else (lambda b, j: (0, 0))
    return [pl.BlockSpec(t.shape, imap, pipeline_mode=pl.Buffered(1)) for t in tabs]


def _spectrum(filters, tabs):
    kq, width = filters[0].shape
    tc = DFT_TC
    half = pl.BlockSpec((kq, tc), lambda j: (0, j))
    return pl.pallas_call(
        _spectrum_kernel,
        grid=(width // tc,),
        in_specs=[half] * 4 + _table_specs(tabs[:4], 1),
        out_specs=[half] * 4 + [pl.BlockSpec((2, tc), lambda j: (0, j))],
        out_shape=[jax.ShapeDtypeStruct((kq, width), F32)] * 4
        + [jax.ShapeDtypeStruct((2, width), F32)],
        compiler_params=_cparams("arbitrary"),
        name="hyena_spectrum",
    )(*filters, *tabs[:4])


def _dft_conv_kernel(ve_ref, vo_ref, hare_ref, haim_ref, hbre_ref, hbim_ref, hk_ref,
                     ec_ref, es_ref, oc_ref, os_ref, oct_ref, ost_ref, ye_ref, yo_ref,
                     pe_scr, qe_scr, po_scr, qo_scr):
    kq, tc = ve_ref.shape[1], ve_ref.shape[2]
    tm = min(DFT_TM, kq)
    alt, _ = _alt_sign(kq)
    ve, vo = ve_ref[0], vo_ref[0]
    dot = functools.partial(jnp.dot, preferred_element_type=F32)
    for r0 in range(0, kq, tm):
        rows = slice(r0, r0 + tm)
        ce, co = dot(ec_ref[rows, :], ve), dot(oc_ref[rows, :], vo)
        se, so = dot(es_ref[rows, :], ve), dot(os_ref[rows, :], vo)
        va_re, va_s, vb_re, vb_s = ce + co, se + so, ce - co, so - se
        hare, haim = hare_ref[rows, :], haim_ref[rows, :]
        hbre, hbim = hbre_ref[rows, :], hbim_ref[rows, :]
        ya_re, ya_s = va_re * hare + va_s * haim, va_s * hare - va_re * haim
        yb_re, yb_s = vb_re * hbre + vb_s * hbim, vb_s * hbre - vb_re * hbim
        pe_scr[rows, :] = (ya_re + yb_re).astype(BF16)
        qe_scr[rows, :] = (ya_s - yb_s).astype(BF16)
        po_scr[rows, :] = (ya_re - yb_re).astype(BF16)
        qo_scr[rows, :] = (ya_s + yb_s).astype(BF16)
    vk_re = jnp.sum(ve.astype(F32) * alt, axis=0, keepdims=True)
    vk_s = jnp.sum(vo.astype(F32) * alt, axis=0, keepdims=True)
    hk_re, hk_im = hk_ref[0:1, :], hk_ref[1:2, :]
    yk_re = vk_re * hk_re + vk_s * hk_im
    yk_s = vk_s * hk_re - vk_re * hk_im
    pe, qe, po, qo = pe_scr[...], qe_scr[...], po_scr[...], qo_scr[...]
    for r0 in range(0, kq, tm):
        rows = slice(r0, r0 + tm)
        ye_ref[0, rows, :] = (dot(ec_ref[rows, :], pe) + dot(es_ref[rows, :], qe)
                              + alt[rows] * yk_re).astype(BF16)
        yo_ref[0, rows, :] = (dot(oct_ref[rows, :], po) + dot(ost_ref[rows, :], qo)
                              + alt[rows] * yk_s).astype(BF16)


def _dft_conv(v_even, v_odd, spec, tabs):
    bsz, kq, width = v_even.shape
    tc = DFT_TC
    half = pl.BlockSpec((1, kq, tc), lambda b, j: (b, 0, j))
    col = pl.BlockSpec((kq, tc), lambda b, j: (0, j))
    return pl.pallas_call(
        _dft_conv_kernel,
        grid=(bsz, width // tc),
        in_specs=[half, half, col, col, col, col, pl.BlockSpec((2, tc), lambda b, j: (0, j))]
        + _table_specs(tabs, 2),
        out_specs=[half, half],
        out_shape=[jax.ShapeDtypeStruct(v_even.shape, BF16)] * 2,
        scratch_shapes=[pltpu.VMEM((kq, tc), BF16)] * 4,
        compiler_params=_cparams("parallel", "arbitrary"),
        name="hyena_dft_conv",
    )(v_even, v_odd, *spec, *tabs)


def _hy_out_kernel(x_ref, gate_ref, ye_ref, yo_ref, ve_ref, vo_ref, x0_ref, g_ref, bias_ref, w_ref,
                   o_ref, t_scr):
    f32 = lambda ref: ref[0].astype(F32)
    bias = bias_ref[...]
    t = _merge_rows(f32(ye_ref) + f32(ve_ref) * bias, f32(yo_ref) + f32(vo_ref) * bias, t_scr)
    z = t * f32(x0_ref) * _silu(f32(g_ref))
    o_ref[0] = x_ref[0] + gate_ref[0] * jnp.dot(z.astype(BF16), w_ref[...],
                                                preferred_element_type=F32)


def _hy_out(x, gate, y_even, y_odd, v_even, v_odd, x0, g, bias_d, w_out):
    bsz, n_rows, d = x.shape
    tm = min(ROW_TILE, n_rows)
    rows = pl.BlockSpec((1, tm, d), lambda b, i: (b, i, 0))
    half = pl.BlockSpec((1, tm // 2, d), lambda b, i: (b, i, 0))
    return pl.pallas_call(
        _hy_out_kernel,
        grid=(bsz, n_rows // tm),
        in_specs=[rows, pl.BlockSpec((1, 1, d), lambda b, i: (b, 0, 0)), half, half, half, half,
                  rows, rows, pl.BlockSpec((1, d), lambda b, i: (0, 0)),
                  pl.BlockSpec(w_out.shape, lambda b, i: (0, 0))],
        out_specs=rows,
        out_shape=jax.ShapeDtypeStruct(x.shape, F32),
        scratch_shapes=[_row_split_scratch(tm, d)],
        compiler_params=_cparams("parallel", "arbitrary"),
        name="hyena_out",
    )(x, gate, y_even, y_odd, v_even, v_odd, x0, g, bias_d.reshape(1, d), w_out)


def _rope_perm():
    pairs = np.arange(MLA_ROPE // 2)
    return np.concatenate([np.arange(MLA_NOPE), MLA_NOPE + 2 * pairs, MLA_NOPE + 2 * pairs + 1])


def _rope_tables(n_pos):
    rows = n_pos // GRID_W
    row = jnp.repeat(jnp.arange(rows, dtype=F32), GRID_W)
    col = jnp.tile(jnp.arange(GRID_W, dtype=F32), rows)
    n_freq = MLA_ROPE // 4
    inv = ROPE_BASE ** (-jnp.arange(n_freq, dtype=F32) / n_freq)
    ang = jnp.concatenate([row[:, None] * inv, col[:, None] * inv], axis=-1)
    cos, sin = jnp.cos(ang), jnp.sin(ang)
    half = MLA_ROPE // 2
    ones = jnp.ones((n_pos, MLA_NOPE), F32)
    zeros = jnp.zeros((n_pos, MLA_NOPE), F32)
    pad1 = jnp.ones((n_pos, HEAD_PAD - MLA_QK), F32)
    pad0 = jnp.zeros((n_pos, HEAD_PAD - MLA_QK), F32)
    z16 = jnp.zeros((n_pos, half), F32)
    cos_f = jnp.concatenate([ones, cos, cos, pad1], axis=-1)
    sin_a = jnp.concatenate([zeros, z16, sin, pad0], axis=-1)
    sin_b = jnp.concatenate([zeros, -sin, z16, pad0], axis=-1)
    return cos_f, sin_a, sin_b


def _pad_heads(w, width):
    k, h, _ = w.shape
    return jnp.zeros((k, h, HEAD_PAD), w.dtype).at[:, :, :width].set(w).reshape(k, h * HEAD_PAD)


def _even_weights(e, ev_w_in, ev_w_out, mla_q_a_norm, mla_w_uq, mla_kv_a_norm, mla_w_ukv,
                  mla_q_norm, mla_k_norm, rwkv_mu_prev, rwkv_mu_next, rwkv_w0, rwkv_w_up, rwkv_a0,
                  rwkv_a_up, rwkv_k_k, rwkv_k_a, rwkv_r_k, rwkv_ln_w, rwkv_ln_b):
    perm = _rope_perm()
    w_in = ev_w_in[e]
    d = w_in.shape[0]
    o_dkv = EV_DQ
    o_rw = EV_DQ + EV_DKV
    o_g = o_rw + RW_SHIFT
    w_dkv = jnp.zeros((d, 2 * LANES), F32)
    w_dkv = w_dkv.at[:, :MLA_KV_LORA].set(w_in[:, o_dkv:o_dkv + MLA_KV_LORA])
    rope_cols = o_dkv + MLA_KV_LORA + (perm[MLA_NOPE:] - MLA_NOPE)
    w_dkv = w_dkv.at[:, LANES + MLA_NOPE:LANES + MLA_QK].set(w_in[:, rope_cols])
    proj = [w_in[:, :EV_DQ].astype(BF16), w_dkv.astype(BF16),
            w_in[:, o_rw:o_g].astype(BF16), w_in[:, o_g:].astype(BF16)]
    w_ukv = mla_w_ukv[e]
    pad_gain = lambda g: jnp.zeros((1, HEAD_PAD), F32).at[0, :MLA_QK].set(g[perm])
    seg_id = np.arange(RW_WIDTH) // RW_HEAD
    zero_up = lambda up, dd: jnp.zeros((2 * RW_LORA_W, RW_WIDTH), F32).at[
        dd * RW_LORA_W:(dd + 1) * RW_LORA_W].set(up[dd])
    rw = dict(
        mu_prev=rwkv_mu_prev[e].reshape(1, RW_SHIFT), mu_next=rwkv_mu_next[e].reshape(1, RW_SHIFT),
        w0=rwkv_w0[e], a0=rwkv_a0[e],
        w_up=jnp.stack([zero_up(rwkv_w_up[e], 0), zero_up(rwkv_w_up[e], 1)]),
        a_up=jnp.stack([zero_up(rwkv_a_up[e], 0), zero_up(rwkv_a_up[e], 1)]),
        k_k=rwkv_k_k[e].reshape(1, RW_WIDTH), k_a=rwkv_k_a[e].reshape(1, RW_WIDTH),
        r_k=rwkv_r_k[e].reshape(2, RW_WIDTH),
        seg=jnp.asarray((seg_id[:, None] == seg_id[None, :]).astype(np.float32)),
    )
    head_par = np.arange(MLA_HEADS) % 2
    v_lane = (np.arange(HEAD_PAD)[None, :] >= MLA_V) == (head_par[:, None] == 1)
    v_ones = (~v_lane).astype(np.float32)
    w_vh = w_ukv[:, :, MLA_NOPE:].reshape(MLA_KV_LORA, MLA_HEADS // 2, 2, MLA_V)
    zero_v = jnp.zeros_like(w_vh[:, :, 0])
    w_v = jnp.stack([w_vh[:, :, 0], zero_v, zero_v, w_vh[:, :, 1]], axis=2)
    return dict(
        proj=proj,
        q_a_norm=mla_q_a_norm[e], kv_a_norm=mla_kv_a_norm[e],
        w_uq=_pad_heads(mla_w_uq[e][:, :, perm], MLA_QK).astype(BF16),
        w_k=_pad_heads(w_ukv[:, :, :MLA_NOPE], MLA_NOPE).astype(BF16),
        w_v=w_v.reshape(MLA_KV_LORA, MLA_HEADS * HEAD_PAD).astype(BF16),
        v_ones=jnp.asarray(v_ones.reshape(1, MLA_HEADS * HEAD_PAD)),
        q_gain=pad_gain(mla_q_norm[e]) * (MLA_QK ** -0.5 * math.log2(math.e)),
        k_gain=pad_gain(mla_k_norm[e]),
        attn_bound=(1.01 * MLA_QK * (MLA_QK ** -0.5 * math.log2(math.e))
                    * jnp.max(jnp.abs(mla_q_norm[e])) * jnp.max(jnp.abs(mla_k_norm[e]))
                    ).reshape(1).astype(F32),
        rw=rw,
        ln_w=rwkv_ln_w[e].reshape(1, RW_WIDTH), ln_b=rwkv_ln_b[e].reshape(1, RW_WIDTH),
        w_out=ev_w_out[e].astype(BF16),
    )


def _rwkv_branch(st_c, st, want_ctx):
    bsz = st["r"].shape[0]
    zero_state = jnp.zeros((bsz, 2 * RW_HEAD, RW_WIDTH), F32)
    of_c, ob_c, s_c = _rwkv_scan(st_c, zero_state)
    o_f, o_b, _ = _rwkv_scan(st, s_c)
    return [o_f, o_b], ([of_c, ob_c] if want_ctx else None)


def _even_layer(x, xc, mod, mod_c, g_norm, wts, rope_tabs, ctx_out):
    shift, scale1p, gate = mod
    shift_c, scale1p_c, gate_c = mod_c
    q, k, v, p_rw, g = _even_in(x, g_norm, scale1p, shift, wts, rope_tabs)
    qc, kc, vc, pc_rw, gc = _even_in(xc, g_norm, scale1p_c, shift_c, wts, None)
    o_mla = _attention(q, [k, kc], [v, vc], wts["attn_bound"])
    st_c = _rw_prep(pc_rw, wts["rw"])
    st = _rw_prep(p_rw, wts["rw"])
    outs, outs_c = _rwkv_branch(st_c, st, ctx_out)
    seg = wts["rw"]["seg"]
    x_new = _even_out(x, gate, o_mla, outs[0], outs[1], st["bonus"], g, wts["ln_w"], wts["ln_b"],
                      seg, wts["w_out"])
    if not ctx_out:
        return x_new, None
    oc_mla = _attention(qc, [kc], [vc], wts["attn_bound"])
    xc_new = _even_out(xc, gate_c, oc_mla, outs_c[0], outs_c[1], st_c["bonus"], gc, wts["ln_w"],
                       wts["ln_b"], seg, wts["w_out"])
    return x_new, xc_new


def _hyena_layer(x, mod, g_norm, wts, tables):
    shift, scale1p, gate = mod
    n_pos = x.shape[1]
    u, g = _norm_proj(x, g_norm, scale1p, shift, wts["proj"])
    v_even, v_odd, x0 = _conv3(u, wts["conv_w"], wts["conv_b"])
    spec = _spectrum(_hyena_filters(n_pos, *wts["filt"]), tables)
    y_even, y_odd = _dft_conv(v_even, v_odd, spec, tables)
    return _hy_out(x, gate, y_even, y_odd, v_even, v_odd, x0, g, wts["bias_d"], wts["w_out"])


def kernel(x, c, ctx, c_ctx, mod_w, mod_b, norm_g, ev_w_in, ev_w_out, mla_q_a_norm, mla_w_uq, mla_kv_a_norm, mla_w_ukv, mla_q_norm, mla_k_norm, rwkv_mu_prev, rwkv_mu_next, rwkv_w0, rwkv_w_up, rwkv_a0, rwkv_a_up, rwkv_k_k, rwkv_k_a, rwkv_r_k, rwkv_ln_w, rwkv_ln_b, od_w_in, od_w_out, hy_conv_w, hy_conv_b, hy_bias_d, hy_f_w1, hy_f_b1, hy_f_w2, hy_f_b2, hy_f_wout, hy_freq):
    bsz, n_lat, d = x.shape
    n_ctx = ctx.shape[1]
    scan_rows = SCAN_CHUNKS * CHUNK
    assert n_lat % max(scan_rows, GRID_W) == 0 and n_ctx % scan_rows == 0 and d == D_MODEL
    assert CHUNK == RW_HEAD and 2 * CHUNK == RW_PAIR

    n_rows = -(-(bsz + 1) // 16) * 16
    cvec = jnp.zeros((n_rows, d), F32).at[:bsz].set(c).at[bsz].set(c_ctx)
    mods = _modulation(cvec, mod_w, mod_b)

    def split_mod(i, lo, hi, reps):
        m = mods[i, lo:hi]
        m = jnp.broadcast_to(m, (reps, 3 * d)) if hi - lo == 1 else m
        m = m[:, None, :]
        return m[..., :d], 1.0 + m[..., d:2 * d], m[..., 2 * d:]

    rope_tabs = _rope_tables(n_lat)
    deltas = jnp.abs(jnp.linspace(math.log(HY_TARGET) / HY_FAST_DECAY,
                                  math.log(HY_TARGET) / HY_SLOW_DECAY, HY_WIDTH,
                                  dtype=F32)).reshape(1, HY_WIDTH)
    tables = {n_lat: _dft_tables(n_lat)}

    xc = ctx
    for i in range(DEPTH):
        ctx_needed_later = any(j > i and j % 2 == 0 for j in range(DEPTH))
        mod = split_mod(i, 0, bsz, bsz)
        mod_c = split_mod(i, bsz, bsz + 1, bsz)
        if i % 2 == 0:
            wts = _even_weights(i // 2, ev_w_in, ev_w_out, mla_q_a_norm, mla_w_uq, mla_kv_a_norm,
                                mla_w_ukv, mla_q_norm, mla_k_norm, rwkv_mu_prev, rwkv_mu_next,
                                rwkv_w0, rwkv_w_up, rwkv_a0, rwkv_a_up, rwkv_k_k, rwkv_k_a,
                                rwkv_r_k, rwkv_ln_w, rwkv_ln_b)
            x, xc_new = _even_layer(x, xc, mod, mod_c, norm_g[i], wts, rope_tabs, ctx_needed_later)
            xc = xc_new if ctx_needed_later else xc
        else:
            o = i // 2
            w_in = od_w_in[o]
            wts = dict(
                proj=[w_in[:, :3 * HY_WIDTH].astype(BF16), w_in[:, 3 * HY_WIDTH:].astype(BF16)],
                conv_w=hy_conv_w[o], conv_b=hy_conv_b[o], bias_d=hy_bias_d[o],
                filt=(hy_f_w1[o], hy_f_b1[o], hy_f_w2[o], hy_f_b2[o], hy_f_wout[o], hy_freq[o],
                      deltas),
                w_out=od_w_out[o].astype(BF16),
            )
            if ctx_needed_later:
                if n_ctx not in tables:
                    tables[n_ctx] = _dft_tables(n_ctx)
                xc = _hyena_layer(xc, mod_c, norm_g[i], wts, tables[n_ctx])
            x = _hyena_layer(x, mod, norm_g[i], wts, tables[n_lat])
    return x
```

```python
import functools
import math

import numpy as np
import jax
import jax.numpy as jnp
from jax import lax
from jax.experimental import pallas as pl
from jax.experimental.pallas import tpu as pltpu

F32 = jnp.float32
BF16 = jnp.bfloat16
HIGHEST = lax.Precision.HIGHEST

D_MODEL = 1024
DEPTH = 4
GRID_W = 64
NORM_EPS = 1e-6
MLA_HEADS = 8
MLA_NOPE = 64
MLA_ROPE = 32
MLA_QK = MLA_NOPE + MLA_ROPE
MLA_V = 64
MLA_Q_LORA = 256
MLA_KV_LORA = 128
MLA_WIDTH = MLA_HEADS * MLA_V
ROPE_BASE = 10000.0
RW_HEADS = 8
RW_HEAD = 64
RW_WIDTH = RW_HEADS * RW_HEAD
RW_LORA_W = 64
RW_LORA_A = 64
RW_SHIFT = 3 * RW_WIDTH + 2 * RW_LORA_W + 2 * RW_LORA_A
RW_GN_EPS = 64e-5
EV_DQ = MLA_Q_LORA
EV_DKV = MLA_KV_LORA + MLA_ROPE
HY_WIDTH = D_MODEL
HY_ORDER = 64
HY_BANDS = 16
HY_EMB = 1 + 2 * HY_BANDS
HY_INNER = 2
HY_FAST_DECAY = 0.3
HY_SLOW_DECAY = 1.5
HY_TARGET = 1e-2

LANES = 128
HEAD_PAD = 128
RW_PAIR = 2 * RW_HEAD
N_PAIRS = RW_WIDTH // RW_PAIR
CHUNK = 64
SCAN_CHUNKS = 4
ROW_TILE = 256
ATTN_TQ = 256
ATTN_TK = 256
ATTN_SAFE_BITS = 40.0
DFT_TC = 256
DFT_TM = 256
HALO_ROWS = 16
EVEN_IN_TILE = 256
SUB_ROWS = 128
VMEM_LIMIT = 56 * 1024 * 1024

_NN = (((1,), (0,)), ((), ()))
_NT = (((1,), (1,)), ((), ()))


def _mm(a, b, dn=_NN, mode="bf16"):
    if mode == "f32":
        return lax.dot_general(a, b, dn, precision=HIGHEST, preferred_element_type=F32)
    dg = functools.partial(lax.dot_general, dimension_numbers=dn, preferred_element_type=F32)
    ah = a.astype(BF16)
    bh = b.astype(BF16)
    if mode == "rhs3":
        r1 = b - bh.astype(F32)
        bm = r1.astype(BF16)
        bl = (r1 - bm.astype(F32)).astype(BF16)
        return dg(ah, bh) + (dg(ah, bm) + dg(ah, bl))
    if mode == "bf16":
        return dg(ah, bh)
    al = (a - ah.astype(F32)).astype(BF16)
    if mode == "lhs2":
        return dg(ah, bh) + dg(al, bh)
    bl = (b - bh.astype(F32)).astype(BF16)
    return dg(ah, bh) + (dg(ah, bl) + dg(al, bh))


def _cparams(*sem):
    return pltpu.CompilerParams(dimension_semantics=sem, vmem_limit_bytes=VMEM_LIMIT)


def _silu(t):
    return t * jax.nn.sigmoid(t)


def _shifted_rows(pb, prev_ref, next_ref, on_mxu):
    tm = pb.shape[0]
    i = pl.program_id(1)
    last = pl.num_programs(1) - 1
    prev_row = jnp.where(i > 0, prev_ref[0, HALO_ROWS - 1:HALO_ROWS, :].astype(F32), 0.0)
    next_row = jnp.where(i < last, next_ref[0, 0:1, :].astype(F32), 0.0)
    if on_mxu:
        out_row = lax.broadcasted_iota(jnp.int32, (2 * tm, tm), 0)
        src_row = lax.broadcasted_iota(jnp.int32, (2 * tm, tm), 1)
        want = jnp.where(out_row < tm, out_row - 1, out_row - tm + 1)
        both = jnp.dot(jnp.where(src_row == want, 1.0, 0.0).astype(BF16), pb,
                       preferred_element_type=F32)
        down, up = both[:tm], both[tm:]
    else:
        p = pb.astype(F32)
        down, up = pltpu.roll(p, 1, axis=0), pltpu.roll(p, tm - 1, axis=0)
    sub = lax.broadcasted_iota(jnp.int32, (8, 1), 0)
    prev = jnp.concatenate([jnp.where(sub == 0, prev_row, down[:8]), down[8:]], axis=0)
    nxt = jnp.concatenate([up[:tm - 8], jnp.where(sub == 7, next_row, up[tm - 8:])], axis=0)
    return prev, nxt


def _split_rows(val, scr):
    rows, width = val.shape
    half = rows // 2
    for j in range(width // LANES):
        scr[j] = val[:, j * LANES:(j + 1) * LANES]
    pick = lambda start: jnp.concatenate(
        [scr[j, pl.ds(start, half, stride=2), :] for j in range(width // LANES)], axis=1)
    return pick(0), pick(1)


def _merge_rows(even, odd, scr):
    half, width = even.shape
    for j in range(width // LANES):
        cols = slice(j * LANES, (j + 1) * LANES)
        scr[j, pl.ds(0, half, stride=2), :] = even[:, cols]
        scr[j, pl.ds(1, half, stride=2), :] = odd[:, cols]
    return jnp.concatenate([scr[j] for j in range(width // LANES)], axis=1)


def _row_split_scratch(rows, width):
    return pltpu.VMEM((width // LANES, rows, LANES), F32)


def _halo_specs(tm, width, n_rows):
    th = tm // HALO_ROWS
    last_h = n_rows // HALO_ROWS - 1
    main = pl.BlockSpec((1, tm, width), lambda b, i: (b, i, 0))
    prev = pl.BlockSpec((1, HALO_ROWS, width), lambda b, i: (b, jnp.maximum(i * th - 1, 0), 0))
    nxt = pl.BlockSpec((1, HALO_ROWS, width),
                       lambda b, i: (b, jnp.minimum((i + 1) * th, last_h), 0))
    return main, prev, nxt


def _mod_kernel(c_ref, w_ref, b_ref, o_ref):
    o_ref[0] = _mm(_silu(c_ref[...]), w_ref[0], mode="x3") + b_ref[0]


def _modulation(cvec, mod_w, mod_b):
    rows, d = cvec.shape
    n = mod_w.shape[-1]
    tn = 1024
    return pl.pallas_call(
        _mod_kernel,
        grid=(DEPTH, n // tn),
        in_specs=[pl.BlockSpec((rows, d), lambda i, j: (0, 0)),
                  pl.BlockSpec((1, d, tn), lambda i, j: (i, 0, j)),
                  pl.BlockSpec((1, 1, tn), lambda i, j: (i, 0, j))],
        out_specs=pl.BlockSpec((1, rows, tn), lambda i, j: (i, 0, j)),
        out_shape=jax.ShapeDtypeStruct((DEPTH, rows, n), F32),
        compiler_params=_cparams("arbitrary", "arbitrary"),
        name="modulation",
    )(cvec, mod_w, mod_b.reshape(DEPTH, 1, n))


def _norm_proj_kernel(nw, x_ref, g_ref, sc_ref, sh_ref, *refs):
    x = x_ref[0]
    h = x * lax.rsqrt(jnp.mean(x * x, axis=-1, keepdims=True) + NORM_EPS) * g_ref[...]
    hb = (h * sc_ref[0] + sh_ref[0]).astype(BF16)
    for w_ref, o_ref in zip(refs[:nw], refs[nw:]):
        o_ref[0] = jnp.dot(hb, w_ref[...], preferred_element_type=F32).astype(BF16)


def _norm_proj(x, g, scale1p, shift, weights):
    bsz, n_rows, d = x.shape
    tm = min(ROW_TILE, n_rows)
    vec = pl.BlockSpec((1, 1, d), lambda b, i: (b, 0, 0))
    in_specs = [pl.BlockSpec((1, tm, d), lambda b, i: (b, i, 0)),
                pl.BlockSpec((1, d), lambda b, i: (0, 0)), vec, vec]
    in_specs += [pl.BlockSpec(w.shape, lambda b, i: (0, 0)) for w in weights]
    return pl.pallas_call(
        functools.partial(_norm_proj_kernel, len(weights)),
        grid=(bsz, n_rows // tm),
        in_specs=in_specs,
        out_specs=[pl.BlockSpec((1, tm, w.shape[1]), lambda b, i: (b, i, 0)) for w in weights],
        out_shape=[jax.ShapeDtypeStruct((bsz, n_rows, w.shape[1]), BF16) for w in weights],
        compiler_params=_cparams("parallel", "arbitrary"),
        name="norm_proj",
    )(x, g.reshape(1, d), scale1p, shift, *weights)


def _head_norm_rope(t, gain, tabs):
    ms = jnp.sum(t * t, axis=-1, keepdims=True) * (1.0 / MLA_QK)
    t = t * lax.rsqrt(ms + NORM_EPS) * gain
    if tabs is not None:
        cos_f, sin_a, sin_b = tabs
        t = (t * cos_f + pltpu.roll(t, MLA_ROPE // 2, axis=1) * sin_a
             + pltpu.roll(t, HEAD_PAD - MLA_ROPE // 2, axis=1) * sin_b)
    return t


def _even_in_kernel(rope, x_ref, g_ref, sc_ref, sh_ref, wdq_ref, wdkv_ref, wrw_ref, wg_ref,
                    qan_ref, wuq_ref, qgn_ref, kan_ref, wk_ref, wv_ref, vone_ref, kgn_ref, *refs):
    q_ref, k_ref, v_ref, prw_ref, go_ref = refs[-5:]
    q_gain, k_gain = qgn_ref[...], kgn_ref[...]

    def rms(t, gain_ref):
        return t * lax.rsqrt(jnp.mean(t * t, axis=-1, keepdims=True) + NORM_EPS) * gain_ref[...]

    tm = x_ref.shape[1]
    sub = min(SUB_ROWS, tm)
    for r0 in range(0, tm, sub):
        rows = slice(r0, r0 + sub)
        tabs = tuple(r[rows, :] for r in refs[:3]) if rope else None
        hb = (rms(x_ref[0, rows, :], g_ref) * sc_ref[0] + sh_ref[0]).astype(BF16)
        prw_ref[0, rows, :] = jnp.dot(hb, wrw_ref[...], preferred_element_type=F32).astype(BF16)
        go_ref[0, rows, :] = jnp.dot(hb, wg_ref[...], preferred_element_type=F32).astype(BF16)
        p_dq = jnp.dot(hb, wdq_ref[...], preferred_element_type=F32)
        q = jnp.dot(rms(p_dq, qan_ref).astype(BF16), wuq_ref[...], preferred_element_type=F32)
        p_dkv = jnp.dot(hb, wdkv_ref[...], preferred_element_type=F32)
        k_rope = p_dkv[:, MLA_KV_LORA:]
        ab = rms(p_dkv[:, :MLA_KV_LORA], kan_ref).astype(BF16)
        k_nope = jnp.dot(ab, wk_ref[...], preferred_element_type=F32)
        v_ref[0, rows, :] = (jnp.dot(ab, wv_ref[...], preferred_element_type=F32)
                             + vone_ref[...]).astype(BF16)
        for hd in range(MLA_HEADS):
            sl = slice(hd * HEAD_PAD, (hd + 1) * HEAD_PAD)
            q_ref[0, rows, sl] = _head_norm_rope(q[:, sl], q_gain, tabs).astype(BF16)
            k_ref[0, rows, sl] = _head_norm_rope(k_nope[:, sl] + k_rope, k_gain,
                                                 tabs).astype(BF16)


def _even_in(x, g, scale1p, shift, wts, tabs):
    bsz, n_rows, d = x.shape
    tm = min(EVEN_IN_TILE, n_rows)
    rope = tabs is not None
    kw = MLA_HEADS * HEAD_PAD
    vec = pl.BlockSpec((1, 1, d), lambda b, i: (b, 0, 0))
    full = lambda a: pl.BlockSpec(a.shape, lambda b, i: (0,) * a.ndim)
    consts = list(wts["proj"]) + [wts["q_a_norm"].reshape(1, MLA_Q_LORA), wts["w_uq"], wts["q_gain"],
                                  wts["kv_a_norm"].reshape(1, MLA_KV_LORA), wts["w_k"], wts["w_v"],
                                  wts["v_ones"], wts["k_gain"]]
    in_specs = [pl.BlockSpec((1, tm, d), lambda b, i: (b, i, 0)),
                pl.BlockSpec((1, d), lambda b, i: (0, 0)), vec, vec] + [full(a) for a in consts]
    args = [x, g.reshape(1, d), scale1p, shift] + consts
    if rope:
        in_specs += [pl.BlockSpec((tm, HEAD_PAD), lambda b, i: (i, 0))] * 3
        args += list(tabs)
    widths = [kw, kw, kw, RW_SHIFT, d]
    return pl.pallas_call(
        functools.partial(_even_in_kernel, rope),
        grid=(bsz, n_rows // tm),
        in_specs=in_specs,
        out_specs=[pl.BlockSpec((1, tm, w), lambda b, i: (b, i, 0)) for w in widths],
        out_shape=[jax.ShapeDtypeStruct((bsz, n_rows, w), BF16) for w in widths],
        compiler_params=_cparams("parallel", "arbitrary"),
        name="even_in",
    )(*args)


def _attn_body(q_ref, k_refs, v_refs, o_ref, bound):
    tq = q_ref.shape[1]
    lane = lax.broadcasted_iota(jnp.int32, (tq, LANES), 1)
    for hp in range(MLA_HEADS // 2):
        psl = slice(2 * hp * HEAD_PAD, (2 * hp + 2) * HEAD_PAD)
        outs = []
        for h in (2 * hp, 2 * hp + 1):
            sl = slice(h * HEAD_PAD, (h + 1) * HEAD_PAD)
            q = q_ref[0, :, sl]
            score = lambda k_ref, k0, k1: lax.dot_general(q, k_ref[0, k0:k1, sl], _NT,
                                                          preferred_element_type=F32)
            if bound is None:
                ss = [score(k_ref, 0, k_ref.shape[1]) for k_ref in k_refs]
                m = functools.reduce(jnp.maximum, [jnp.max(s, axis=-1, keepdims=True) for s in ss])
            acc = None
            for seg, (k_ref, v_ref) in enumerate(zip(k_refs, v_refs)):
                for k0 in range(0, k_ref.shape[1], ATTN_TK):
                    k1 = k0 + ATTN_TK
                    p = (jnp.exp2(ss[seg][:, k0:k1] - m) if bound is None
                         else jnp.exp2(score(k_ref, k0, k1) - bound))
                    pv = jnp.dot(p.astype(BF16), v_ref[0, k0:k1, psl], preferred_element_type=F32)
                    acc = pv if acc is None else pv + acc
            acc = acc[:, :HEAD_PAD] if h % 2 == 0 else acc[:, HEAD_PAD:]
            ones_lane = MLA_V if h % 2 == 0 else 0
            outs.append(acc * (1.0 / acc[:, ones_lane:ones_lane + 1]))
        o_ref[0, :, hp * LANES:(hp + 1) * LANES] = jnp.where(lane < MLA_V, outs[0],
                                                             outs[1]).astype(BF16)


def _attn_kernel(nseg, bound_ref, q_ref, *refs):
    k_refs, v_refs, o_ref = refs[:nseg], refs[nseg:2 * nseg], refs[2 * nseg]
    bound = bound_ref[0]

    @pl.when(bound <= ATTN_SAFE_BITS)
    def _():
        _attn_body(q_ref, k_refs, v_refs, o_ref, bound)

    @pl.when(bound > ATTN_SAFE_BITS)
    def _():
        _attn_body(q_ref, k_refs, v_refs, o_ref, None)


def _attention(q, ks, vs, bound):
    bsz, n_q, qw = q.shape
    tq = min(ATTN_TQ, n_q)
    nseg = len(ks)
    in_specs = [pl.BlockSpec(memory_space=pltpu.SMEM),
                pl.BlockSpec((1, tq, qw), lambda b, i: (b, i, 0))]
    in_specs += [pl.BlockSpec((1,) + k.shape[1:], lambda b, i: (b, 0, 0)) for k in ks]
    in_specs += [pl.BlockSpec((1,) + v.shape[1:], lambda b, i: (b, 0, 0)) for v in vs]
    return pl.pallas_call(
        functools.partial(_attn_kernel, nseg),
        grid=(bsz, n_q // tq),
        in_specs=in_specs,
        out_specs=pl.BlockSpec((1, tq, MLA_WIDTH), lambda b, i: (b, i, 0)),
        out_shape=jax.ShapeDtypeStruct((bsz, n_q, MLA_WIDTH), BF16),
        compiler_params=_cparams("parallel", "arbitrary"),
        name="mla_attention",
    )(bound, q, *ks, *vs)


def _rw_prep_kernel(p_ref, prev_ref, next_ref, mup_ref, mun_ref, w0_ref, wup_ref, a0_ref, aup_ref,
                    kk_ref, ka_ref, rk_ref, seg_ref,
                    r_out, v_out, kkn_out, bonus_out, lw0_out, lw1_out, b0_out, b1_out,
                    kd0_out, kd1_out):
    pb = p_ref[0]
    p = pb.astype(F32)
    prev, nxt = _shifted_rows(pb, prev_ref, next_ref, on_mxu=False)
    mu_p, mu_n = mup_ref[...], mun_ref[...]
    ps = p * (1.0 - mu_p - mu_n) + prev * mu_p + nxt * mu_n
    w = RW_WIDTH
    r, k, v = ps[:, :w], ps[:, w:2 * w], ps[:, 2 * w:3 * w]
    wd = jnp.tanh(ps[:, 3 * w:3 * w + 2 * RW_LORA_W])
    ad = ps[:, 3 * w + 2 * RW_LORA_W:]
    seg = seg_ref[...]
    kq = k * kk_ref[...]
    kk = kq * lax.rsqrt(jnp.maximum(_mm(kq * kq, seg, mode="lhs2"), 1e-24))
    r_out[0] = r.astype(BF16)
    v_out[0] = v.astype(BF16)
    kkn_out[0] = kk.astype(BF16)
    bonus_in = jnp.zeros_like(r)
    for d, (lw_out, b_out, kd_out) in enumerate(((lw0_out, b0_out, kd0_out),
                                                  (lw1_out, b1_out, kd1_out))):
        z = w0_ref[d:d + 1, :] + _mm(wd, wup_ref[d], mode="x3")
        lw_out[0] = -math.exp(-0.5) * jax.nn.sigmoid(z)
        a = jax.nn.sigmoid(a0_ref[d:d + 1, :] + _mm(ad, aup_ref[d], mode="x3"))
        kd = k * (1.0 + (a - 1.0) * ka_ref[...])
        b_out[0] = (kk * a).astype(BF16)
        kd_out[0] = kd.astype(BF16)
        bonus_in = bonus_in + r * kd * rk_ref[d:d + 1, :]
    bonus_out[0] = (_mm(bonus_in, seg, mode="lhs2") * v).astype(BF16)


def _rw_prep(p_rw, prm):
    bsz, n_rows, width = p_rw.shape
    tm = min(ROW_TILE, n_rows)
    main, prev, nxt = _halo_specs(tm, width, n_rows)
    full = lambda a: pl.BlockSpec(a.shape, lambda b, i: (0,) * a.ndim)
    consts = [prm["mu_prev"], prm["mu_next"], prm["w0"], prm["w_up"], prm["a0"], prm["a_up"],
              prm["k_k"], prm["k_a"], prm["r_k"], prm["seg"]]
    out_spec = pl.BlockSpec((1, tm, RW_WIDTH), lambda b, i: (b, i, 0))
    shape = lambda dt: jax.ShapeDtypeStruct((bsz, n_rows, RW_WIDTH), dt)
    outs = pl.pallas_call(
        _rw_prep_kernel,
        grid=(bsz, n_rows // tm),
        in_specs=[main, prev, nxt] + [full(a) for a in consts],
        out_specs=[out_spec] * 10,
        out_shape=[shape(BF16)] * 4 + [shape(F32)] * 2 + [shape(BF16)] * 4,
        compiler_params=_cparams("parallel", "arbitrary"),
        name="rwkv_prep",
    )(p_rw, p_rw, p_rw, *consts)
    r, v, kk, bonus, lw0, lw1, b0, b1, kd0, kd1 = outs
    return dict(r=r, v=v, kk=kk, bonus=bonus, lw=(lw0, lw1), b=(b0, b1), kd=(kd0, kd1))


def _rwkv_scan_kernel(rf_ref, vf_ref, kkf_ref, lw0_ref, b0_ref, kd0_ref,
                      rb_ref, vb_ref, kkb_ref, lw1_ref, b1_ref, kd1_ref, s0_ref,
                      of_ref, ob_ref, sfin_ref, st_ref):
    j = pl.program_id(1)

    @pl.when(j == 0)
    def _():
        st_ref[...] = s0_ref[0]

    c = CHUNK
    c2 = 2 * c
    nsub = rf_ref.shape[1] // c
    dir_refs = ((rf_ref, vf_ref, kkf_ref, lw0_ref, b0_ref, kd0_ref, of_ref),
                (rb_ref, vb_ref, kkb_ref, lw1_ref, b1_ref, kd1_ref, ob_ref))
    ti = lax.broadcasted_iota(jnp.int32, (c, c), 0)
    si = lax.broadcasted_iota(jnp.int32, (c, c), 1)
    tp = lax.broadcasted_iota(jnp.int32, (c, c2), 0)
    sp = lax.broadcasted_iota(jnp.int32, (c, c2), 1) & (c - 1)
    eye_p = jnp.where(tp == sp, 1.0, 0.0)
    head0 = lax.broadcasted_iota(jnp.int32, (c, RW_PAIR), 1) < RW_HEAD
    masks = []
    for rev in (False, True):
        tri = jnp.where((si >= ti) if rev else (si <= ti), 1.0, 0.0)
        before = (sp > tp) if rev else (sp < tp)
        upto = (sp >= tp) if rev else (sp <= tp)
        masks.append((tri, before, upto))

    def stack(t):
        return jnp.concatenate([jnp.where(head0, t, 0.0), jnp.where(head0, 0.0, t)], axis=0)

    def unstack(t):
        return t[:c] + t[c:]

    mm = functools.partial(_mm, mode="bf16")
    items = [(d, p, s) for d in range(2) for p in range(N_PAIRS) for s in range(nsub)]
    pair_sl = lambda p: slice(p * RW_PAIR, (p + 1) * RW_PAIR)
    chunk_rows = lambda s: slice(s * c, (s + 1) * c)

    lams = [[_mm(masks[d][0], dir_refs[d][3][0, chunk_rows(s), :], mode="rhs3") for s in range(nsub)]
            for d in range(2)]

    ops = {key: {} for key in items}

    def prep(key):
        d, p, s = key
        r_ref, v_ref, kk_ref, lw_ref, b_ref, kd_ref, _ = dir_refs[d]
        sl, rs = pair_sl(p), chunk_rows(s)
        lam = lams[d][s][:, sl]
        lam_tot = lam[0:1] if d == 1 else lam[c - 1:c]
        e_neg = jnp.exp(-lam)
        e_tail = jnp.exp(lam_tot - lam)
        ld = lambda ref: ref[0, rs, sl].astype(F32)
        b, kd = ld(b_ref), ld(kd_ref)
        at = -ld(kk_ref) * jnp.exp(lam - lw_ref[0, rs, sl])
        bt_t = unstack(stack(b * e_tail).T)
        kt_t = unstack(stack(kd * e_tail).T)
        ops[key].update(
            at=at, rt=ld(r_ref) * jnp.exp(lam), at2=stack(at), v2=stack(ld(v_ref)),
            bk2=jnp.concatenate([stack(b * e_neg), stack(kd * e_neg)], axis=0),
            bkt_t=jnp.concatenate([bt_t, kt_t], axis=1), bt_t=bt_t,
            decay_tot=jnp.where(eye_p == 1.0, jnp.exp(lam_tot), 0.0))

    def interactions(key):
        o = ops[key]
        _, before, upto = masks[key[0]]
        x = mm(jnp.concatenate([o["at"], o["rt"]], axis=0), o["bk2"], _NT)
        o["a_ab"] = jnp.where(before, x[:c, :c2], 0.0)
        o["a_ak"] = jnp.where(before, x[:c, c2:], 0.0)
        o["a_rb"] = jnp.where(upto, x[c:, :c2], 0.0)
        o["a_rk"] = jnp.where(upto, x[c:, c2:], 0.0)

    def inverse_first(key):
        o = ops[key]
        o["t_inv"] = eye_p + o["a_ab"]
        o["pw"] = mm(o["a_ab"], stack(o["a_ab"]))
        o["w"] = mm(o["a_ak"], o["v2"])

    def inverse_step(key):
        o = ops[key]
        y = mm(jnp.concatenate([o["pw"], o["t_inv"]], axis=0), stack(o["pw"]))
        o["pw"] = y[:c]
        o["t_inv"] = o["t_inv"] + y[c:]

    def inverse_last(key):
        o = ops[key]
        o["t_inv"] = o["t_inv"] + mm(o["t_inv"], stack(o["pw"]))

    def transforms(key):
        o = ops[key]
        z = mm(o["t_inv"], jnp.concatenate([o["at2"], stack(o["w"])], axis=1))
        ap2, u02 = stack(z[:, :c2]), stack(z[:, c2:])
        ya = mm(jnp.concatenate([o["a_rb"], o["bt_t"]], axis=0), ap2)
        yb = mm(jnp.concatenate([jnp.concatenate([o["a_rb"], o["a_rk"]], axis=1), o["bkt_t"]], axis=0),
                jnp.concatenate([u02, o["v2"]], axis=0))
        o["rp"] = o["rt"] + ya[:c]
        o["o0"] = yb[:c]
        o["m_p"] = o["decay_tot"] + ya[c:]
        o["n_p"] = yb[c:]

    stages = ([prep, interactions, inverse_first] + [inverse_step] * (int(math.log2(c)) - 2)
              + [inverse_last, transforms])
    for stage in stages:
        for key in items:
            stage(key)

    by_item = ops
    chains =[(d, p) for d in range(2) for p in range(N_PAIRS)]
    st_rows = lambda d: slice(d * RW_HEAD, (d + 1) * RW_HEAD)
    states = {(d, p): st_ref[st_rows(d), pair_sl(p)] for d, p in chains}
    for step in range(nsub):
        for d, p in chains:
            s = step if d == 0 else nsub - 1 - step
            o = by_item[(d, p, s)]
            state2 = stack(states[(d, p)])
            dir_refs[d][6][0, chunk_rows(s), pair_sl(p)] = (mm(o["rp"], state2)
                                                            + o["o0"]).astype(BF16)
            states[(d, p)] = mm(o["m_p"], state2) + o["n_p"]
    for d, p in chains:
        st_ref[st_rows(d), pair_sl(p)] = states[(d, p)]

    @pl.when(j == pl.num_programs(1) - 1)
    def _():
        sfin_ref[0] = st_ref[...]


def _rwkv_scan(st, s0):
    bsz, n_rows, w = st["r"].shape
    rows = SCAN_CHUNKS * CHUNK
    nblk = n_rows // rows
    fwd = pl.BlockSpec((1, rows, w), lambda b, j: (b, j, 0))
    bwd = pl.BlockSpec((1, rows, w), lambda b, j: (b, nblk - 1 - j, 0))
    sspec = pl.BlockSpec((1, 2 * RW_HEAD, w), lambda b, j: (b, 0, 0))
    row_shape = jax.ShapeDtypeStruct((bsz, n_rows, w), BF16)
    return pl.pallas_call(
        _rwkv_scan_kernel,
        grid=(bsz, nblk),
        in_specs=[fwd] * 6 + [bwd] * 6 + [sspec],
        out_specs=[fwd, bwd, sspec],
        out_shape=[row_shape, row_shape, jax.ShapeDtypeStruct((bsz, 2 * RW_HEAD, w), F32)],
        scratch_shapes=[pltpu.VMEM((2 * RW_HEAD, w), F32)],
        compiler_params=_cparams("parallel", "arbitrary"),
        name="rwkv_scan",
    )(st["r"], st["v"], st["kk"], st["lw"][0], st["b"][0], st["kd"][0],
      st["r"], st["v"], st["kk"], st["lw"][1], st["b"][1], st["kd"][1], s0)


def _even_out_kernel(x_ref, gate_ref, om_ref, of_ref, ob_ref, bonus_ref, g_ref, lnw_ref, lnb_ref,
                     seg_ref, w_ref, o_ref):
    seg = seg_ref[...]
    f32 = lambda ref: ref[0].astype(F32)
    o = f32(of_ref) + f32(ob_ref)
    mu = _mm(o, seg, mode="lhs2") * (1.0 / RW_HEAD)
    dlt = o - mu
    var = _mm(dlt * dlt, seg, mode="lhs2") * (1.0 / RW_HEAD)
    o_rw = dlt * lax.rsqrt(var + RW_GN_EPS) * lnw_ref[...] + lnb_ref[...] + f32(bonus_ref)
    g = f32(g_ref)
    z_m = (f32(om_ref) * _silu(g[:, :MLA_WIDTH])).astype(BF16)
    z_r = (o_rw * _silu(g[:, MLA_WIDTH:])).astype(BF16)
    y = (jnp.dot(z_m, w_ref[:MLA_WIDTH, :], preferred_element_type=F32)
         + jnp.dot(z_r, w_ref[MLA_WIDTH:, :], preferred_element_type=F32))
    o_ref[0] = x_ref[0] + gate_ref[0] * y


def _even_out(x, gate, o_mla, o_f, o_b, bonus, g, ln_w, ln_b, seg, w_out):
    bsz, n_rows, d = x.shape
    tm = min(ROW_TILE, n_rows)
    rows = lambda width: pl.BlockSpec((1, tm, width), lambda b, i: (b, i, 0))
    full = lambda a: pl.BlockSpec(a.shape, lambda b, i: (0,) * a.ndim)
    return pl.pallas_call(
        _even_out_kernel,
        grid=(bsz, n_rows // tm),
        in_specs=[rows(d), pl.BlockSpec((1, 1, d), lambda b, i: (b, 0, 0)),
                  rows(MLA_WIDTH), rows(RW_WIDTH), rows(RW_WIDTH), rows(RW_WIDTH), rows(d),
                  full(ln_w), full(ln_b), full(seg), full(w_out)],
        out_specs=rows(d),
        out_shape=jax.ShapeDtypeStruct(x.shape, F32),
        compiler_params=_cparams("parallel", "arbitrary"),
        name="even_out",
    )(x, gate, o_mla, o_f, o_b, bonus, g, ln_w, ln_b, seg, w_out)


def _conv3_kernel(u_ref, prev_ref, next_ref, w_ref, b_ref, ve_ref, vo_ref, x0_ref, v_scr):
    ub = u_ref[0]
    u = ub.astype(F32)
    prev, nxt = _shifted_rows(ub, prev_ref, next_ref, on_mxu=True)
    cv = prev * w_ref[0:1, :] + u * w_ref[1:2, :] + nxt * w_ref[2:3, :] + b_ref[...]
    hw = HY_WIDTH
    x0_ref[0] = cv[:, :hw].astype(BF16)
    v_even, v_odd = _split_rows(cv[:, 2 * hw:] * cv[:, hw:2 * hw], v_scr)
    ve_ref[0] = v_even.astype(BF16)
    vo_ref[0] = v_odd.astype(BF16)


def _conv3(u, conv_w, conv_b):
    bsz, n_rows, width = u.shape
    tm = min(ROW_TILE, n_rows)
    main, prev, nxt = _halo_specs(tm, width, n_rows)
    half_spec = pl.BlockSpec((1, tm // 2, HY_WIDTH), lambda b, i: (b, i, 0))
    half_shape = jax.ShapeDtypeStruct((bsz, n_rows // 2, HY_WIDTH), BF16)
    return pl.pallas_call(
        _conv3_kernel,
        grid=(bsz, n_rows // tm),
        in_specs=[main, prev, nxt,
                  pl.BlockSpec(conv_w.shape, lambda b, i: (0, 0)),
                  pl.BlockSpec((1, width), lambda b, i: (0, 0))],
        out_specs=[half_spec, half_spec, pl.BlockSpec((1, tm, HY_WIDTH), lambda b, i: (b, i, 0))],
        out_shape=[half_shape, half_shape, jax.ShapeDtypeStruct((bsz, n_rows, HY_WIDTH), BF16)],
        scratch_shapes=[_row_split_scratch(tm, HY_WIDTH)],
        compiler_params=_cparams("parallel", "arbitrary"),
        name="hyena_conv3",
    )(u, u, u, conv_w, conv_b.reshape(1, width))


def _hy_hidden_kernel(w1_ref, b1_ref, w2_ref, b2_ref, freq_ref, h_ref):
    n_pos = h_ref.shape[0]
    pos = lax.broadcasted_iota(jnp.int32, (n_pos, 1), 0).astype(F32)
    lane = lax.broadcasted_iota(jnp.int32, (1, LANES), 1)
    band_idx = jnp.where(lane <= HY_BANDS, lane - 1, lane - 1 - HY_BANDS).astype(F32)
    band = 1e-4 + band_idx * ((HY_BANDS - 1 - 1e-4) / (HY_BANDS - 1))
    ang = pos * (2.0 * math.pi / n_pos) * band
    z = jnp.where(lane == 0, pos / (n_pos - 1),
                  jnp.where(lane <= HY_BANDS, jnp.cos(ang),
                            jnp.where(lane <= 2 * HY_BANDS, -jnp.sin(ang), 0.0)))
    freq = freq_ref[...]
    hdn = jnp.sin(freq * (_mm(z, w1_ref[...], mode="f32") + b1_ref[...]))
    for j in range(HY_INNER):
        hdn = jnp.sin(freq * (_mm(hdn, w2_ref[j], mode="f32") + b2_ref[j]))
    h_ref[...] = hdn


def _hy_filter_kernel(h_ref, w0_ref, w1_ref, dl_ref, fse_ref, fso_ref, fde_ref, fdo_ref, scr):
    n_pos = h_ref.shape[0]
    pos = lax.broadcasted_iota(jnp.int32, (n_pos, 1), 0)
    t = pos.astype(F32) / (n_pos - 1)
    dec = jnp.exp(-t * dl_ref[...])
    hdn = h_ref[...]
    f_fwd = _mm(hdn, w0_ref[...], mode="f32") * dec
    f_bwd = jnp.where(pos == 0, 0.0, _mm(hdn, w1_ref[...], mode="f32") * dec)
    inv = 1.0 / (jnp.sum(jnp.abs(f_fwd), axis=0, keepdims=True)
                 + jnp.sum(jnp.abs(f_bwd), axis=0, keepdims=True))
    fse_ref[...], fso_ref[...] = _split_rows((f_fwd + f_bwd) * inv, scr)
    fde_ref[...], fdo_ref[...] = _split_rows((f_bwd - f_fwd) * inv, scr)


def _hyena_filters(n_pos, f_w1, f_b1, f_w2, f_b2, f_wout, freq, deltas):
    w1 = jnp.zeros((LANES, HY_ORDER), F32).at[:HY_EMB].set(f_w1)
    hdn = pl.pallas_call(
        _hy_hidden_kernel,
        out_shape=jax.ShapeDtypeStruct((n_pos, HY_ORDER), F32),
        name="hyena_filter_hidden",
    )(w1, f_b1.reshape(1, HY_ORDER), f_w2, f_b2.reshape(HY_INNER, 1, HY_ORDER),
      freq.reshape(1, HY_ORDER))
    tn = 256
    cspec = pl.BlockSpec((HY_ORDER, tn), lambda j: (0, j))
    ospec = pl.BlockSpec((n_pos // 2, tn), lambda j: (0, j))
    oshape = jax.ShapeDtypeStruct((n_pos // 2, HY_WIDTH), F32)
    return pl.pallas_call(
        _hy_filter_kernel,
        grid=(HY_WIDTH // tn,),
        in_specs=[pl.BlockSpec((n_pos, HY_ORDER), lambda j: (0, 0)), cspec, cspec,
                  pl.BlockSpec((1, tn), lambda j: (0, j))],
        out_specs=[ospec] * 4,
        out_shape=[oshape] * 4,
        scratch_shapes=[_row_split_scratch(n_pos, tn)],
        compiler_params=_cparams("arbitrary"),
        name="hyena_filter",
    )(hdn, f_wout[:, :HY_WIDTH], f_wout[:, HY_WIDTH:], deltas)


def _dft_tables_kernel(ec_ref, es_ref, oc_ref, os_ref, oct_ref, ost_ref, base_scr):
    tr, kq = ec_ref.shape
    n = 4 * kq
    r0 = pl.program_id(0) * tr

    def cos_sin(phase):
        ph = phase & (n - 1)
        ph = jnp.where(ph >= n // 2, ph - n, ph)
        ang = ph.astype(F32) * (2.0 * math.pi / n)
        return jnp.cos(ang), jnp.sin(ang)

    @pl.when(pl.program_id(0) == 0)
    def _():
        rr = lax.broadcasted_iota(jnp.int32, (tr, kq), 0)
        c = lax.broadcasted_iota(jnp.int32, (tr, kq), 1)
        for i, phase in enumerate((2 * rr * c, rr * (2 * c + 1), c * (2 * rr + 1))):
            base_scr[2 * i], base_scr[2 * i + 1] = cos_sin(phase)

    c1 = lax.broadcasted_iota(jnp.int32, (1, kq), 1)
    cos_e, sin_e = cos_sin(2 * r0 * c1)
    cos_o, sin_o = cos_sin(r0 * (2 * c1 + 1))
    outs = ((ec_ref, es_ref, cos_e, sin_e), (oc_ref, os_ref, cos_o, sin_o),
            (oct_ref, ost_ref, cos_e, sin_e))
    for i, (c_ref, s_ref, cos_a, sin_a) in enumerate(outs):
        cos_r, sin_r = base_scr[2 * i], base_scr[2 * i + 1]
        c_ref[...] = (cos_a * cos_r - sin_a * sin_r).astype(BF16)
        s_ref[...] = (sin_a * cos_r + cos_a * sin_r).astype(BF16)


def _dft_tables(n_pos):
    kq = n_pos // 2
    tr = min(256, kq)
    spec = pl.BlockSpec((tr, kq), lambda i: (i, 0))
    shape = jax.ShapeDtypeStruct((kq, kq), BF16)
    return pl.pallas_call(
        _dft_tables_kernel,
        grid=(kq // tr,),
        out_specs=[spec] * 6,
        out_shape=[shape] * 6,
        scratch_shapes=[pltpu.VMEM((6, tr, kq), F32)],
        compiler_params=_cparams("arbitrary"),
        name="dft_tables",
    )()


def _alt_sign(n_pos):
    pos = lax.broadcasted_iota(jnp.int32, (n_pos, 1), 0)
    return jnp.where((pos & 1) == 0, 1.0, -1.0), pos


def _spectrum_kernel(fse_ref, fso_ref, fde_ref, fdo_ref, ec_ref, es_ref, oc_ref, os_ref,
                     hare_ref, haim_ref, hbre_ref, hbim_ref, hk_ref):
    kq = fse_ref.shape[0]
    alt, pos = _alt_sign(kq)

    def mm2(tab_ref, f):
        fh = f.astype(BF16)
        fl = (f - fh.astype(F32)).astype(BF16)
        return (jnp.dot(tab_ref[...], fh, preferred_element_type=F32)
                + jnp.dot(tab_ref[...], fl, preferred_element_type=F32))

    fse, fdo = fse_ref[...], fdo_ref[...]
    ce, co = mm2(ec_ref, fse), mm2(oc_ref, fso_ref[...])
    se, so = mm2(es_ref, fde_ref[...]), mm2(os_ref, fdo)
    scale = 0.5 / kq
    scale_re = jnp.where(pos == 0, 0.5 * scale, scale)
    hare_ref[...] = (ce + co) * scale_re
    hbre_ref[...] = (ce - co) * scale_re
    haim_ref[...] = (se + so) * scale
    hbim_ref[...] = (so - se) * scale
    hk_ref[0:1, :] = jnp.sum(fse * alt, axis=0, keepdims=True) * scale
    hk_ref[1:2, :] = jnp.sum(fdo * alt, axis=0, keepdims=True) * scale


def _table_specs(tabs, nd):
    imap = (lambda j: (0, 0)) if nd == 1 else (lambda b, j: (0, 0))
    return [pl.BlockSpec(t.shape, imap, pipeline_mode=pl.Buffered(1)) for t in tabs]


def _spectrum(filters, tabs):
    kq, width = filters[0].shape
    tc = DFT_TC
    half = pl.BlockSpec((kq, tc), lambda j: (0, j))
    return pl.pallas_call(
        _spectrum_kernel,
        grid=(width // tc,),
        in_specs=[half] * 4 + _table_specs(tabs[:4], 1),
        out_specs=[half] * 4 + [pl.BlockSpec((2, tc), lambda j: (0, j))],
        out_shape=[jax.ShapeDtypeStruct((kq, width), F32)] * 4
        + [jax.ShapeDtypeStruct((2, width), F32)],
        compiler_params=_cparams("arbitrary"),
        name="hyena_spectrum",
    )(*filters, *tabs[:4])


def _dft_conv_kernel(ve_ref, vo_ref, hare_ref, haim_ref, hbre_ref, hbim_ref, hk_ref,
                     ec_ref, es_ref, oc_ref, os_ref, oct_ref, ost_ref, ye_ref, yo_ref,
                     pe_scr, qe_scr, po_scr, qo_scr):
    kq, tc = ve_ref.shape[1], ve_ref.shape[2]
    tm = min(DFT_TM, kq)
    alt, _ = _alt_sign(kq)
    ve, vo = ve_ref[0], vo_ref[0]
    dot = functools.partial(jnp.dot, preferred_element_type=F32)
    for r0 in range(0, kq, tm):
        rows = slice(r0, r0 + tm)
        ce, co = dot(ec_ref[rows, :], ve), dot(oc_ref[rows, :], vo)
        se, so = dot(es_ref[rows, :], ve), dot(os_ref[rows, :], vo)
        va_re, va_s, vb_re, vb_s = ce + co, se + so, ce - co, so - se
        hare, haim = hare_ref[rows, :], haim_ref[rows, :]
        hbre, hbim = hbre_ref[rows, :], hbim_ref[rows, :]
        ya_re, ya_s = va_re * hare + va_s * haim, va_s * hare - va_re * haim
        yb_re, yb_s = vb_re * hbre + vb_s * hbim, vb_s * hbre - vb_re * hbim
        pe_scr[rows, :] = (ya_re + yb_re).astype(BF16)
        qe_scr[rows, :] = (ya_s - yb_s).astype(BF16)
        po_scr[rows, :] = (ya_re - yb_re).astype(BF16)
        qo_scr[rows, :] = (ya_s + yb_s).astype(BF16)
    vk_re = jnp.sum(ve.astype(F32) * alt, axis=0, keepdims=True)
    vk_s = jnp.sum(vo.astype(F32) * alt, axis=0, keepdims=True)
    hk_re, hk_im = hk_ref[0:1, :], hk_ref[1:2, :]
    yk_re = vk_re * hk_re + vk_s * hk_im
    yk_s = vk_s * hk_re - vk_re * hk_im
    pe, qe, po, qo = pe_scr[...], qe_scr[...], po_scr[...], qo_scr[...]
    for r0 in range(0, kq, tm):
        rows = slice(r0, r0 + tm)
        ye_ref[0, rows, :] = (dot(ec_ref[rows, :], pe) + dot(es_ref[rows, :], qe)
                              + alt[rows] * yk_re).astype(BF16)
        yo_ref[0, rows, :] = (dot(oct_ref[rows, :], po) + dot(ost_ref[rows, :], qo)
                              + alt[rows] * yk_s).astype(BF16)


def _dft_conv(v_even, v_odd, spec, tabs):
    bsz, kq, width = v_even.shape
    tc = DFT_TC
    half = pl.BlockSpec((1, kq, tc), lambda b, j: (b, 0, j))
    col = pl.BlockSpec((kq, tc), lambda b, j: (0, j))
    return pl.pallas_call(
        _dft_conv_kernel,
        grid=(bsz, width // tc),
        in_specs=[half, half, col, col, col, col, pl.BlockSpec((2, tc), lambda b, j: (0, j))]
        + _table_specs(tabs, 2),
        out_specs=[half, half],
        out_shape=[jax.ShapeDtypeStruct(v_even.shape, BF16)] * 2,
        scratch_shapes=[pltpu.VMEM((kq, tc), BF16)] * 4,
        compiler_params=_cparams("parallel", "arbitrary"),
        name="hyena_dft_conv",
    )(v_even, v_odd, *spec, *tabs)


def _hy_out_kernel(x_ref, gate_ref, ye_ref, yo_ref, ve_ref, vo_ref, x0_ref, g_ref, bias_ref, w_ref,
                   o_ref, t_scr):
    f32 = lambda ref: ref[0].astype(F32)
    bias = bias_ref[...]
    t = _merge_rows(f32(ye_ref) + f32(ve_ref) * bias, f32(yo_ref) + f32(vo_ref) * bias, t_scr)
    z = t * f32(x0_ref) * _silu(f32(g_ref))
    o_ref[0] = x_ref[0] + gate_ref[0] * jnp.dot(z.astype(BF16), w_ref[...],
                                                preferred_element_type=F32)


def _hy_out(x, gate, y_even, y_odd, v_even, v_odd, x0, g, bias_d, w_out):
    bsz, n_rows, d = x.shape
    tm = min(ROW_TILE, n_rows)
    rows = pl.BlockSpec((1, tm, d), lambda b, i: (b, i, 0))
    half = pl.BlockSpec((1, tm // 2, d), lambda b, i: (b, i, 0))
    return pl.pallas_call(
        _hy_out_kernel,
        grid=(bsz, n_rows // tm),
        in_specs=[rows, pl.BlockSpec((1, 1, d), lambda b, i: (b, 0, 0)), half, half, half, half,
                  rows, rows, pl.BlockSpec((1, d), lambda b, i: (0, 0)),
                  pl.BlockSpec(w_out.shape, lambda b, i: (0, 0))],
        out_specs=rows,
        out_shape=jax.ShapeDtypeStruct(x.shape, F32),
        scratch_shapes=[_row_split_scratch(tm, d)],
        compiler_params=_cparams("parallel", "arbitrary"),
        name="hyena_out",
    )(x, gate, y_even, y_odd, v_even, v_odd, x0, g, bias_d.reshape(1, d), w_out)


def _rope_perm():
    pairs = np.arange(MLA_ROPE // 2)
    return np.concatenate([np.arange(MLA_NOPE), MLA_NOPE + 2 * pairs, MLA_NOPE + 2 * pairs + 1])


def _rope_tables(n_pos):
    rows = n_pos // GRID_W
    row = jnp.repeat(jnp.arange(rows, dtype=F32), GRID_W)
    col = jnp.tile(jnp.arange(GRID_W, dtype=F32), rows)
    n_freq = MLA_ROPE // 4
    inv = ROPE_BASE ** (-jnp.arange(n_freq, dtype=F32) / n_freq)
    ang = jnp.concatenate([row[:, None] * inv, col[:, None] * inv], axis=-1)
    cos, sin = jnp.cos(ang), jnp.sin(ang)
    half = MLA_ROPE // 2
    ones = jnp.ones((n_pos, MLA_NOPE), F32)
    zeros = jnp.zeros((n_pos, MLA_NOPE), F32)
    pad1 = jnp.ones((n_pos, HEAD_PAD - MLA_QK), F32)
    pad0 = jnp.zeros((n_pos, HEAD_PAD - MLA_QK), F32)
    z16 = jnp.zeros((n_pos, half), F32)
    cos_f = jnp.concatenate([ones, cos, cos, pad1], axis=-1)
    sin_a = jnp.concatenate([zeros, z16, sin, pad0], axis=-1)
    sin_b = jnp.concatenate([zeros, -sin, z16, pad0], axis=-1)
    return cos_f, sin_a, sin_b


def _pad_heads(w, width):
    k, h, _ = w.shape
    return jnp.zeros((k, h, HEAD_PAD), w.dtype).at[:, :, :width].set(w).reshape(k, h * HEAD_PAD)


def _even_weights(e, ev_w_in, ev_w_out, mla_q_a_norm, mla_w_uq, mla_kv_a_norm, mla_w_ukv,
                  mla_q_norm, mla_k_norm, rwkv_mu_prev, rwkv_mu_next, rwkv_w0, rwkv_w_up, rwkv_a0,
                  rwkv_a_up, rwkv_k_k, rwkv_k_a, rwkv_r_k, rwkv_ln_w, rwkv_ln_b):
    perm = _rope_perm()
    w_in = ev_w_in[e]
    d = w_in.shape[0]
    o_dkv = EV_DQ
    o_rw = EV_DQ + EV_DKV
    o_g = o_rw + RW_SHIFT
    w_dkv = jnp.zeros((d, 2 * LANES), F32)
    w_dkv = w_dkv.at[:, :MLA_KV_LORA].set(w_in[:, o_dkv:o_dkv + MLA_KV_LORA])
    rope_cols = o_dkv + MLA_KV_LORA + (perm[MLA_NOPE:] - MLA_NOPE)
    w_dkv = w_dkv.at[:, LANES + MLA_NOPE:LANES + MLA_QK].set(w_in[:, rope_cols])
    proj = [w_in[:, :EV_DQ].astype(BF16), w_dkv.astype(BF16),
            w_in[:, o_rw:o_g].astype(BF16), w_in[:, o_g:].astype(BF16)]
    w_ukv = mla_w_ukv[e]
    pad_gain = lambda g: jnp.zeros((1, HEAD_PAD), F32).at[0, :MLA_QK].set(g[perm])
    seg_id = np.arange(RW_WIDTH) // RW_HEAD
    zero_up = lambda up, dd: jnp.zeros((2 * RW_LORA_W, RW_WIDTH), F32).at[
        dd * RW_LORA_W:(dd + 1) * RW_LORA_W].set(up[dd])
    rw = dict(
        mu_prev=rwkv_mu_prev[e].reshape(1, RW_SHIFT), mu_next=rwkv_mu_next[e].reshape(1, RW_SHIFT),
        w0=rwkv_w0[e], a0=rwkv_a0[e],
        w_up=jnp.stack([zero_up(rwkv_w_up[e], 0), zero_up(rwkv_w_up[e], 1)]),
        a_up=jnp.stack([zero_up(rwkv_a_up[e], 0), zero_up(rwkv_a_up[e], 1)]),
        k_k=rwkv_k_k[e].reshape(1, RW_WIDTH), k_a=rwkv_k_a[e].reshape(1, RW_WIDTH),
        r_k=rwkv_r_k[e].reshape(2, RW_WIDTH),
        seg=jnp.asarray((seg_id[:, None] == seg_id[None, :]).astype(np.float32)),
    )
    head_par = np.arange(MLA_HEADS) % 2
    v_lane = (np.arange(HEAD_PAD)[None, :] >= MLA_V) == (head_par[:, None] == 1)
    v_ones = (~v_lane).astype(np.float32)
    w_vh = w_ukv[:, :, MLA_NOPE:].reshape(MLA_KV_LORA, MLA_HEADS // 2, 2, MLA_V)
    zero_v = jnp.zeros_like(w_vh[:, :, 0])
    w_v = jnp.stack([w_vh[:, :, 0], zero_v, zero_v, w_vh[:, :, 1]], axis=2)
    return dict(
        proj=proj,
        q_a_norm=mla_q_a_norm[e], kv_a_norm=mla_kv_a_norm[e],
        w_uq=_pad_heads(mla_w_uq[e][:, :, perm], MLA_QK).astype(BF16),
        w_k=_pad_heads(w_ukv[:, :, :MLA_NOPE], MLA_NOPE).astype(BF16),
        w_v=w_v.reshape(MLA_KV_LORA, MLA_HEADS * HEAD_PAD).astype(BF16),
        v_ones=jnp.asarray(v_ones.reshape(1, MLA_HEADS * HEAD_PAD)),
        q_gain=pad_gain(mla_q_norm[e]) * (MLA_QK ** -0.5 * math.log2(math.e)),
        k_gain=pad_gain(mla_k_norm[e]),
        attn_bound=(1.01 * MLA_QK * (MLA_QK ** -0.5 * math.log2(math.e))
                    * jnp.max(jnp.abs(mla_q_norm[e])) * jnp.max(jnp.abs(mla_k_norm[e]))
                    ).reshape(1).astype(F32),
        rw=rw,
        ln_w=rwkv_ln_w[e].reshape(1, RW_WIDTH), ln_b=rwkv_ln_b[e].reshape(1, RW_WIDTH),
        w_out=ev_w_out[e].astype(BF16),
    )


def _rwkv_branch(st_c, st, want_ctx):
    bsz = st["r"].shape[0]
    zero_state = jnp.zeros((bsz, 2 * RW_HEAD, RW_WIDTH), F32)
    of_c, ob_c, s_c = _rwkv_scan(st_c, zero_state)
    o_f, o_b, _ = _rwkv_scan(st, s_c)
    return [o_f, o_b], ([of_c, ob_c] if want_ctx else None)


def _even_layer(x, xc, mod, mod_c, g_norm, wts, rope_tabs, ctx_out):
    shift, scale1p, gate = mod
    shift_c, scale1p_c, gate_c = mod_c
    q, k, v, p_rw, g = _even_in(x, g_norm, scale1p, shift, wts, rope_tabs)
    qc, kc, vc, pc_rw, gc = _even_in(xc, g_norm, scale1p_c, shift_c, wts, None)
    o_mla = _attention(q, [k, kc], [v, vc], wts["attn_bound"])
    st_c = _rw_prep(pc_rw, wts["rw"])
    st = _rw_prep(p_rw, wts["rw"])
    outs, outs_c = _rwkv_branch(st_c, st, ctx_out)
    seg = wts["rw"]["seg"]
    x_new = _even_out(x, gate, o_mla, outs[0], outs[1], st["bonus"], g, wts["ln_w"], wts["ln_b"],
                      seg, wts["w_out"])
    if not ctx_out:
        return x_new, None
    oc_mla = _attention(qc, [kc], [vc], wts["attn_bound"])
    xc_new = _even_out(xc, gate_c, oc_mla, outs_c[0], outs_c[1], st_c["bonus"], gc, wts["ln_w"],
                       wts["ln_b"], seg, wts["w_out"])
    return x_new, xc_new


def _hyena_layer(x, mod, g_norm, wts, tables):
    shift, scale1p, gate = mod
    n_pos = x.shape[1]
    u, g = _norm_proj(x, g_norm, scale1p, shift, wts["proj"])
    v_even, v_odd, x0 = _conv3(u, wts["conv_w"], wts["conv_b"])
    spec = _spectrum(_hyena_filters(n_pos, *wts["filt"]), tables)
    y_even, y_odd = _dft_conv(v_even, v_odd, spec, tables)
    return _hy_out(x, gate, y_even, y_odd, v_even, v_odd, x0, g, wts["bias_d"], wts["w_out"])


def kernel(x, c, ctx, c_ctx, mod_w, mod_b, norm_g, ev_w_in, ev_w_out, mla_q_a_norm, mla_w_uq, mla_kv_a_norm, mla_w_ukv, mla_q_norm, mla_k_norm, rwkv_mu_prev, rwkv_mu_next, rwkv_w0, rwkv_w_up, rwkv_a0, rwkv_a_up, rwkv_k_k, rwkv_k_a, rwkv_r_k, rwkv_ln_w, rwkv_ln_b, od_w_in, od_w_out, hy_conv_w, hy_conv_b, hy_bias_d, hy_f_w1, hy_f_b1, hy_f_w2, hy_f_b2, hy_f_wout, hy_freq):
    bsz, n_lat, d = x.shape
    n_ctx = ctx.shape[1]
    scan_rows = SCAN_CHUNKS * CHUNK
    assert n_lat % max(scan_rows, GRID_W) == 0 and n_ctx % scan_rows == 0 and d == D_MODEL
    assert CHUNK == RW_HEAD and 2 * CHUNK == RW_PAIR

    n_rows = -(-(bsz + 1) // 16) * 16
    cvec = jnp.zeros((n_rows, d), F32).at[:bsz].set(c).at[bsz].set(c_ctx)
    mods = _modulation(cvec, mod_w, mod_b)

    def split_mod(i, lo, hi, reps):
        m = mods[i, lo:hi]
        m = jnp.broadcast_to(m, (reps, 3 * d)) if hi - lo == 1 else m
        m = m[:, None, :]
        return m[..., :d], 1.0 + m[..., d:2 * d], m[..., 2 * d:]

    rope_tabs = _rope_tables(n_lat)
    deltas = jnp.abs(jnp.linspace(math.log(HY_TARGET) / HY_FAST_DECAY,
                                  math.log(HY_TARGET) / HY_SLOW_DECAY, HY_WIDTH,
                                  dtype=F32)).reshape(1, HY_WIDTH)
    tables = {n_lat: _dft_tables(n_lat)}

    xc = ctx
    for i in range(DEPTH):
        ctx_needed_later = any(j > i and j % 2 == 0 for j in range(DEPTH))
        mod = split_mod(i, 0, bsz, bsz)
        mod_c = split_mod(i, bsz, bsz + 1, bsz)
        if i % 2 == 0:
            wts = _even_weights(i // 2, ev_w_in, ev_w_out, mla_q_a_norm, mla_w_uq, mla_kv_a_norm,
                                mla_w_ukv, mla_q_norm, mla_k_norm, rwkv_mu_prev, rwkv_mu_next,
                                rwkv_w0, rwkv_w_up, rwkv_a0, rwkv_a_up, rwkv_k_k, rwkv_k_a,
                                rwkv_r_k, rwkv_ln_w, rwkv_ln_b)
            x, xc_new = _even_layer(x, xc, mod, mod_c, norm_g[i], wts, rope_tabs, ctx_needed_later)
            xc = xc_new if ctx_needed_later else xc
        else:
            o = i // 2
            w_in = od_w_in[o]
            wts = dict(
                proj=[w_in[:, :3 * HY_WIDTH].astype(BF16), w_in[:, 3 * HY_WIDTH:].astype(BF16)],
                conv_w=hy_conv_w[o], conv_b=hy_conv_b[o], bias_d=hy_bias_d[o],
                filt=(hy_f_w1[o], hy_f_b1[o], hy_f_w2[o], hy_f_b2[o], hy_f_wout[o], hy_freq[o],
                      deltas),
                w_out=od_w_out[o].astype(BF16),
            )
            if ctx_needed_later:
                if n_ctx not in tables:
                    tables[n_ctx] = _dft_tables(n_ctx)
                xc = _hyena_layer(xc, mod_c, norm_g[i], wts, tables[n_ctx])
            x = _hyena_layer(x, mod, norm_g[i], wts, tables[n_lat])
    return x
```

```python
import functools
import math

import numpy as np
import jax
import jax.numpy as jnp
from jax import lax
from jax.experimental import pallas as pl
from jax.experimental.pallas import tpu as pltpu

F32 = jnp.float32
BF16 = jnp.bfloat16
HIGHEST = lax.Precision.HIGHEST

D_MODEL = 1024
DEPTH = 4
GRID_W = 64
NORM_EPS = 1e-6
MLA_HEADS = 8
MLA_NOPE = 64
MLA_ROPE = 32
MLA_QK = MLA_NOPE + MLA_ROPE
MLA_V = 64
MLA_Q_LORA = 256
MLA_KV_LORA = 128
MLA_WIDTH = MLA_HEADS * MLA_V
ROPE_BASE = 10000.0
RW_HEADS = 8
RW_HEAD = 64
RW_WIDTH = RW_HEADS * RW_HEAD
RW_LORA_W = 64
RW_LORA_A = 64
RW_SHIFT = 3 * RW_WIDTH + 2 * RW_LORA_W + 2 * RW_LORA_A
RW_GN_EPS = 64e-5
EV_DQ = MLA_Q_LORA
EV_DKV = MLA_KV_LORA + MLA_ROPE
HY_WIDTH = D_MODEL
HY_ORDER = 64
HY_BANDS = 16
HY_EMB = 1 + 2 * HY_BANDS
HY_INNER = 2
HY_FAST_DECAY = 0.3
HY_SLOW_DECAY = 1.5
HY_TARGET = 1e-2

LANES = 128
HEAD_PAD = 128
RW_PAIR = 2 * RW_HEAD
N_PAIRS = RW_WIDTH // RW_PAIR
CHUNK = 64
SCAN_CHUNKS = 4
ROW_TILE = 256
STREAM_TILE = 512
ATTN_TQ = 256
ATTN_TK = 256
ATTN_SAFE_BITS = 40.0
DFT_TC = 256
DFT_TM = 256
HALO_ROWS = 16
EVEN_IN_TILE = 256
SUB_ROWS = 128
VMEM_LIMIT = 56 * 1024 * 1024

_NN = (((1,), (0,)), ((), ()))
_NT = (((1,), (1,)), ((), ()))


def _mm(a, b, dn=_NN, mode="bf16"):
    if mode == "f32":
        return lax.dot_general(a, b, dn, precision=HIGHEST, preferred_element_type=F32)
    dg = functools.partial(lax.dot_general, dimension_numbers=dn, preferred_element_type=F32)
    ah = a.astype(BF16)
    bh = b.astype(BF16)
    if mode == "rhs3":
        r1 = b - bh.astype(F32)
        bm = r1.astype(BF16)
        bl = (r1 - bm.astype(F32)).astype(BF16)
        return dg(ah, bh) + (dg(ah, bm) + dg(ah, bl))
    if mode == "bf16":
        return dg(ah, bh)
    al = (a - ah.astype(F32)).astype(BF16)
    if mode == "lhs2":
        return dg(ah, bh) + dg(al, bh)
    bl = (b - bh.astype(F32)).astype(BF16)
    return dg(ah, bh) + (dg(ah, bl) + dg(al, bh))


def _cparams(*sem):
    return pltpu.CompilerParams(dimension_semantics=sem, vmem_limit_bytes=VMEM_LIMIT)


def _silu(t):
    return t * jax.nn.sigmoid(t)


def _shifted_rows(pb, prev_ref, next_ref, on_mxu):
    tm = pb.shape[0]
    i = pl.program_id(1)
    last = pl.num_programs(1) - 1
    prev_row = jnp.where(i > 0, prev_ref[0, HALO_ROWS - 1:HALO_ROWS, :].astype(F32), 0.0)
    next_row = jnp.where(i < last, next_ref[0, 0:1, :].astype(F32), 0.0)
    if on_mxu:
        out_row = lax.broadcasted_iota(jnp.int32, (2 * tm, tm), 0)
        src_row = lax.broadcasted_iota(jnp.int32, (2 * tm, tm), 1)
        want = jnp.where(out_row < tm, out_row - 1, out_row - tm + 1)
        both = jnp.dot(jnp.where(src_row == want, 1.0, 0.0).astype(BF16), pb,
                       preferred_element_type=F32)
        down, up = both[:tm], both[tm:]
    else:
        p = pb.astype(F32)
        down, up = pltpu.roll(p, 1, axis=0), pltpu.roll(p, tm - 1, axis=0)
    sub = lax.broadcasted_iota(jnp.int32, (8, 1), 0)
    prev = jnp.concatenate([jnp.where(sub == 0, prev_row, down[:8]), down[8:]], axis=0)
    nxt = jnp.concatenate([up[:tm - 8], jnp.where(sub == 7, next_row, up[tm - 8:])], axis=0)
    return prev, nxt


def _split_rows(val, scr):
    rows, width = val.shape
    half = rows // 2
    for j in range(width // LANES):
        scr[j] = val[:, j * LANES:(j + 1) * LANES]
    pick = lambda start: jnp.concatenate(
        [scr[j, pl.ds(start, half, stride=2), :] for j in range(width // LANES)], axis=1)
    return pick(0), pick(1)


def _merge_rows(even, odd, scr):
    half, width = even.shape
    for j in range(width // LANES):
        cols = slice(j * LANES, (j + 1) * LANES)
        scr[j, pl.ds(0, half, stride=2), :] = even[:, cols]
        scr[j, pl.ds(1, half, stride=2), :] = odd[:, cols]
    return jnp.concatenate([scr[j] for j in range(width // LANES)], axis=1)


def _row_split_scratch(rows, width):
    return pltpu.VMEM((width // LANES, rows, LANES), F32)


def _halo_specs(tm, width, n_rows):
    th = tm // HALO_ROWS
    last_h = n_rows // HALO_ROWS - 1
    main = pl.BlockSpec((1, tm, width), lambda b, i: (b, i, 0))
    prev = pl.BlockSpec((1, HALO_ROWS, width), lambda b, i: (b, jnp.maximum(i * th - 1, 0), 0))
    nxt = pl.BlockSpec((1, HALO_ROWS, width),
                       lambda b, i: (b, jnp.minimum((i + 1) * th, last_h), 0))
    return main, prev, nxt


def _mod_kernel(c_ref, w_ref, b_ref, o_ref):
    o_ref[0] = _mm(_silu(c_ref[...]), w_ref[0], mode="x3") + b_ref[0]


def _modulation(cvec, mod_w, mod_b):
    rows, d = cvec.shape
    n = mod_w.shape[-1]
    tn = 1024
    return pl.pallas_call(
        _mod_kernel,
        grid=(DEPTH, n // tn),
        in_specs=[pl.BlockSpec((rows, d), lambda i, j: (0, 0)),
                  pl.BlockSpec((1, d, tn), lambda i, j: (i, 0, j)),
                  pl.BlockSpec((1, 1, tn), lambda i, j: (i, 0, j))],
        out_specs=pl.BlockSpec((1, rows, tn), lambda i, j: (i, 0, j)),
        out_shape=jax.ShapeDtypeStruct((DEPTH, rows, n), F32),
        compiler_params=_cparams("arbitrary", "arbitrary"),
        name="modulation",
    )(cvec, mod_w, mod_b.reshape(DEPTH, 1, n))


def _norm_proj_kernel(nw, x_ref, g_ref, sc_ref, sh_ref, *refs):
    x = x_ref[0]
    h = x * lax.rsqrt(jnp.mean(x * x, axis=-1, keepdims=True) + NORM_EPS) * g_ref[...]
    hb = (h * sc_ref[0] + sh_ref[0]).astype(BF16)
    for w_ref, o_ref in zip(refs[:nw], refs[nw:]):
        o_ref[0] = jnp.dot(hb, w_ref[...], preferred_element_type=F32).astype(BF16)


def _norm_proj(x, g, scale1p, shift, weights):
    bsz, n_rows, d = x.shape
    tm = min(STREAM_TILE, n_rows)
    vec = pl.BlockSpec((1, 1, d), lambda b, i: (b, 0, 0))
    in_specs = [pl.BlockSpec((1, tm, d), lambda b, i: (b, i, 0)),
                pl.BlockSpec((1, d), lambda b, i: (0, 0)), vec, vec]
    in_specs += [pl.BlockSpec(w.shape, lambda b, i: (0, 0)) for w in weights]
    return pl.pallas_call(
        functools.partial(_norm_proj_kernel, len(weights)),
        grid=(bsz, n_rows // tm),
        in_specs=in_specs,
        out_specs=[pl.BlockSpec((1, tm, w.shape[1]), lambda b, i: (b, i, 0)) for w in weights],
        out_shape=[jax.ShapeDtypeStruct((bsz, n_rows, w.shape[1]), BF16) for w in weights],
        compiler_params=_cparams("parallel", "arbitrary"),
        name="norm_proj",
    )(x, g.reshape(1, d), scale1p, shift, *weights)


def _head_norm_rope(t, gain, tabs):
    ms = jnp.sum(t * t, axis=-1, keepdims=True) * (1.0 / MLA_QK)
    t = t * lax.rsqrt(ms + NORM_EPS) * gain
    if tabs is not None:
        cos_f, sin_a, sin_b = tabs
        t = (t * cos_f + pltpu.roll(t, MLA_ROPE // 2, axis=1) * sin_a
             + pltpu.roll(t, HEAD_PAD - MLA_ROPE // 2, axis=1) * sin_b)
    return t


def _even_in_kernel(rope, x_ref, g_ref, sc_ref, sh_ref, wdq_ref, wdkv_ref, wrw_ref, wg_ref,
                    qan_ref, wuq_ref, qgn_ref, kan_ref, wk_ref, wv_ref, vone_ref, kgn_ref, *refs):
    q_ref, k_ref, v_ref, prw_ref, go_ref = refs[-5:]
    q_gain, k_gain = qgn_ref[...], kgn_ref[...]

    def rms(t, gain_ref):
        return t * lax.rsqrt(jnp.mean(t * t, axis=-1, keepdims=True) + NORM_EPS) * gain_ref[...]

    tm = x_ref.shape[1]
    sub = min(SUB_ROWS, tm)
    for r0 in range(0, tm, sub):
        rows = slice(r0, r0 + sub)
        tabs = tuple(r[rows, :] for r in refs[:3]) if rope else None
        hb = (rms(x_ref[0, rows, :], g_ref) * sc_ref[0] + sh_ref[0]).astype(BF16)
        prw_ref[0, rows, :] = jnp.dot(hb, wrw_ref[...], preferred_element_type=F32).astype(BF16)
        go_ref[0, rows, :] = jnp.dot(hb, wg_ref[...], preferred_element_type=F32).astype(BF16)
        p_dq = jnp.dot(hb, wdq_ref[...], preferred_element_type=F32)
        q = jnp.dot(rms(p_dq, qan_ref).astype(BF16), wuq_ref[...], preferred_element_type=F32)
        p_dkv = jnp.dot(hb, wdkv_ref[...], preferred_element_type=F32)
        k_rope = p_dkv[:, MLA_KV_LORA:]
        ab = rms(p_dkv[:, :MLA_KV_LORA], kan_ref).astype(BF16)
        k_nope = jnp.dot(ab, wk_ref[...], preferred_element_type=F32)
        v_ref[0, rows, :] = (jnp.dot(ab, wv_ref[...], preferred_element_type=F32)
                             + vone_ref[...]).astype(BF16)
        for hd in range(MLA_HEADS):
            sl = slice(hd * HEAD_PAD, (hd + 1) * HEAD_PAD)
            q_ref[0, rows, sl] = _head_norm_rope(q[:, sl], q_gain, tabs).astype(BF16)
            k_ref[0, rows, sl] = _head_norm_rope(k_nope[:, sl] + k_rope, k_gain,
                                                 tabs).astype(BF16)


def _even_in(x, g, scale1p, shift, wts, tabs):
    bsz, n_rows, d = x.shape
    tm = min(EVEN_IN_TILE, n_rows)
    rope = tabs is not None
    kw = MLA_HEADS * HEAD_PAD
    vec = pl.BlockSpec((1, 1, d), lambda b, i: (b, 0, 0))
    full = lambda a: pl.BlockSpec(a.shape, lambda b, i: (0,) * a.ndim)
    consts = list(wts["proj"]) + [wts["q_a_norm"].reshape(1, MLA_Q_LORA), wts["w_uq"], wts["q_gain"],
                                  wts["kv_a_norm"].reshape(1, MLA_KV_LORA), wts["w_k"], wts["w_v"],
                                  wts["v_ones"], wts["k_gain"]]
    in_specs = [pl.BlockSpec((1, tm, d), lambda b, i: (b, i, 0)),
                pl.BlockSpec((1, d), lambda b, i: (0, 0)), vec, vec] + [full(a) for a in consts]
    args = [x, g.reshape(1, d), scale1p, shift] + consts
    if rope:
        in_specs += [pl.BlockSpec((tm, HEAD_PAD), lambda b, i: (i, 0))] * 3
        args += list(tabs)
    widths = [kw, kw, kw, RW_SHIFT, d]
    return pl.pallas_call(
        functools.partial(_even_in_kernel, rope),
        grid=(bsz, n_rows // tm),
        in_specs=in_specs,
        out_specs=[pl.BlockSpec((1, tm, w), lambda b, i: (b, i, 0)) for w in widths],
        out_shape=[jax.ShapeDtypeStruct((bsz, n_rows, w), BF16) for w in widths],
        compiler_params=_cparams("parallel", "arbitrary"),
        name="even_in",
    )(*args)


def _attn_body(q_ref, k_refs, v_refs, o_ref, bound):
    tq = q_ref.shape[1]
    lane = lax.broadcasted_iota(jnp.int32, (tq, LANES), 1)
    for hp in range(MLA_HEADS // 2):
        psl = slice(2 * hp * HEAD_PAD, (2 * hp + 2) * HEAD_PAD)
        outs = []
        for h in (2 * hp, 2 * hp + 1):
            sl = slice(h * HEAD_PAD, (h + 1) * HEAD_PAD)
            q = q_ref[0, :, sl]
            score = lambda k_ref, k0, k1: lax.dot_general(q, k_ref[0, k0:k1, sl], _NT,
                                                          preferred_element_type=F32)
            if bound is None:
                ss = [score(k_ref, 0, k_ref.shape[1]) for k_ref in k_refs]
                m = functools.reduce(jnp.maximum, [jnp.max(s, axis=-1, keepdims=True) for s in ss])
            acc = None
            for seg, (k_ref, v_ref) in enumerate(zip(k_refs, v_refs)):
                for k0 in range(0, k_ref.shape[1], ATTN_TK):
                    k1 = k0 + ATTN_TK
                    p = (jnp.exp2(ss[seg][:, k0:k1] - m) if bound is None
                         else jnp.exp2(score(k_ref, k0, k1) - bound))
                    pv = jnp.dot(p.astype(BF16), v_ref[0, k0:k1, psl], preferred_element_type=F32)
                    acc = pv if acc is None else pv + acc
            acc = acc[:, :HEAD_PAD] if h % 2 == 0 else acc[:, HEAD_PAD:]
            ones_lane = MLA_V if h % 2 == 0 else 0
            outs.append(acc * (1.0 / acc[:, ones_lane:ones_lane + 1]))
        o_ref[0, :, hp * LANES:(hp + 1) * LANES] = jnp.where(lane < MLA_V, outs[0],
                                                             outs[1]).astype(BF16)


def _attn_kernel(nseg, bound_ref, q_ref, *refs):
    k_refs, v_refs, o_ref = refs[:nseg], refs[nseg:2 * nseg], refs[2 * nseg]
    bound = bound_ref[0]

    @pl.when(bound <= ATTN_SAFE_BITS)
    def _():
        _attn_body(q_ref, k_refs, v_refs, o_ref, bound)

    @pl.when(bound > ATTN_SAFE_BITS)
    def _():
        _attn_body(q_ref, k_refs, v_refs, o_ref, None)


def _attention(q, ks, vs, bound):
    bsz, n_q, qw = q.shape
    tq = min(ATTN_TQ, n_q)
    nseg = len(ks)
    in_specs = [pl.BlockSpec(memory_space=pltpu.SMEM),
                pl.BlockSpec((1, tq, qw), lambda b, i: (b, i, 0))]
    in_specs += [pl.BlockSpec((1,) + k.shape[1:], lambda b, i: (b, 0, 0)) for k in ks]
    in_specs += [pl.BlockSpec((1,) + v.shape[1:], lambda b, i: (b, 0, 0)) for v in vs]
    return pl.pallas_call(
        functools.partial(_attn_kernel, nseg),
        grid=(bsz, n_q // tq),
        in_specs=in_specs,
        out_specs=pl.BlockSpec((1, tq, MLA_WIDTH), lambda b, i: (b, i, 0)),
        out_shape=jax.ShapeDtypeStruct((bsz, n_q, MLA_WIDTH), BF16),
        compiler_params=_cparams("parallel", "arbitrary"),
        name="mla_attention",
    )(bound, q, *ks, *vs)


def _rw_prep_kernel(p_ref, prev_ref, next_ref, mup_ref, mun_ref, w0_ref, wup_ref, a0_ref, aup_ref,
                    kk_ref, ka_ref, rk_ref, seg_ref,
                    r_out, v_out, kkn_out, bonus_out, lw0_out, lw1_out, b0_out, b1_out,
                    kd0_out, kd1_out):
    pb = p_ref[0]
    p = pb.astype(F32)
    prev, nxt = _shifted_rows(pb, prev_ref, next_ref, on_mxu=False)
    mu_p, mu_n = mup_ref[...], mun_ref[...]
    ps = p * (1.0 - mu_p - mu_n) + prev * mu_p + nxt * mu_n
    w = RW_WIDTH
    r, k, v = ps[:, :w], ps[:, w:2 * w], ps[:, 2 * w:3 * w]
    wd = jnp.tanh(ps[:, 3 * w:3 * w + 2 * RW_LORA_W])
    ad = ps[:, 3 * w + 2 * RW_LORA_W:]
    seg = seg_ref[...]
    kq = k * kk_ref[...]
    kk = kq * lax.rsqrt(jnp.maximum(_mm(kq * kq, seg, mode="lhs2"), 1e-24))
    r_out[0] = r.astype(BF16)
    v_out[0] = v.astype(BF16)
    kkn_out[0] = kk.astype(BF16)
    bonus_in = jnp.zeros_like(r)
    for d, (lw_out, b_out, kd_out) in enumerate(((lw0_out, b0_out, kd0_out),
                                                  (lw1_out, b1_out, kd1_out))):
        z = w0_ref[d:d + 1, :] + _mm(wd, wup_ref[d], mode="x3")
        lw_out[0] = -math.exp(-0.5) * jax.nn.sigmoid(z)
        a = jax.nn.sigmoid(a0_ref[d:d + 1, :] + _mm(ad, aup_ref[d], mode="x3"))
        kd = k * (1.0 + (a - 1.0) * ka_ref[...])
        b_out[0] = (kk * a).astype(BF16)
        kd_out[0] = kd.astype(BF16)
        bonus_in = bonus_in + r * kd * rk_ref[d:d + 1, :]
    bonus_out[0] = (_mm(bonus_in, seg, mode="lhs2") * v).astype(BF16)


def _rw_prep(p_rw, prm):
    bsz, n_rows, width = p_rw.shape
    tm = min(ROW_TILE, n_rows)
    main, prev, nxt = _halo_specs(tm, width, n_rows)
    full = lambda a: pl.BlockSpec(a.shape, lambda b, i: (0,) * a.ndim)
    consts = [prm["mu_prev"], prm["mu_next"], prm["w0"], prm["w_up"], prm["a0"], prm["a_up"],
              prm["k_k"], prm["k_a"], prm["r_k"], prm["seg"]]
    out_spec = pl.BlockSpec((1, tm, RW_WIDTH), lambda b, i: (b, i, 0))
    shape = lambda dt: jax.ShapeDtypeStruct((bsz, n_rows, RW_WIDTH), dt)
    outs = pl.pallas_call(
        _rw_prep_kernel,
        grid=(bsz, n_rows // tm),
        in_specs=[main, prev, nxt] + [full(a) for a in consts],
        out_specs=[out_spec] * 10,
        out_shape=[shape(BF16)] * 4 + [shape(F32)] * 2 + [shape(BF16)] * 4,
        compiler_params=_cparams("parallel", "arbitrary"),
        name="rwkv_prep",
    )(p_rw, p_rw, p_rw, *consts)
    r, v, kk, bonus, lw0, lw1, b0, b1, kd0, kd1 = outs
    return dict(r=r, v=v, kk=kk, bonus=bonus, lw=(lw0, lw1), b=(b0, b1), kd=(kd0, kd1))


def _rwkv_scan_kernel(rf_ref, vf_ref, kkf_ref, lw0_ref, b0_ref, kd0_ref,
                      rb_ref, vb_ref, kkb_ref, lw1_ref, b1_ref, kd1_ref, s0_ref,
                      of_ref, ob_ref, sfin_ref, st_ref):
    j = pl.program_id(1)

    @pl.when(j == 0)
    def _():
        st_ref[...] = s0_ref[0]

    c = CHUNK
    c2 = 2 * c
    nsub = rf_ref.shape[1] // c
    dir_refs = ((rf_ref, vf_ref, kkf_ref, lw0_ref, b0_ref, kd0_ref, of_ref),
                (rb_ref, vb_ref, kkb_ref, lw1_ref, b1_ref, kd1_ref, ob_ref))
    ti = lax.broadcasted_iota(jnp.int32, (c, c), 0)
    si = lax.broadcasted_iota(jnp.int32, (c, c), 1)
    tp = lax.broadcasted_iota(jnp.int32, (c, c2), 0)
    sp = lax.broadcasted_iota(jnp.int32, (c, c2), 1) & (c - 1)
    eye_p = jnp.where(tp == sp, 1.0, 0.0)
    head0 = lax.broadcasted_iota(jnp.int32, (c, RW_PAIR), 1) < RW_HEAD
    masks = []
    for rev in (False, True):
        tri = jnp.where((si >= ti) if rev else (si <= ti), 1.0, 0.0)
        before = (sp > tp) if rev else (sp < tp)
        upto = (sp >= tp) if rev else (sp <= tp)
        masks.append((tri, before, upto))

    def stack(t):
        return jnp.concatenate([jnp.where(head0, t, 0.0), jnp.where(head0, 0.0, t)], axis=0)

    def unstack(t):
        return t[:c] + t[c:]

    mm = functools.partial(_mm, mode="bf16")
    items = [(d, p, s) for d in range(2) for p in range(N_PAIRS) for s in range(nsub)]
    pair_sl = lambda p: slice(p * RW_PAIR, (p + 1) * RW_PAIR)
    chunk_rows = lambda s: slice(s * c, (s + 1) * c)

    lams = [[_mm(masks[d][0], dir_refs[d][3][0, chunk_rows(s), :], mode="rhs3") for s in range(nsub)]
            for d in range(2)]

    ops = {key: {} for key in items}

    def prep(key):
        d, p, s = key
        r_ref, v_ref, kk_ref, lw_ref, b_ref, kd_ref, _ = dir_refs[d]
        sl, rs = pair_sl(p), chunk_rows(s)
        lam = lams[d][s][:, sl]
        lam_tot = lam[0:1] if d == 1 else lam[c - 1:c]
        e_neg = jnp.exp(-lam)
        e_tail = jnp.exp(lam_tot - lam)
        ld = lambda ref: ref[0, rs, sl].astype(F32)
        b, kd = ld(b_ref), ld(kd_ref)
        at = -ld(kk_ref) * jnp.exp(lam - lw_ref[0, rs, sl])
        bt_t = unstack(stack(b * e_tail).T)
        kt_t = unstack(stack(kd * e_tail).T)
        ops[key].update(
            at=at, rt=ld(r_ref) * jnp.exp(lam), at2=stack(at), v2=stack(ld(v_ref)),
            bk2=jnp.concatenate([stack(b * e_neg), stack(kd * e_neg)], axis=0),
            bkt_t=jnp.concatenate([bt_t, kt_t], axis=1), bt_t=bt_t,
            decay_tot=jnp.where(eye_p == 1.0, jnp.exp(lam_tot), 0.0))

    def interactions(key):
        o = ops[key]
        _, before, upto = masks[key[0]]
        x = mm(jnp.concatenate([o["at"], o["rt"]], axis=0), o["bk2"], _NT)
        o["a_ab"] = jnp.where(before, x[:c, :c2], 0.0)
        o["a_ak"] = jnp.where(before, x[:c, c2:], 0.0)
        o["a_rb"] = jnp.where(upto, x[c:, :c2], 0.0)
        o["a_rk"] = jnp.where(upto, x[c:, c2:], 0.0)

    def inverse_first(key):
        o = ops[key]
        o["t_inv"] = eye_p + o["a_ab"]
        o["pw"] = mm(o["a_ab"], stack(o["a_ab"]))
        o["w"] = mm(o["a_ak"], o["v2"])

    def inverse_step(key):
        o = ops[key]
        y = mm(jnp.concatenate([o["pw"], o["t_inv"]], axis=0), stack(o["pw"]))
        o["pw"] = y[:c]
        o["t_inv"] = o["t_inv"] + y[c:]

    def inverse_last(key):
        o = ops[key]
        o["t_inv"] = o["t_inv"] + mm(o["t_inv"], stack(o["pw"]))

    def transforms(key):
        o = ops[key]
        z = mm(o["t_inv"], jnp.concatenate([o["at2"], stack(o["w"])], axis=1))
        ap2, u02 = stack(z[:, :c2]), stack(z[:, c2:])
        ya = mm(jnp.concatenate([o["a_rb"], o["bt_t"]], axis=0), ap2)
        yb = mm(jnp.concatenate([jnp.concatenate([o["a_rb"], o["a_rk"]], axis=1), o["bkt_t"]], axis=0),
                jnp.concatenate([u02, o["v2"]], axis=0))
        o["rp"] = o["rt"] + ya[:c]
        o["o0"] = yb[:c]
        o["m_p"] = o["decay_tot"] + ya[c:]
        o["n_p"] = yb[c:]

    stages = ([prep, interactions, inverse_first] + [inverse_step] * (int(math.log2(c)) - 2)
              + [inverse_last, transforms])
    for stage in stages:
        for key in items:
            stage(key)

    by_item = ops
    chains =[(d, p) for d in range(2) for p in range(N_PAIRS)]
    st_rows = lambda d: slice(d * RW_HEAD, (d + 1) * RW_HEAD)
    states = {(d, p): st_ref[st_rows(d), pair_sl(p)] for d, p in chains}
    for step in range(nsub):
        for d, p in chains:
            s = step if d == 0 else nsub - 1 - step
            o = by_item[(d, p, s)]
            state2 = stack(states[(d, p)])
            dir_refs[d][6][0, chunk_rows(s), pair_sl(p)] = (mm(o["rp"], state2)
                                                            + o["o0"]).astype(BF16)
            states[(d, p)] = mm(o["m_p"], state2) + o["n_p"]
    for d, p in chains:
        st_ref[st_rows(d), pair_sl(p)] = states[(d, p)]

    @pl.when(j == pl.num_programs(1) - 1)
    def _():
        sfin_ref[0] = st_ref[...]


def _rwkv_scan(st, s0):
    bsz, n_rows, w = st["r"].shape
    rows = SCAN_CHUNKS * CHUNK
    nblk = n_rows // rows
    fwd = pl.BlockSpec((1, rows, w), lambda b, j: (b, j, 0))
    bwd = pl.BlockSpec((1, rows, w), lambda b, j: (b, nblk - 1 - j, 0))
    sspec = pl.BlockSpec((1, 2 * RW_HEAD, w), lambda b, j: (b, 0, 0))
    row_shape = jax.ShapeDtypeStruct((bsz, n_rows, w), BF16)
    return pl.pallas_call(
        _rwkv_scan_kernel,
        grid=(bsz, nblk),
        in_specs=[fwd] * 6 + [bwd] * 6 + [sspec],
        out_specs=[fwd, bwd, sspec],
        out_shape=[row_shape, row_shape, jax.ShapeDtypeStruct((bsz, 2 * RW_HEAD, w), F32)],
        scratch_shapes=[pltpu.VMEM((2 * RW_HEAD, w), F32)],
        compiler_params=_cparams("parallel", "arbitrary"),
        name="rwkv_scan",
    )(st["r"], st["v"], st["kk"], st["lw"][0], st["b"][0], st["kd"][0],
      st["r"], st["v"], st["kk"], st["lw"][1], st["b"][1], st["kd"][1], s0)


def _even_out_kernel(x_ref, gate_ref, om_ref, of_ref, ob_ref, bonus_ref, g_ref, lnw_ref, lnb_ref,
                     seg_ref, w_ref, o_ref):
    seg = seg_ref[...]
    f32 = lambda ref: ref[0].astype(F32)
    o = f32(of_ref) + f32(ob_ref)
    mu = _mm(o, seg, mode="lhs2") * (1.0 / RW_HEAD)
    dlt = o - mu
    var = _mm(dlt * dlt, seg, mode="lhs2") * (1.0 / RW_HEAD)
    o_rw = dlt * lax.rsqrt(var + RW_GN_EPS) * lnw_ref[...] + lnb_ref[...] + f32(bonus_ref)
    g = f32(g_ref)
    z_m = (f32(om_ref) * _silu(g[:, :MLA_WIDTH])).astype(BF16)
    z_r = (o_rw * _silu(g[:, MLA_WIDTH:])).astype(BF16)
    y = (jnp.dot(z_m, w_ref[:MLA_WIDTH, :], preferred_element_type=F32)
         + jnp.dot(z_r, w_ref[MLA_WIDTH:, :], preferred_element_type=F32))
    o_ref[0] = x_ref[0] + gate_ref[0] * y


def _even_out(x, gate, o_mla, o_f, o_b, bonus, g, ln_w, ln_b, seg, w_out):
    bsz, n_rows, d = x.shape
    tm = min(STREAM_TILE, n_rows)
    rows = lambda width: pl.BlockSpec((1, tm, width), lambda b, i: (b, i, 0))
    full = lambda a: pl.BlockSpec(a.shape, lambda b, i: (0,) * a.ndim)
    return pl.pallas_call(
        _even_out_kernel,
        grid=(bsz, n_rows // tm),
        in_specs=[rows(d), pl.BlockSpec((1, 1, d), lambda b, i: (b, 0, 0)),
                  rows(MLA_WIDTH), rows(RW_WIDTH), rows(RW_WIDTH), rows(RW_WIDTH), rows(d),
                  full(ln_w), full(ln_b), full(seg), full(w_out)],
        out_specs=rows(d),
        out_shape=jax.ShapeDtypeStruct(x.shape, F32),
        compiler_params=_cparams("parallel", "arbitrary"),
        name="even_out",
    )(x, gate, o_mla, o_f, o_b, bonus, g, ln_w, ln_b, seg, w_out)


def _conv3_kernel(u_ref, prev_ref, next_ref, w_ref, b_ref, ve_ref, vo_ref, x0_ref, v_scr):
    ub = u_ref[0]
    u = ub.astype(F32)
    prev, nxt = _shifted_rows(ub, prev_ref, next_ref, on_mxu=True)
    cv = prev * w_ref[0:1, :] + u * w_ref[1:2, :] + nxt * w_ref[2:3, :] + b_ref[...]
    hw = HY_WIDTH
    x0_ref[0] = cv[:, :hw].astype(BF16)
    v_even, v_odd = _split_rows(cv[:, 2 * hw:] * cv[:, hw:2 * hw], v_scr)
    ve_ref[0] = v_even.astype(BF16)
    vo_ref[0] = v_odd.astype(BF16)


def _conv3(u, conv_w, conv_b):
    bsz, n_rows, width = u.shape
    tm = min(ROW_TILE, n_rows)
    main, prev, nxt = _halo_specs(tm, width, n_rows)
    half_spec = pl.BlockSpec((1, tm // 2, HY_WIDTH), lambda b, i: (b, i, 0))
    half_shape = jax.ShapeDtypeStruct((bsz, n_rows // 2, HY_WIDTH), BF16)
    return pl.pallas_call(
        _conv3_kernel,
        grid=(bsz, n_rows // tm),
        in_specs=[main, prev, nxt,
                  pl.BlockSpec(conv_w.shape, lambda b, i: (0, 0)),
                  pl.BlockSpec((1, width), lambda b, i: (0, 0))],
        out_specs=[half_spec, half_spec, pl.BlockSpec((1, tm, HY_WIDTH), lambda b, i: (b, i, 0))],
        out_shape=[half_shape, half_shape, jax.ShapeDtypeStruct((bsz, n_rows, HY_WIDTH), BF16)],
        scratch_shapes=[_row_split_scratch(tm, HY_WIDTH)],
        compiler_params=_cparams("parallel", "arbitrary"),
        name="hyena_conv3",
    )(u, u, u, conv_w, conv_b.reshape(1, width))


def _hy_hidden_kernel(w1_ref, b1_ref, w2_ref, b2_ref, freq_ref, h_ref):
    n_pos = h_ref.shape[0]
    pos = lax.broadcasted_iota(jnp.int32, (n_pos, 1), 0).astype(F32)
    lane = lax.broadcasted_iota(jnp.int32, (1, LANES), 1)
    band_idx = jnp.where(lane <= HY_BANDS, lane - 1, lane - 1 - HY_BANDS).astype(F32)
    band = 1e-4 + band_idx * ((HY_BANDS - 1 - 1e-4) / (HY_BANDS - 1))
    ang = pos * (2.0 * math.pi / n_pos) * band
    z = jnp.where(lane == 0, pos / (n_pos - 1),
                  jnp.where(lane <= HY_BANDS, jnp.cos(ang),
                            jnp.where(lane <= 2 * HY_BANDS, -jnp.sin(ang), 0.0)))
    freq = freq_ref[...]
    hdn = jnp.sin(freq * (_mm(z, w1_ref[...], mode="f32") + b1_ref[...]))
    for j in range(HY_INNER):
        hdn = jnp.sin(freq * (_mm(hdn, w2_ref[j], mode="f32") + b2_ref[j]))
    h_ref[...] = hdn


def _hy_filter_kernel(h_ref, w0_ref, w1_ref, dl_ref, fse_ref, fso_ref, fde_ref, fdo_ref, scr):
    n_pos = h_ref.shape[0]
    pos = lax.broadcasted_iota(jnp.int32, (n_pos, 1), 0)
    t = pos.astype(F32) / (n_pos - 1)
    dec = jnp.exp(-t * dl_ref[...])
    hdn = h_ref[...]
    f_fwd = _mm(hdn, w0_ref[...], mode="f32") * dec
    f_bwd = jnp.where(pos == 0, 0.0, _mm(hdn, w1_ref[...], mode="f32") * dec)
    inv = 1.0 / (jnp.sum(jnp.abs(f_fwd), axis=0, keepdims=True)
                 + jnp.sum(jnp.abs(f_bwd), axis=0, keepdims=True))
    fse_ref[...], fso_ref[...] = _split_rows((f_fwd + f_bwd) * inv, scr)
    fde_ref[...], fdo_ref[...] = _split_rows((f_bwd - f_fwd) * inv, scr)


def _hyena_filters(n_pos, f_w1, f_b1, f_w2, f_b2, f_wout, freq, deltas):
    w1 = jnp.zeros((LANES, HY_ORDER), F32).at[:HY_EMB].set(f_w1)
    hdn = pl.pallas_call(
        _hy_hidden_kernel,
        out_shape=jax.ShapeDtypeStruct((n_pos, HY_ORDER), F32),
        name="hyena_filter_hidden",
    )(w1, f_b1.reshape(1, HY_ORDER), f_w2, f_b2.reshape(HY_INNER, 1, HY_ORDER),
      freq.reshape(1, HY_ORDER))
    tn = 256
    cspec = pl.BlockSpec((HY_ORDER, tn), lambda j: (0, j))
    ospec = pl.BlockSpec((n_pos // 2, tn), lambda j: (0, j))
    oshape = jax.ShapeDtypeStruct((n_pos // 2, HY_WIDTH), F32)
    return pl.pallas_call(
        _hy_filter_kernel,
        grid=(HY_WIDTH // tn,),
        in_specs=[pl.BlockSpec((n_pos, HY_ORDER), lambda j: (0, 0)), cspec, cspec,
                  pl.BlockSpec((1, tn), lambda j: (0, j))],
        out_specs=[ospec] * 4,
        out_shape=[oshape] * 4,
        scratch_shapes=[_row_split_scratch(n_pos, tn)],
        compiler_params=_cparams("arbitrary"),
        name="hyena_filter",
    )(hdn, f_wout[:, :HY_WIDTH], f_wout[:, HY_WIDTH:], deltas)


def _dft_tables_kernel(ec_ref, es_ref, oc_ref, os_ref, oct_ref, ost_ref, base_scr):
    tr, kq = ec_ref.shape
    n = 4 * kq
    r0 = pl.program_id(0) * tr

    def cos_sin(phase):
        ph = phase & (n - 1)
        ph = jnp.where(ph >= n // 2, ph - n, ph)
        ang = ph.astype(F32) * (2.0 * math.pi / n)
        return jnp.cos(ang), jnp.sin(ang)

    @pl.when(pl.program_id(0) == 0)
    def _():
        rr = lax.broadcasted_iota(jnp.int32, (tr, kq), 0)
        c = lax.broadcasted_iota(jnp.int32, (tr, kq), 1)
        for i, phase in enumerate((2 * rr * c, rr * (2 * c + 1), c * (2 * rr + 1))):
            base_scr[2 * i], base_scr[2 * i + 1] = cos_sin(phase)

    c1 = lax.broadcasted_iota(jnp.int32, (1, kq), 1)
    cos_e, sin_e = cos_sin(2 * r0 * c1)
    cos_o, sin_o = cos_sin(r0 * (2 * c1 + 1))
    outs = ((ec_ref, es_ref, cos_e, sin_e), (oc_ref, os_ref, cos_o, sin_o),
            (oct_ref, ost_ref, cos_e, sin_e))
    for i, (c_ref, s_ref, cos_a, sin_a) in enumerate(outs):
        cos_r, sin_r = base_scr[2 * i], base_scr[2 * i + 1]
        c_ref[...] = (cos_a * cos_r - sin_a * sin_r).astype(BF16)
        s_ref[...] = (sin_a * cos_r + cos_a * sin_r).astype(BF16)


def _dft_tables(n_pos):
    kq = n_pos // 2
    tr = min(256, kq)
    spec = pl.BlockSpec((tr, kq), lambda i: (i, 0))
    shape = jax.ShapeDtypeStruct((kq, kq), BF16)
    return pl.pallas_call(
        _dft_tables_kernel,
        grid=(kq // tr,),
        out_specs=[spec] * 6,
        out_shape=[shape] * 6,
        scratch_shapes=[pltpu.VMEM((6, tr, kq), F32)],
        compiler_params=_cparams("arbitrary"),
        name="dft_tables",
    )()


def _alt_sign(n_pos):
    pos = lax.broadcasted_iota(jnp.int32, (n_pos, 1), 0)
    return jnp.where((pos & 1) == 0, 1.0, -1.0), pos


def _spectrum_kernel(fse_ref, fso_ref, fde_ref, fdo_ref, ec_ref, es_ref, oc_ref, os_ref,
                     hare_ref, haim_ref, hbre_ref, hbim_ref, hk_ref):
    kq = fse_ref.shape[0]
    alt, pos = _alt_sign(kq)

    def mm2(tab_ref, f):
        fh = f.astype(BF16)
        fl = (f - fh.astype(F32)).astype(BF16)
        return (jnp.dot(tab_ref[...], fh, preferred_element_type=F32)
                + jnp.dot(tab_ref[...], fl, preferred_element_type=F32))

    fse, fdo = fse_ref[...], fdo_ref[...]
    ce, co = mm2(ec_ref, fse), mm2(oc_ref, fso_ref[...])
    se, so = mm2(es_ref, fde_ref[...]), mm2(os_ref, fdo)
    scale = 0.5 / kq
    scale_re = jnp.where(pos == 0, 0.5 * scale, scale)
    hare_ref[...] = (ce + co) * scale_re
    hbre_ref[...] = (ce - co) * scale_re
    haim_ref[...] = (se + so) * scale
    hbim_ref[...] = (so - se) * scale
    hk_ref[0:1, :] = jnp.sum(fse * alt, axis=0, keepdims=True) * scale
    hk_ref[1:2, :] = jnp.sum(fdo * alt, axis=0, keepdims=True) * scale


def _table_specs(tabs, nd):
    imap = (lambda j: (0, 0)) if nd == 1 else (lambda b, j: (0, 0))
    return [pl.BlockSpec(t.shape, imap, pipeline_mode=pl.Buffered(1)) for t in tabs]


def _spectrum(filters, tabs):
    kq, width = filters[0].shape
    tc = DFT_TC
    half = pl.BlockSpec((kq, tc), lambda j: (0, j))
    return pl.pallas_call(
        _spectrum_kernel,
        grid=(width // tc,),
        in_specs=[half] * 4 + _table_specs(tabs[:4], 1),
        out_specs=[half] * 4 + [pl.BlockSpec((2, tc), lambda j: (0, j))],
        out_shape=[jax.ShapeDtypeStruct((kq, width), F32)] * 4
        + [jax.ShapeDtypeStruct((2, width), F32)],
        compiler_params=_cparams("arbitrary"),
        name="hyena_spectrum",
    )(*filters, *tabs[:4])


def _dft_conv_kernel(ve_ref, vo_ref, hare_ref, haim_ref, hbre_ref, hbim_ref, hk_ref,
                     ec_ref, es_ref, oc_ref, os_ref, oct_ref, ost_ref, ye_ref, yo_ref,
                     pe_scr, qe_scr, po_scr, qo_scr):
    kq, tc = ve_ref.shape[1], ve_ref.shape[2]
    tm = min(DFT_TM, kq)
    alt, _ = _alt_sign(kq)
    ve, vo = ve_ref[0], vo_ref[0]
    dot = functools.partial(jnp.dot, preferred_element_type=F32)
    for r0 in range(0, kq, tm):
        rows = slice(r0, r0 + tm)
        ce, co = dot(ec_ref[rows, :], ve), dot(oc_ref[rows, :], vo)
        se, so = dot(es_ref[rows, :], ve), dot(os_ref[rows, :], vo)
        va_re, va_s, vb_re, vb_s = ce + co, se + so, ce - co, so - se
        hare, haim = hare_ref[rows, :], haim_ref[rows, :]
        hbre, hbim = hbre_ref[rows, :], hbim_ref[rows, :]
        ya_re, ya_s = va_re * hare + va_s * haim, va_s * hare - va_re * haim
        yb_re, yb_s = vb_re * hbre + vb_s * hbim, vb_s * hbre - vb_re * hbim
        pe_scr[rows, :] = (ya_re + yb_re).astype(BF16)
        qe_scr[rows, :] = (ya_s - yb_s).astype(BF16)
        po_scr[rows, :] = (ya_re - yb_re).astype(BF16)
        qo_scr[rows, :] = (ya_s + yb_s).astype(BF16)
    vk_re = jnp.sum(ve.astype(F32) * alt, axis=0, keepdims=True)
    vk_s = jnp.sum(vo.astype(F32) * alt, axis=0, keepdims=True)
    hk_re, hk_im = hk_ref[0:1, :], hk_ref[1:2, :]
    yk_re = vk_re * hk_re + vk_s * hk_im
    yk_s = vk_s * hk_re - vk_re * hk_im
    pe, qe, po, qo = pe_scr[...], qe_scr[...], po_scr[...], qo_scr[...]
    for r0 in range(0, kq, tm):
        rows = slice(r0, r0 + tm)
        ye_ref[0, rows, :] = (dot(ec_ref[rows, :], pe) + dot(es_ref[rows, :], qe)
                              + alt[rows] * yk_re).astype(BF16)
        yo_ref[0, rows, :] = (dot(oct_ref[rows, :], po) + dot(ost_ref[rows, :], qo)
                              + alt[rows] * yk_s).astype(BF16)


def _dft_conv(v_even, v_odd, spec, tabs):
    bsz, kq, width = v_even.shape
    tc = DFT_TC
    half = pl.BlockSpec((1, kq, tc), lambda b, j: (b, 0, j))
    col = pl.BlockSpec((kq, tc), lambda b, j: (0, j))
    return pl.pallas_call(
        _dft_conv_kernel,
        grid=(bsz, width // tc),
        in_specs=[half, half, col, col, col, col, pl.BlockSpec((2, tc), lambda b, j: (0, j))]
        + _table_specs(tabs, 2),
        out_specs=[half, half],
        out_shape=[jax.ShapeDtypeStruct(v_even.shape, BF16)] * 2,
        scratch_shapes=[pltpu.VMEM((kq, tc), BF16)] * 4,
        compiler_params=_cparams("parallel", "arbitrary"),
        name="hyena_dft_conv",
    )(v_even, v_odd, *spec, *tabs)


def _hy_out_kernel(x_ref, gate_ref, ye_ref, yo_ref, ve_ref, vo_ref, x0_ref, g_ref, bias_ref, w_ref,
                   o_ref, t_scr):
    f32 = lambda ref: ref[0].astype(F32)
    bias = bias_ref[...]
    t = _merge_rows(f32(ye_ref) + f32(ve_ref) * bias, f32(yo_ref) + f32(vo_ref) * bias, t_scr)
    z = t * f32(x0_ref) * _silu(f32(g_ref))
    o_ref[0] = x_ref[0] + gate_ref[0] * jnp.dot(z.astype(BF16), w_ref[...],
                                                preferred_element_type=F32)


def _hy_out(x, gate, y_even, y_odd, v_even, v_odd, x0, g, bias_d, w_out):
    bsz, n_rows, d = x.shape
    tm = min(STREAM_TILE, n_rows)
    rows = pl.BlockSpec((1, tm, d), lambda b, i: (b, i, 0))
    half = pl.BlockSpec((1, tm // 2, d), lambda b, i: (b, i, 0))
    return pl.pallas_call(
        _hy_out_kernel,
        grid=(bsz, n_rows // tm),
        in_specs=[rows, pl.BlockSpec((1, 1, d), lambda b, i: (b, 0, 0)), half, half, half, half,
                  rows, rows, pl.BlockSpec((1, d), lambda b, i: (0, 0)),
                  pl.BlockSpec(w_out.shape, lambda b, i: (0, 0))],
        out_specs=rows,
        out_shape=jax.ShapeDtypeStruct(x.shape, F32),
        scratch_shapes=[_row_split_scratch(tm, d)],
        compiler_params=_cparams("parallel", "arbitrary"),
        name="hyena_out",
    )(x, gate, y_even, y_odd, v_even, v_odd, x0, g, bias_d.reshape(1, d), w_out)


def _rope_perm():
    pairs = np.arange(MLA_ROPE // 2)
    return np.concatenate([np.arange(MLA_NOPE), MLA_NOPE + 2 * pairs, MLA_NOPE + 2 * pairs + 1])


def _rope_tables(n_pos):
    rows = n_pos // GRID_W
    row = jnp.repeat(jnp.arange(rows, dtype=F32), GRID_W)
    col = jnp.tile(jnp.arange(GRID_W, dtype=F32), rows)
    n_freq = MLA_ROPE // 4
    inv = ROPE_BASE ** (-jnp.arange(n_freq, dtype=F32) / n_freq)
    ang = jnp.concatenate([row[:, None] * inv, col[:, None] * inv], axis=-1)
    cos, sin = jnp.cos(ang), jnp.sin(ang)
    half = MLA_ROPE // 2
    ones = jnp.ones((n_pos, MLA_NOPE), F32)
    zeros = jnp.zeros((n_pos, MLA_NOPE), F32)
    pad1 = jnp.ones((n_pos, HEAD_PAD - MLA_QK), F32)
    pad0 = jnp.zeros((n_pos, HEAD_PAD - MLA_QK), F32)
    z16 = jnp.zeros((n_pos, half), F32)
    cos_f = jnp.concatenate([ones, cos, cos, pad1], axis=-1)
    sin_a = jnp.concatenate([zeros, z16, sin, pad0], axis=-1)
    sin_b = jnp.concatenate([zeros, -sin, z16, pad0], axis=-1)
    return cos_f, sin_a, sin_b


def _pad_heads(w, width):
    k, h, _ = w.shape
    return jnp.zeros((k, h, HEAD_PAD), w.dtype).at[:, :, :width].set(w).reshape(k, h * HEAD_PAD)


def _even_weights(e, ev_w_in, ev_w_out, mla_q_a_norm, mla_w_uq, mla_kv_a_norm, mla_w_ukv,
                  mla_q_norm, mla_k_norm, rwkv_mu_prev, rwkv_mu_next, rwkv_w0, rwkv_w_up, rwkv_a0,
                  rwkv_a_up, rwkv_k_k, rwkv_k_a, rwkv_r_k, rwkv_ln_w, rwkv_ln_b):
    perm = _rope_perm()
    w_in = ev_w_in[e]
    d = w_in.shape[0]
    o_dkv = EV_DQ
    o_rw = EV_DQ + EV_DKV
    o_g = o_rw + RW_SHIFT
    w_dkv = jnp.zeros((d, 2 * LANES), F32)
    w_dkv = w_dkv.at[:, :MLA_KV_LORA].set(w_in[:, o_dkv:o_dkv + MLA_KV_LORA])
    rope_cols = o_dkv + MLA_KV_LORA + (perm[MLA_NOPE:] - MLA_NOPE)
    w_dkv = w_dkv.at[:, LANES + MLA_NOPE:LANES + MLA_QK].set(w_in[:, rope_cols])
    proj = [w_in[:, :EV_DQ].astype(BF16), w_dkv.astype(BF16),
            w_in[:, o_rw:o_g].astype(BF16), w_in[:, o_g:].astype(BF16)]
    w_ukv = mla_w_ukv[e]
    pad_gain = lambda g: jnp.zeros((1, HEAD_PAD), F32).at[0, :MLA_QK].set(g[perm])
    seg_id = np.arange(RW_WIDTH) // RW_HEAD
    zero_up = lambda up, dd: jnp.zeros((2 * RW_LORA_W, RW_WIDTH), F32).at[
        dd * RW_LORA_W:(dd + 1) * RW_LORA_W].set(up[dd])
    rw = dict(
        mu_prev=rwkv_mu_prev[e].reshape(1, RW_SHIFT), mu_next=rwkv_mu_next[e].reshape(1, RW_SHIFT),
        w0=rwkv_w0[e], a0=rwkv_a0[e],
        w_up=jnp.stack([zero_up(rwkv_w_up[e], 0), zero_up(rwkv_w_up[e], 1)]),
        a_up=jnp.stack([zero_up(rwkv_a_up[e], 0), zero_up(rwkv_a_up[e], 1)]),
        k_k=rwkv_k_k[e].reshape(1, RW_WIDTH), k_a=rwkv_k_a[e].reshape(1, RW_WIDTH),
        r_k=rwkv_r_k[e].reshape(2, RW_WIDTH),
        seg=jnp.asarray((seg_id[:, None] == seg_id[None, :]).astype(np.float32)),
    )
    head_par = np.arange(MLA_HEADS) % 2
    v_lane = (np.arange(HEAD_PAD)[None, :] >= MLA_V) == (head_par[:, None] == 1)
    v_ones = (~v_lane).astype(np.float32)
    w_vh = w_ukv[:, :, MLA_NOPE:].reshape(MLA_KV_LORA, MLA_HEADS // 2, 2, MLA_V)
    zero_v = jnp.zeros_like(w_vh[:, :, 0])
    w_v = jnp.stack([w_vh[:, :, 0], zero_v, zero_v, w_vh[:, :, 1]], axis=2)
    return dict(
        proj=proj,
        q_a_norm=mla_q_a_norm[e], kv_a_norm=mla_kv_a_norm[e],
        w_uq=_pad_heads(mla_w_uq[e][:, :, perm], MLA_QK).astype(BF16),
        w_k=_pad_heads(w_ukv[:, :, :MLA_NOPE], MLA_NOPE).astype(BF16),
        w_v=w_v.reshape(MLA_KV_LORA, MLA_HEADS * HEAD_PAD).astype(BF16),
        v_ones=jnp.asarray(v_ones.reshape(1, MLA_HEADS * HEAD_PAD)),
        q_gain=pad_gain(mla_q_norm[e]) * (MLA_QK ** -0.5 * math.log2(math.e)),
        k_gain=pad_gain(mla_k_norm[e]),
        attn_bound=(1.01 * MLA_QK * (MLA_QK ** -0.5 * math.log2(math.e))
                    * jnp.max(jnp.abs(mla_q_norm[e])) * jnp.max(jnp.abs(mla_k_norm[e]))
                    ).reshape(1).astype(F32),
        rw=rw,
        ln_w=rwkv_ln_w[e].reshape(1, RW_WIDTH), ln_b=rwkv_ln_b[e].reshape(1, RW_WIDTH),
        w_out=ev_w_out[e].astype(BF16),
    )


def _rwkv_branch(st_c, st, want_ctx):
    bsz = st["r"].shape[0]
    zero_state = jnp.zeros((bsz, 2 * RW_HEAD, RW_WIDTH), F32)
    of_c, ob_c, s_c = _rwkv_scan(st_c, zero_state)
    o_f, o_b, _ = _rwkv_scan(st, s_c)
    return [o_f, o_b], ([of_c, ob_c] if want_ctx else None)


def _even_layer(x, xc, mod, mod_c, g_norm, wts, rope_tabs, ctx_out):
    shift, scale1p, gate = mod
    shift_c, scale1p_c, gate_c = mod_c
    q, k, v, p_rw, g = _even_in(x, g_norm, scale1p, shift, wts, rope_tabs)
    qc, kc, vc, pc_rw, gc = _even_in(xc, g_norm, scale1p_c, shift_c, wts, None)
    o_mla = _attention(q, [k, kc], [v, vc], wts["attn_bound"])
    st_c = _rw_prep(pc_rw, wts["rw"])
    st = _rw_prep(p_rw, wts["rw"])
    outs, outs_c = _rwkv_branch(st_c, st, ctx_out)
    seg = wts["rw"]["seg"]
    x_new = _even_out(x, gate, o_mla, outs[0], outs[1], st["bonus"], g, wts["ln_w"], wts["ln_b"],
                      seg, wts["w_out"])
    if not ctx_out:
        return x_new, None
    oc_mla = _attention(qc, [kc], [vc], wts["attn_bound"])
    xc_new = _even_out(xc, gate_c, oc_mla, outs_c[0], outs_c[1], st_c["bonus"], gc, wts["ln_w"],
                       wts["ln_b"], seg, wts["w_out"])
    return x_new, xc_new


def _hyena_layer(x, mod, g_norm, wts, tables):
    shift, scale1p, gate = mod
    n_pos = x.shape[1]
    u, g = _norm_proj(x, g_norm, scale1p, shift, wts["proj"])
    v_even, v_odd, x0 = _conv3(u, wts["conv_w"], wts["conv_b"])
    spec = _spectrum(_hyena_filters(n_pos, *wts["filt"]), tables)
    y_even, y_odd = _dft_conv(v_even, v_odd, spec, tables)
    return _hy_out(x, gate, y_even, y_odd, v_even, v_odd, x0, g, wts["bias_d"], wts["w_out"])


def kernel(x, c, ctx, c_ctx, mod_w, mod_b, norm_g, ev_w_in, ev_w_out, mla_q_a_norm, mla_w_uq, mla_kv_a_norm, mla_w_ukv, mla_q_norm, mla_k_norm, rwkv_mu_prev, rwkv_mu_next, rwkv_w0, rwkv_w_up, rwkv_a0, rwkv_a_up, rwkv_k_k, rwkv_k_a, rwkv_r_k, rwkv_ln_w, rwkv_ln_b, od_w_in, od_w_out, hy_conv_w, hy_conv_b, hy_bias_d, hy_f_w1, hy_f_b1, hy_f_w2, hy_f_b2, hy_f_wout, hy_freq):
    bsz, n_lat, d = x.shape
    n_ctx = ctx.shape[1]
    scan_rows = SCAN_CHUNKS * CHUNK
    assert n_lat % max(scan_rows, GRID_W) == 0 and n_ctx % scan_rows == 0 and d == D_MODEL
    assert CHUNK == RW_HEAD and 2 * CHUNK == RW_PAIR

    n_rows = -(-(bsz + 1) // 16) * 16
    cvec = jnp.zeros((n_rows, d), F32).at[:bsz].set(c).at[bsz].set(c_ctx)
    mods = _modulation(cvec, mod_w, mod_b)

    def split_mod(i, lo, hi, reps):
        m = mods[i, lo:hi]
        m = jnp.broadcast_to(m, (reps, 3 * d)) if hi - lo == 1 else m
        m = m[:, None, :]
        return m[..., :d], 1.0 + m[..., d:2 * d], m[..., 2 * d:]

    rope_tabs = _rope_tables(n_lat)
    deltas = jnp.abs(jnp.linspace(math.log(HY_TARGET) / HY_FAST_DECAY,
                                  math.log(HY_TARGET) / HY_SLOW_DECAY, HY_WIDTH,
                                  dtype=F32)).reshape(1, HY_WIDTH)
    tables = {n_lat: _dft_tables(n_lat)}

    xc = ctx
    for i in range(DEPTH):
        ctx_needed_later = any(j > i and j % 2 == 0 for j in range(DEPTH))
        mod = split_mod(i, 0, bsz, bsz)
        mod_c = split_mod(i, bsz, bsz + 1, bsz)
        if i % 2 == 0:
            wts = _even_weights(i // 2, ev_w_in, ev_w_out, mla_q_a_norm, mla_w_uq, mla_kv_a_norm,
                                mla_w_ukv, mla_q_norm, mla_k_norm, rwkv_mu_prev, rwkv_mu_next,
                                rwkv_w0, rwkv_w_up, rwkv_a0, rwkv_a_up, rwkv_k_k, rwkv_k_a,
                                rwkv_r_k, rwkv_ln_w, rwkv_ln_b)
            x, xc_new = _even_layer(x, xc, mod, mod_c, norm_g[i], wts, rope_tabs, ctx_needed_later)
            xc = xc_new if ctx_needed_later else xc
        else:
            o = i // 2
            w_in = od_w_in[o]
            wts = dict(
                proj=[w_in[:, :3 * HY_WIDTH].astype(BF16), w_in[:, 3 * HY_WIDTH:].astype(BF16)],
                conv_w=hy_conv_w[o], conv_b=hy_conv_b[o], bias_d=hy_bias_d[o],
                filt=(hy_f_w1[o], hy_f_b1[o], hy_f_w2[o], hy_f_b2[o], hy_f_wout[o], hy_freq[o],
                      deltas),
                w_out=od_w_out[o].astype(BF16),
            )
            if ctx_needed_later:
                if n_ctx not in tables:
                    tables[n_ctx] = _dft_tables(n_ctx)
                xc = _hyena_layer(xc, mod_c, norm_g[i], wts, tables[n_ctx])
            x = _hyena_layer(x, mod, norm_g[i], wts, tables[n_lat])
    return x
```
